```python
import math
import jax, jax.numpy as jnp
from jax import lax
import numpy as np

D_MODEL = 1024
BATCH = 16
SEQ = 4096
DEPTH = 4
DEC_BATCH = 4
DEC_SEQ = 4096
PAST_LEN = 128

HEAD_DIM = 64
N_Q_HEADS = 16
N_KV_HEADS = 4
GROUP = N_Q_HEADS // N_KV_HEADS
Q_WIDTH = N_Q_HEADS * HEAD_DIM
KV_WIDTH = N_KV_HEADS * HEAD_DIM
QKV_WIDTH = Q_WIDTH + 2 * KV_WIDTH
GRID_W = 64
ROPE_THETA = 10000.0
Q_BLOCK = 128
WINDOW = 128
N_BUCKETS = 32
MAX_DISTANCE = 128
D_FF = 2816
N_EXPERTS = 8
TOP_K = 2
D_FF_EXPERT = 3584
EPS = 1e-6
NEG_INF = -1e30
N_A_LAYERS = (DEPTH + 1) // 2
N_B_LAYERS = DEPTH // 2

kernel_name = "hybrid_axial_window_sink_moe_encoder"


def rmsnorm(x, g):
    xf = x.astype(jnp.float32)
    y = xf * lax.rsqrt(jnp.mean(xf * xf, axis=-1, keepdims=True) + EPS)
    return (y * g.astype(jnp.float32)).astype(x.dtype)


def split_qkv(x, w_qkv):
    b, s, _ = x.shape
    qkv = x @ w_qkv
    q = qkv[..., :Q_WIDTH].reshape(b, s, N_KV_HEADS, GROUP, HEAD_DIM)
    k = qkv[..., Q_WIDTH:Q_WIDTH + KV_WIDTH].reshape(b, s, N_KV_HEADS, HEAD_DIM)
    v = qkv[..., Q_WIDTH + KV_WIDTH:].reshape(b, s, N_KV_HEADS, HEAD_DIM)
    return q, k, v


def axial_rope_tables(seq):
    rows = seq // GRID_W
    row = jnp.broadcast_to(jnp.arange(rows)[:, None], (rows, GRID_W)).reshape(seq).astype(jnp.float32)
    col = jnp.broadcast_to(jnp.arange(GRID_W)[None, :], (rows, GRID_W)).reshape(seq).astype(jnp.float32)
    half = HEAD_DIM // 2
    inv_freq = ROPE_THETA ** (-jnp.arange(0, half, 2, dtype=jnp.float32) / half)
    ang = jnp.concatenate([row[:, None] * inv_freq[None, :], col[:, None] * inv_freq[None, :]], axis=-1)
    return jnp.cos(ang), jnp.sin(ang)


def apply_rope(x, cos, sin):
    xf = x.astype(jnp.float32).reshape(x.shape[:-1] + (HEAD_DIM // 2, 2))
    x0, x1 = xf[..., 0], xf[..., 1]
    shape = (1, cos.shape[0]) + (1,) * (x.ndim - 3) + (HEAD_DIM // 2,)
    c, s_ = cos.reshape(shape), sin.reshape(shape)
    out = jnp.stack([x0 * c - x1 * s_, x0 * s_ + x1 * c], axis=-1).reshape(x.shape)
    return out.astype(x.dtype)


def global_axial_attention(x, w_qkv, q_gain, k_gain, w_o):
    b, s, _ = x.shape
    q, k, v = split_qkv(x, w_qkv)
    q = rmsnorm(q, q_gain)
    k = rmsnorm(k, k_gain)
    cos, sin = axial_rope_tables(s)
    q = apply_rope(q, cos, sin)
    k = apply_rope(k, cos, sin)
    scale = HEAD_DIM ** -0.5
    nb = s // Q_BLOCK
    q_blocks = q.reshape(b, nb, Q_BLOCK, N_KV_HEADS, GROUP, HEAD_DIM).transpose(1, 0, 2, 3, 4, 5)

    def attend(qb):
        logits = jnp.einsum("bqhgd,bkhd->bhgqk", qb, k, preferred_element_type=jnp.float32) * scale
        p = jax.nn.softmax(logits, axis=-1)
        return jnp.einsum("bhgqk,bkhd->bqhgd", p.astype(v.dtype), v)

    o = lax.map(attend, q_blocks)
    o = o.transpose(1, 0, 2, 3, 4, 5).reshape(b, s, Q_WIDTH)
    return o @ w_o


def t5_bucket(rel):
    half = N_BUCKETS // 2
    ret = jnp.where(rel > 0, half, 0)
    n = jnp.abs(rel)
    max_exact = half // 2
    nf = jnp.maximum(n, 1).astype(jnp.float32)
    large = max_exact + (jnp.log(nf / max_exact) / math.log(MAX_DISTANCE / max_exact) * (half - max_exact)).astype(jnp.int32)
    large = jnp.minimum(large, half - 1)
    return ret + jnp.where(n < max_exact, n, large)


def windowed_sink_attention(x, w_qkv, sinks, rel_bias, w_o):
    b, s, _ = x.shape
    q, k, v = split_qkv(x, w_qkv)
    scale = HEAD_DIM ** -0.5
    nb = s // Q_BLOCK
    span = Q_BLOCK + 2 * WINDOW
    pad = ((0, 0), (WINDOW, WINDOW), (0, 0), (0, 0))
    kp = jnp.pad(k, pad).reshape(b, nb + 2, Q_BLOCK, N_KV_HEADS, HEAD_DIM).transpose(1, 0, 2, 3, 4)
    vp = jnp.pad(v, pad).reshape(b, nb + 2, Q_BLOCK, N_KV_HEADS, HEAD_DIM).transpose(1, 0, 2, 3, 4)
    k_band = jnp.concatenate([kp[:-2], kp[1:-1], kp[2:]], axis=2)
    v_band = jnp.concatenate([vp[:-2], vp[1:-1], vp[2:]], axis=2)
    q_blocks = q.reshape(b, nb, Q_BLOCK, N_KV_HEADS, GROUP, HEAD_DIM).transpose(1, 0, 2, 3, 4, 5)

    r = jnp.arange(Q_BLOCK)[:, None]
    c = jnp.arange(span)[None, :]
    rel = c - WINDOW - r
    bias = rel_bias[t5_bucket(rel)].astype(jnp.float32)
    bias = bias.transpose(2, 0, 1).reshape(N_KV_HEADS, GROUP, Q_BLOCK, span)
    key_pos = jnp.arange(nb)[:, None, None] * Q_BLOCK + c[None] - WINDOW
    valid = (jnp.abs(rel) <= WINDOW)[None] & (key_pos >= 0) & (key_pos < s)
    sink = sinks.astype(jnp.float32).reshape(N_KV_HEADS, GROUP)[None, :, :, None, None]

    def attend(args):
        qb, kb, vb, vmask = args
        logits = jnp.einsum("bqhgd,bkhd->bhgqk", qb, kb, preferred_element_type=jnp.float32) * scale + bias[None]
        logits = jnp.where(vmask[None, None, None], logits, NEG_INF)
        m = jnp.maximum(jnp.max(logits, axis=-1, keepdims=True), sink)
        e = jnp.exp(logits - m)
        p = e / (jnp.sum(e, axis=-1, keepdims=True) + jnp.exp(sink - m))
        return jnp.einsum("bhgqk,bkhd->bqhgd", p.astype(vb.dtype), vb)

    o = lax.map(attend, (q_blocks, k_band, v_band, valid))
    o = o.transpose(1, 0, 2, 3, 4, 5).reshape(b, s, Q_WIDTH)
    return o @ w_o


def swiglu(x, w_in, w_out):
    d_ff = w_out.shape[0]
    gu = x @ w_in
    return (jax.nn.silu(gu[..., :d_ff]) * gu[..., d_ff:]) @ w_out


def moe_swiglu(x, w_router, w_exp_in, w_exp_out):
    b, s, d = x.shape
    xt = x.reshape(b * s, d)
    logits = (xt @ w_router).astype(jnp.float32)
    top_vals, top_idx = lax.top_k(logits, TOP_K)
    gates = jax.nn.softmax(top_vals, axis=-1)
    combine = jnp.sum(jax.nn.one_hot(top_idx, N_EXPERTS, dtype=jnp.float32) * gates[..., None], axis=1)
    out = jnp.zeros((b * s, d), jnp.float32)
    for e in range(N_EXPERTS):
        out = out + combine[:, e:e + 1] * swiglu(xt, w_exp_in[e], w_exp_out[e]).astype(jnp.float32)
    return out.astype(x.dtype).reshape(b, s, d)


def trunk(x, norm_mix, norm_ffn, norm_final, w_qkv_a, q_gain_a, k_gain_a, w_o_a,
          w_qkv_b, sink_b, w_o_b, rel_bias, w_ff_in, w_ff_out, w_router, w_exp_in, w_exp_out):
    for i in range(DEPTH):
        j = i // 2
        h = rmsnorm(x, norm_mix[i])
        if i % 2 == 0:
            x = x + global_axial_attention(h, w_qkv_a[j], q_gain_a[j], k_gain_a[j], w_o_a[j])
        else:
            x = x + windowed_sink_attention(h, w_qkv_b[j], sink_b[j], rel_bias, w_o_b[j])
        h = rmsnorm(x, norm_ffn[i])
        if i % 2 == 0:
            x = x + swiglu(h, w_ff_in[j], w_ff_out[j])
        else:
            x = x + moe_swiglu(h, w_router[j], w_exp_in[j], w_exp_out[j])
    return rmsnorm(x, norm_final)


def setup_inputs(seed: int = 0) -> dict:
    key = jax.random.key(seed)
    ks = jax.random.split(key, 20)
    f32 = jnp.float32
    nrm = lambda k, shape, sc: jax.random.normal(k, shape, f32) * sc
    out_scale = (2.0 * DEPTH) ** -0.5
    return {
        "x_prompt": nrm(ks[0], (BATCH, SEQ, D_MODEL), 1.0),
        "x_sample": nrm(ks[1], (DEC_BATCH, DEC_SEQ, D_MODEL), 1.0),
        "norm_mix": 1.0 + nrm(ks[2], (DEPTH, D_MODEL), 0.02),
        "norm_ffn": 1.0 + nrm(ks[3], (DEPTH, D_MODEL), 0.02),
        "norm_final": 1.0 + nrm(ks[4], (D_MODEL,), 0.02),
        "w_qkv_a": nrm(ks[5], (N_A_LAYERS, D_MODEL, QKV_WIDTH), D_MODEL ** -0.5),
        "q_gain_a": 1.0 + nrm(ks[6], (N_A_LAYERS, HEAD_DIM), 0.02),
        "k_gain_a": 1.0 + nrm(ks[7], (N_A_LAYERS, HEAD_DIM), 0.02),
        "w_o_a": nrm(ks[8], (N_A_LAYERS, Q_WIDTH, D_MODEL), Q_WIDTH ** -0.5 * out_scale),
        "w_qkv_b": nrm(ks[9], (N_B_LAYERS, D_MODEL, QKV_WIDTH), D_MODEL ** -0.5),
        "sink_b": nrm(ks[10], (N_B_LAYERS, N_Q_HEADS), 0.5),
        "w_o_b": nrm(ks[11], (N_B_LAYERS, Q_WIDTH, D_MODEL), Q_WIDTH ** -0.5 * out_scale),
        "rel_bias": nrm(ks[12], (N_BUCKETS, N_Q_HEADS), 0.5),
        "w_ff_in": nrm(ks[13], (N_A_LAYERS, D_MODEL, 2 * D_FF), D_MODEL ** -0.5),
        "w_ff_out": nrm(ks[14], (N_A_LAYERS, D_FF, D_MODEL), D_FF ** -0.5 * out_scale),
        "w_router": nrm(ks[15], (N_B_LAYERS, D_MODEL, N_EXPERTS), D_MODEL ** -0.5),
        "w_exp_in": nrm(ks[16], (N_B_LAYERS, N_EXPERTS, D_MODEL, 2 * D_FF_EXPERT), D_MODEL ** -0.5),
        "w_exp_out": nrm(ks[17], (N_B_LAYERS, N_EXPERTS, D_FF_EXPERT, D_MODEL), D_FF_EXPERT ** -0.5 * out_scale),
    }


def reference(x_prompt, x_sample, norm_mix, norm_ffn, norm_final, w_qkv_a, q_gain_a, k_gain_a, w_o_a,
              w_qkv_b, sink_b, w_o_b, rel_bias, w_ff_in, w_ff_out, w_router, w_exp_in, w_exp_out):
    y_prompt = trunk(x_prompt, norm_mix, norm_ffn, norm_final, w_qkv_a, q_gain_a, k_gain_a, w_o_a,
                     w_qkv_b, sink_b, w_o_b, rel_bias, w_ff_in, w_ff_out, w_router, w_exp_in, w_exp_out)
    y_sample = trunk(x_sample, norm_mix, norm_ffn, norm_final, w_qkv_a, q_gain_a, k_gain_a, w_o_a,
                     w_qkv_b, sink_b, w_o_b, rel_bias, w_ff_in, w_ff_out, w_router, w_exp_in, w_exp_out)
    return (y_prompt, y_sample)
```

```python
import functools
import math

import jax
import jax.numpy as jnp
from jax import lax
from jax.experimental import pallas as pl
from jax.experimental.pallas import tpu as pltpu

D_MODEL = 1024
SEQ = 4096
DEPTH = 4
HEAD_DIM = 64
N_Q_HEADS = 16
N_KV_HEADS = 4
GROUP = N_Q_HEADS // N_KV_HEADS
Q_WIDTH = N_Q_HEADS * HEAD_DIM
KV_WIDTH = N_KV_HEADS * HEAD_DIM
QKV_WIDTH = Q_WIDTH + 2 * KV_WIDTH
GRID_W = 64
ROPE_THETA = 10000.0
Q_BLOCK = 128
WINDOW = 128
N_BUCKETS = 32
MAX_DISTANCE = 128
D_FF = 2816
N_EXPERTS = 8
D_FF_EXPERT = 3584
EPS = 1e-6
NEG_INF = -1e30

LANES = 128
VMEM_LIMIT = 56 * 1024 * 1024

ROW_TILE = 512
KV_TILE = 512
MOE_ROW_TILE = 1024
MOE_FF_CHUNK = 512

F32 = jnp.float32
BF16 = jnp.bfloat16


def _params(*sem):
    return pltpu.CompilerParams(dimension_semantics=sem, vmem_limit_bytes=VMEM_LIMIT)


def _resident(shape, index_map):
    return pl.BlockSpec(shape, index_map, pipeline_mode=pl.Buffered(1))


def _rms(x, g):
    ms = jnp.mean(x * x, axis=-1, keepdims=True)
    return x * lax.rsqrt(ms + EPS) * g


def _qkv_kernel(x_ref, g_ref, w_ref, cos_ref, sin_ref, qg_ref, kg_ref, q_ref, kt_ref, v_ref, *, axial):
    h = _rms(x_ref[...], g_ref[...]).astype(BF16)
    acc = jnp.dot(h, w_ref[...], preferred_element_type=F32)
    scale = HEAD_DIM ** -0.5
    n_q_tiles = Q_WIDTH // LANES
    n_k_tiles = KV_WIDTH // LANES
    if axial:
        lane = lax.broadcasted_iota(jnp.int32, (acc.shape[0], LANES), 1)
        low = lane < HEAD_DIM
        even = (lane & 1) == 0
        cos = cos_ref[...]
        sin = sin_ref[...]

        def norm_rope(seg, gain):
            sq = seg * seg
            s_lo = jnp.sum(jnp.where(low, sq, 0.0), axis=-1, keepdims=True)
            s_hi = jnp.sum(jnp.where(low, 0.0, sq), axis=-1, keepdims=True)
            ms = jnp.where(low, s_lo, s_hi) * (1.0 / HEAD_DIM)
            y = seg * lax.rsqrt(ms + EPS) * gain
            swapped = jnp.where(even, pltpu.roll(y, LANES - 1, 1), pltpu.roll(y, 1, 1))
            return y * cos + swapped * sin

        qg = qg_ref[...]
        kg = kg_ref[...]
        for j in range(n_q_tiles):
            seg = norm_rope(acc[:, j * LANES:(j + 1) * LANES], qg)
            q_ref[:, j * LANES:(j + 1) * LANES] = (seg * scale).astype(BF16)
        k_parts = [norm_rope(acc[:, Q_WIDTH + j * LANES:Q_WIDTH + (j + 1) * LANES], kg)
                   for j in range(n_k_tiles)]
        k = jnp.concatenate(k_parts, axis=1)
    else:
        q_ref[...] = (acc[:, :Q_WIDTH] * scale).astype(BF16)
        k = acc[:, Q_WIDTH:Q_WIDTH + KV_WIDTH]
    kt_ref[0, 0] = k.T.astype(BF16)
    v_ref[...] = acc[:, Q_WIDTH + KV_WIDTH:].astype(BF16)


def _qkv_proj(x, g, w, cos, sin, qg, kg, *, axial):
    n_tok = x.shape[0]
    tm = ROW_TILE
    tiles_per_seq = SEQ // tm
    n_seq = n_tok // SEQ
    return pl.pallas_call(
        functools.partial(_qkv_kernel, axial=axial),
        grid=(n_tok // tm,),
        in_specs=[
            pl.BlockSpec((tm, D_MODEL), lambda i: (i, 0)),
            _resident((1, D_MODEL), lambda i: (0, 0)),
            _resident((D_MODEL, QKV_WIDTH), lambda i: (0, 0)),
            pl.BlockSpec((tm, LANES), lambda i: (i % tiles_per_seq, 0)),
            pl.BlockSpec((tm, LANES), lambda i: (i % tiles_per_seq, 0)),
            _resident((1, LANES), lambda i: (0, 0)),
            _resident((1, LANES), lambda i: (0, 0)),
        ],
        out_specs=[
            pl.BlockSpec((tm, Q_WIDTH), lambda i: (i, 0)),
            pl.BlockSpec((1, 1, KV_WIDTH, tm), lambda i: (i // tiles_per_seq, i % tiles_per_seq, 0, 0)),
            pl.BlockSpec((tm, KV_WIDTH), lambda i: (i, 0)),
        ],
        out_shape=[
            jax.ShapeDtypeStruct((n_tok, Q_WIDTH), BF16),
            jax.ShapeDtypeStruct((n_seq, tiles_per_seq, KV_WIDTH, tm), BF16),
            jax.ShapeDtypeStruct((n_tok, KV_WIDTH), BF16),
        ],
        compiler_params=_params("parallel"),
        name="qkv_axial" if axial else "qkv_window",
    )(x, g, w, cos, sin, qg, kg)


def _stack_group(q_ref, h):
    base = h * GROUP * HEAD_DIM
    return jnp.concatenate(
        [q_ref[:, base + g * HEAD_DIM:base + (g + 1) * HEAD_DIM] for g in range(GROUP)], axis=0)


def _store_group(o_ref, h, o, tq):
    base = h * GROUP * HEAD_DIM
    wide = jnp.concatenate([o[g * tq:(g + 1) * tq] for g in range(GROUP)], axis=1)
    o_ref[:, base:base + GROUP * HEAD_DIM] = wide.astype(o_ref.dtype)


def _global_attn_kernel(q_ref, kt_ref, v_ref, o_ref):
    tq = q_ref.shape[0]
    n_kv = kt_ref.shape[1]
    tk = kt_ref.shape[3]
    rows = GROUP * tq
    for h in range(N_KV_HEADS):
        q4 = _stack_group(q_ref, h)

        def body(j, carry):
            m, l, acc = carry
            kt = kt_ref[0, j, h * HEAD_DIM:(h + 1) * HEAD_DIM, :]
            s = jnp.dot(q4, kt, preferred_element_type=F32)
            m_new = jnp.maximum(m, jnp.max(s, axis=-1, keepdims=True))
            p = jnp.exp(s - m_new)
            alpha = jnp.exp(m - m_new)
            l_new = alpha * l + jnp.sum(p, axis=-1, keepdims=True)
            start = pl.multiple_of(j * tk, tk)
            vh = v_ref[pl.ds(start, tk), h * HEAD_DIM:(h + 1) * HEAD_DIM]
            acc_new = alpha * acc + jnp.dot(p.astype(BF16), vh, preferred_element_type=F32)
            return m_new, l_new, acc_new

        init = (jnp.full((rows, 1), -jnp.inf, F32), jnp.zeros((rows, 1), F32),
                jnp.zeros((rows, HEAD_DIM), F32))
        m, l, acc = lax.fori_loop(0, n_kv, body, init)
        _store_group(o_ref, h, acc / l, tq)


def _global_attention(q, kt, v):
    n_tok = q.shape[0]
    n_seq = n_tok // SEQ
    tq = Q_BLOCK
    nq = SEQ // tq
    return pl.pallas_call(
        _global_attn_kernel,
        grid=(n_seq, nq),
        in_specs=[
            pl.BlockSpec((tq, Q_WIDTH), lambda b, i: (b * nq + i, 0)),
            pl.BlockSpec((1,) + kt.shape[1:], lambda b, i: (b, 0, 0, 0)),
            pl.BlockSpec((SEQ, KV_WIDTH), lambda b, i: (b, 0)),
        ],
        out_specs=pl.BlockSpec((tq, Q_WIDTH), lambda b, i: (b * nq + i, 0)),
        out_shape=jax.ShapeDtypeStruct((n_tok, Q_WIDTH), BF16),
        compiler_params=_params("parallel", "arbitrary"),
        name="global_attention",
    )(q, kt, v)


def _window_attn_kernel(sink_ref, q_ref, ktl_ref, ktc_ref, ktr_ref, vl_ref, vc_ref, vr_ref, bias_ref, o_ref):
    i = pl.program_id(1)
    nq = pl.num_programs(1)
    tq = q_ref.shape[0]
    rows = GROUP * tq
    first_col = jnp.where(i > 0, 0, tq)
    end_col = jnp.where(i < nq - 1, 3 * tq, 2 * tq)
    kt_all = jnp.concatenate([ktl_ref[0, 0], ktc_ref[0, 0], ktr_ref[0, 0]], axis=1)
    v_all = jnp.concatenate([vl_ref[...], vc_ref[...], vr_ref[...]], axis=0)
    col = lax.broadcasted_iota(jnp.int32, (rows, 3 * tq), 1)
    in_seq = (col >= first_col) & (col < end_col)
    for h in range(N_KV_HEADS):
        q4 = _stack_group(q_ref, h)
        s = jnp.dot(q4, kt_all[h * HEAD_DIM:(h + 1) * HEAD_DIM, :], preferred_element_type=F32)
        s = s + bias_ref[h * GROUP:(h + 1) * GROUP].reshape(rows, 3 * tq)
        s = jnp.where(in_seq, s, NEG_INF)
        sink = jnp.concatenate(
            [jnp.full((tq, 1), sink_ref[h * GROUP + g], F32) for g in range(GROUP)], axis=0)
        m = jnp.maximum(jnp.max(s, axis=-1, keepdims=True), sink)
        e = jnp.exp(s - m)
        p = e / (jnp.sum(e, axis=-1, keepdims=True) + jnp.exp(sink - m))
        o = jnp.dot(p.astype(BF16), v_all[:, h * HEAD_DIM:(h + 1) * HEAD_DIM], preferred_element_type=F32)
        _store_group(o_ref, h, o, tq)


def _window_attention(q, kt, v, sinks, bias):
    n_tok = q.shape[0]
    n_seq = n_tok // SEQ
    tq = Q_BLOCK
    nq = SEQ // tq
    per = ROW_TILE // tq
    kt_spec = lambda off: pl.BlockSpec(
        (1, 1, KV_WIDTH, tq),
        lambda b, i, s: (b, jnp.clip(i + off, 0, nq - 1) // per, 0, jnp.clip(i + off, 0, nq - 1) % per))
    v_spec = lambda off: pl.BlockSpec(
        (tq, KV_WIDTH), lambda b, i, s: (b * nq + jnp.clip(i + off, 0, nq - 1), 0))
    grid_spec = pltpu.PrefetchScalarGridSpec(
        num_scalar_prefetch=1,
        grid=(n_seq, nq),
        in_specs=[
            pl.BlockSpec((tq, Q_WIDTH), lambda b, i, s: (b * nq + i, 0)),
            kt_spec(-1), kt_spec(0), kt_spec(1),
            v_spec(-1), v_spec(0), v_spec(1),
            _resident((N_Q_HEADS, tq, 3 * tq), lambda b, i, s: (0, 0, 0)),
        ],
        out_specs=pl.BlockSpec((tq, Q_WIDTH), lambda b, i, s: (b * nq + i, 0)),
    )
    return pl.pallas_call(
        _window_attn_kernel,
        grid_spec=grid_spec,
        out_shape=jax.ShapeDtypeStruct((n_tok, Q_WIDTH), BF16),
        compiler_params=_params("parallel", "arbitrary"),
        name="window_attention",
    )(sinks, q, kt, kt, kt, v, v, v, bias)


def _out_proj_kernel(o_ref, w_ref, x_ref, y_ref):
    y_ref[...] = x_ref[...] + jnp.dot(o_ref[...], w_ref[...], preferred_element_type=F32)


def _out_proj(o, w, x):
    n_tok = x.shape[0]
    tm = ROW_TILE
    return pl.pallas_call(
        _out_proj_kernel,
        grid=(n_tok // tm,),
        in_specs=[
            pl.BlockSpec((tm, Q_WIDTH), lambda i: (i, 0)),
            _resident((Q_WIDTH, D_MODEL), lambda i: (0, 0)),
            pl.BlockSpec((tm, D_MODEL), lambda i: (i, 0)),
        ],
        out_specs=pl.BlockSpec((tm, D_MODEL), lambda i: (i, 0)),
        out_shape=jax.ShapeDtypeStruct((n_tok, D_MODEL), F32),
        compiler_params=_params("parallel"),
        name="out_proj",
    )(o, w, x)


def _ff_chunks(d_ff, width):
    return [(c, min(c + width, d_ff)) for c in range(0, d_ff, width)]


def _dense_ffn_kernel(x_ref, g_ref, w_in_ref, w_out_ref, y_ref):
    x = x_ref[...]
    h = _rms(x, g_ref[...]).astype(BF16)
    acc = x
    for c0, c1 in _ff_chunks(D_FF, 1024):
        gate = jnp.dot(h, w_in_ref[:, c0:c1], preferred_element_type=F32)
        up = jnp.dot(h, w_in_ref[:, D_FF + c0:D_FF + c1], preferred_element_type=F32)
        a = (gate * jax.nn.sigmoid(gate) * up).astype(BF16)
        acc = acc + jnp.dot(a, w_out_ref[c0:c1, :], preferred_element_type=F32)
    y_ref[...] = acc


def _dense_ffn(x, g, w_in, w_out):
    n_tok = x.shape[0]
    tm = ROW_TILE
    return pl.pallas_call(
        _dense_ffn_kernel,
        grid=(n_tok // tm,),
        in_specs=[
            pl.BlockSpec((tm, D_MODEL), lambda i: (i, 0)),
            _resident((1, D_MODEL), lambda i: (0, 0)),
            _resident((D_MODEL, 2 * D_FF), lambda i: (0, 0)),
            _resident((D_FF, D_MODEL), lambda i: (0, 0)),
        ],
        out_specs=pl.BlockSpec((tm, D_MODEL), lambda i: (i, 0)),
        out_shape=jax.ShapeDtypeStruct((n_tok, D_MODEL), F32),
        compiler_params=_params("parallel"),
        name="dense_ffn",
    )(x, g, w_in, w_out)


def _router_kernel(x_ref, g_ref, wr_ref, h_ref, comb_ref):
    hf = _rms(x_ref[...], g_ref[...])
    h_ref[...] = hf.astype(BF16)
    logits = jnp.dot(hf, wr_ref[...], preferred_element_type=F32, precision=lax.Precision.HIGHEST)
    lane = lax.broadcasted_iota(jnp.int32, logits.shape, 1)
    logits = jnp.where(lane < N_EXPERTS, logits, -jnp.inf)
    m1 = jnp.max(logits, axis=-1, keepdims=True)
    i1 = jnp.min(jnp.where(logits == m1, lane, LANES), axis=-1, keepdims=True)
    rest = jnp.where(lane == i1, -jnp.inf, logits)
    m2 = jnp.max(rest, axis=-1, keepdims=True)
    i2 = jnp.min(jnp.where(rest == m2, lane, LANES), axis=-1, keepdims=True)
    e2 = jnp.exp(m2 - m1)
    g1 = 1.0 / (1.0 + e2)
    g2 = e2 / (1.0 + e2)
    comb_ref[...] = jnp.where(lane == i1, g1, 0.0) + jnp.where(lane == i2, g2, 0.0)


def _router(x, g, wr_padded):
    n_tok = x.shape[0]
    tm = ROW_TILE
    return pl.pallas_call(
        _router_kernel,
        grid=(n_tok // tm,),
        in_specs=[
            pl.BlockSpec((tm, D_MODEL), lambda i: (i, 0)),
            _resident((1, D_MODEL), lambda i: (0, 0)),
            _resident((D_MODEL, LANES), lambda i: (0, 0)),
        ],
        out_specs=[
            pl.BlockSpec((tm, D_MODEL), lambda i: (i, 0)),
            pl.BlockSpec((tm, LANES), lambda i: (i, 0)),
        ],
        out_shape=[
            jax.ShapeDtypeStruct((n_tok, D_MODEL), BF16),
            jax.ShapeDtypeStruct((n_tok, LANES), F32),
        ],
        compiler_params=_params("parallel"),
        name="router",
    )(x, g, wr_padded)


def _moe_dense_kernel(h_ref, comb_ref, wg_ref, wu_ref, wo_ref, x_ref, gf_ref, y_ref, acc_ref, *, final_norm):
    e = pl.program_id(1)
    c = pl.program_id(2)
    first = (e == 0) & (c == 0)
    last = (e == pl.num_programs(1) - 1) & (c == pl.num_programs(2) - 1)

    @pl.when(first)
    def _():
        acc_ref[...] = jnp.zeros_like(acc_ref)

    h = h_ref[...]
    gate = jnp.dot(h, wg_ref[0], preferred_element_type=F32)
    up = jnp.dot(h, wu_ref[0], preferred_element_type=F32)
    a = (gate * jax.nn.sigmoid(gate) * up).astype(BF16)
    y = jnp.dot(a, wo_ref[0], preferred_element_type=F32)
    comb = comb_ref[...]
    lane = lax.broadcasted_iota(jnp.int32, comb.shape, 1)
    weight = jnp.sum(jnp.where(lane == e, comb, 0.0), axis=-1, keepdims=True)
    acc_ref[...] += weight * y

    @pl.when(last)
    def _():
        out = x_ref[...] + acc_ref[...]
        if final_norm:
            out = _rms(out, gf_ref[...])
        y_ref[...] = out


def _moe_dense(h, comb, w_in, w_out, x, g_final, *, final_norm):
    n_tok = x.shape[0]
    tm = MOE_ROW_TILE
    ck = MOE_FF_CHUNK
    n_ck = D_FF_EXPERT // ck
    return pl.pallas_call(
        functools.partial(_moe_dense_kernel, final_norm=final_norm),
        grid=(n_tok // tm, N_EXPERTS, n_ck),
        in_specs=[
            pl.BlockSpec((tm, D_MODEL), lambda i, e, c: (i, 0)),
            pl.BlockSpec((tm, LANES), lambda i, e, c: (i, 0)),
            pl.BlockSpec((1, D_MODEL, ck), lambda i, e, c: (e, 0, c)),
            pl.BlockSpec((1, D_MODEL, ck), lambda i, e, c: (e, 0, n_ck + c)),
            pl.BlockSpec((1, ck, D_MODEL), lambda i, e, c: (e, c, 0)),
            pl.BlockSpec((tm, D_MODEL), lambda i, e, c: (i, 0)),
            _resident((1, D_MODEL), lambda i, e, c: (0, 0)),
        ],
        out_specs=pl.BlockSpec((tm, D_MODEL), lambda i, e, c: (i, 0)),
        out_shape=jax.ShapeDtypeStruct((n_tok, D_MODEL), F32),
        scratch_shapes=[pltpu.VMEM((tm, D_MODEL), F32)],
        compiler_params=_params("parallel", "arbitrary", "arbitrary"),
        name="moe_dense",
    )(h, comb, w_in, w_in, w_out, x, g_final)


def _rope_tables():
    t = jnp.arange(SEQ)
    row = (t // GRID_W).astype(F32)
    col = (t % GRID_W).astype(F32)
    half = HEAD_DIM // 2
    inv_freq = ROPE_THETA ** (-jnp.arange(0, half, 2, dtype=F32) / half)
    ang = jnp.concatenate([row[:, None] * inv_freq[None, :], col[:, None] * inv_freq[None, :]], axis=-1)
    cos = jnp.repeat(jnp.cos(ang), 2, axis=-1)
    sign = jnp.where(jnp.arange(HEAD_DIM) % 2 == 0, -1.0, 1.0).astype(F32)
    sin = jnp.repeat(jnp.sin(ang), 2, axis=-1) * sign
    reps = LANES // HEAD_DIM
    return jnp.tile(cos, (1, reps)), jnp.tile(sin, (1, reps))


def _t5_bucket(rel):
    half = N_BUCKETS // 2
    ret = jnp.where(rel > 0, half, 0)
    n = jnp.abs(rel)
    max_exact = half // 2
    nf = jnp.maximum(n, 1).astype(F32)
    large = max_exact + (jnp.log(nf / max_exact) / math.log(MAX_DISTANCE / max_exact)
                         * (half - max_exact)).astype(jnp.int32)
    large = jnp.minimum(large, half - 1)
    return ret + jnp.where(n < max_exact, n, large)


def _window_bias(rel_bias):
    r = jnp.arange(Q_BLOCK)[:, None]
    c = jnp.arange(Q_BLOCK + 2 * WINDOW)[None, :]
    rel = c - WINDOW - r
    bias = rel_bias[_t5_bucket(rel)].astype(F32).transpose(2, 0, 1)
    return jnp.where((jnp.abs(rel) <= WINDOW)[None], bias, NEG_INF)


def _tile_gain(gain):
    return jnp.tile(gain.astype(F32), LANES // HEAD_DIM)[None, :]


def kernel(x_prompt, x_sample, norm_mix, norm_ffn, norm_final, w_qkv_a, q_gain_a, k_gain_a, w_o_a,
           w_qkv_b, sink_b, w_o_b, rel_bias, w_ff_in, w_ff_out, w_router, w_exp_in, w_exp_out):
    n_prompt = x_prompt.shape[0] * x_prompt.shape[1]
    x = jnp.concatenate([x_prompt.reshape(-1, D_MODEL), x_sample.reshape(-1, D_MODEL)], axis=0)
    cos, sin = _rope_tables()
    bias = _window_bias(rel_bias)
    ones_gain = jnp.ones((1, LANES), F32)
    for i in range(DEPTH):
        j = i // 2
        g_mix = norm_mix[i][None, :]
        g_ffn = norm_ffn[i][None, :]
        if i % 2 == 0:
            q, kt, v = _qkv_proj(x, g_mix, w_qkv_a[j].astype(BF16), cos, sin,
                                 _tile_gain(q_gain_a[j]), _tile_gain(k_gain_a[j]), axial=True)
            o = _global_attention(q, kt, v)
            x = _out_proj(o, w_o_a[j].astype(BF16), x)
            x = _dense_ffn(x, g_ffn, w_ff_in[j].astype(BF16), w_ff_out[j].astype(BF16))
        else:
            q, kt, v = _qkv_proj(x, g_mix, w_qkv_b[j].astype(BF16), cos, sin, ones_gain, ones_gain, axial=False)
            o = _window_attention(q, kt, v, sink_b[j].astype(F32), bias)
            x = _out_proj(o, w_o_b[j].astype(BF16), x)
            wr = jnp.pad(w_router[j], ((0, 0), (0, LANES - N_EXPERTS)))
            h, comb = _router(x, g_ffn, wr)
            x = _moe_dense(h, comb, w_exp_in[j].astype(BF16), w_exp_out[j].astype(BF16), x,
                           norm_final[None, :], final_norm=(i == DEPTH - 1))
    y = x.reshape(-1, SEQ, D_MODEL)
    n_seq_prompt = n_prompt // SEQ
    return (y[:n_seq_prompt], y[n_seq_prompt:])
```

```python
import functools
import math

import jax
import jax.numpy as jnp
from jax import lax
from jax.experimental import pallas as pl
from jax.experimental.pallas import tpu as pltpu

D_MODEL = 1024
SEQ = 4096
DEPTH = 4
HEAD_DIM = 64
N_Q_HEADS = 16
N_KV_HEADS = 4
GROUP = N_Q_HEADS // N_KV_HEADS
Q_WIDTH = N_Q_HEADS * HEAD_DIM
KV_WIDTH = N_KV_HEADS * HEAD_DIM
QKV_WIDTH = Q_WIDTH + 2 * KV_WIDTH
GRID_W = 64
ROPE_THETA = 10000.0
Q_BLOCK = 128
WINDOW = 128
N_BUCKETS = 32
MAX_DISTANCE = 128
D_FF = 2816
N_EXPERTS = 8
D_FF_EXPERT = 3584
EPS = 1e-6
NEG_INF = -1e30

LANES = 128
VMEM_LIMIT = 56 * 1024 * 1024

ROW_TILE = 512
KV_TILE = 512
MOE_ROW_TILE = 1024
MOE_FF_CHUNK = 512

F32 = jnp.float32
BF16 = jnp.bfloat16


def _params(*sem):
    return pltpu.CompilerParams(dimension_semantics=sem, vmem_limit_bytes=VMEM_LIMIT)


def _resident(shape, index_map):
    return pl.BlockSpec(shape, index_map, pipeline_mode=pl.Buffered(1))


def _rms(x, g):
    ms = jnp.mean(x * x, axis=-1, keepdims=True)
    return x * lax.rsqrt(ms + EPS) * g


def _qkv_kernel(x_ref, g_ref, w_ref, cos_ref, sin_ref, qg_ref, kg_ref, q_ref, kt_ref, v_ref, *, axial):
    h = _rms(x_ref[...], g_ref[...]).astype(BF16)
    acc = jnp.dot(h, w_ref[...], preferred_element_type=F32)
    scale = HEAD_DIM ** -0.5
    n_q_tiles = Q_WIDTH // LANES
    n_k_tiles = KV_WIDTH // LANES
    if axial:
        lane = lax.broadcasted_iota(jnp.int32, (acc.shape[0], LANES), 1)
        low = lane < HEAD_DIM
        even = (lane & 1) == 0
        cos = cos_ref[...]
        sin = sin_ref[...]

        def norm_rope(seg, gain):
            sq = seg * seg
            s_lo = jnp.sum(jnp.where(low, sq, 0.0), axis=-1, keepdims=True)
            s_hi = jnp.sum(jnp.where(low, 0.0, sq), axis=-1, keepdims=True)
            ms = jnp.where(low, s_lo, s_hi) * (1.0 / HEAD_DIM)
            y = seg * lax.rsqrt(ms + EPS) * gain
            swapped = jnp.where(even, pltpu.roll(y, LANES - 1, 1), pltpu.roll(y, 1, 1))
            return y * cos + swapped * sin

        qg = qg_ref[...]
        kg = kg_ref[...]
        for j in range(n_q_tiles):
            seg = norm_rope(acc[:, j * LANES:(j + 1) * LANES], qg)
            q_ref[:, j * LANES:(j + 1) * LANES] = (seg * scale).astype(BF16)
        k_parts = [norm_rope(acc[:, Q_WIDTH + j * LANES:Q_WIDTH + (j + 1) * LANES], kg)
                   for j in range(n_k_tiles)]
        k = jnp.concatenate(k_parts, axis=1)
    else:
        q_ref[...] = (acc[:, :Q_WIDTH] * scale).astype(BF16)
        k = acc[:, Q_WIDTH:Q_WIDTH + KV_WIDTH]
    kt_ref[0, 0] = k.T.astype(BF16)
    v_ref[...] = acc[:, Q_WIDTH + KV_WIDTH:].astype(BF16)


def _qkv_proj(x, g, w, cos, sin, qg, kg, *, axial):
    n_tok = x.shape[0]
    tm = ROW_TILE
    tiles_per_seq = SEQ // tm
    n_seq = n_tok // SEQ
    return pl.pallas_call(
        functools.partial(_qkv_kernel, axial=axial),
        grid=(n_tok // tm,),
        in_specs=[
            pl.BlockSpec((tm, D_MODEL), lambda i: (i, 0)),
            _resident((1, D_MODEL), lambda i: (0, 0)),
            _resident((D_MODEL, QKV_WIDTH), lambda i: (0, 0)),
            pl.BlockSpec((tm, LANES), lambda i: (i % tiles_per_seq, 0)),
            pl.BlockSpec((tm, LANES), lambda i: (i % tiles_per_seq, 0)),
            _resident((1, LANES), lambda i: (0, 0)),
            _resident((1, LANES), lambda i: (0, 0)),
        ],
        out_specs=[
            pl.BlockSpec((tm, Q_WIDTH), lambda i: (i, 0)),
            pl.BlockSpec((1, 1, KV_WIDTH, tm), lambda i: (i // tiles_per_seq, i % tiles_per_seq, 0, 0)),
            pl.BlockSpec((tm, KV_WIDTH), lambda i: (i, 0)),
        ],
        out_shape=[
            jax.ShapeDtypeStruct((n_tok, Q_WIDTH), BF16),
            jax.ShapeDtypeStruct((n_seq, tiles_per_seq, KV_WIDTH, tm), BF16),
            jax.ShapeDtypeStruct((n_tok, KV_WIDTH), BF16),
        ],
        compiler_params=_params("parallel"),
        name="qkv_axial" if axial else "qkv_window",
    )(x, g, w, cos, sin, qg, kg)


def _stack_group(q_ref, h):
    base = h * GROUP * HEAD_DIM
    return jnp.concatenate(
        [q_ref[:, base + g * HEAD_DIM:base + (g + 1) * HEAD_DIM] for g in range(GROUP)], axis=0)


def _store_group(o_ref, h, o, tq):
    base = h * GROUP * HEAD_DIM
    wide = jnp.concatenate([o[g * tq:(g + 1) * tq] for g in range(GROUP)], axis=1)
    o_ref[:, base:base + GROUP * HEAD_DIM] = wide.astype(o_ref.dtype)


def _global_attn_kernel(q_ref, kt_ref, v_ref, o_ref):
    tq = q_ref.shape[0]
    n_kv = kt_ref.shape[1]
    tk = kt_ref.shape[3]
    rows = GROUP * tq
    for h in range(N_KV_HEADS):
        q4 = _stack_group(q_ref, h)

        def body(j, carry):
            m, l, acc = carry
            kt = kt_ref[0, j, h * HEAD_DIM:(h + 1) * HEAD_DIM, :]
            s = jnp.dot(q4, kt, preferred_element_type=F32)
            m_new = jnp.maximum(m, jnp.max(s, axis=-1, keepdims=True))
            p = jnp.exp(s - m_new)
            alpha = jnp.exp(m - m_new)
            l_new = alpha * l + jnp.sum(p, axis=-1, keepdims=True)
            start = pl.multiple_of(j * tk, tk)
            vh = v_ref[pl.ds(start, tk), h * HEAD_DIM:(h + 1) * HEAD_DIM]
            acc_new = alpha * acc + jnp.dot(p.astype(BF16), vh, preferred_element_type=F32)
            return m_new, l_new, acc_new

        init = (jnp.full((rows, 1), -jnp.inf, F32), jnp.zeros((rows, 1), F32),
                jnp.zeros((rows, HEAD_DIM), F32))
        m, l, acc = lax.fori_loop(0, n_kv, body, init)
        _store_group(o_ref, h, acc / l, tq)


def _global_attention(q, kt, v):
    n_tok = q.shape[0]
    n_seq = n_tok // SEQ
    tq = Q_BLOCK
    nq = SEQ // tq
    return pl.pallas_call(
        _global_attn_kernel,
        grid=(n_seq, nq),
        in_specs=[
            pl.BlockSpec((tq, Q_WIDTH), lambda b, i: (b * nq + i, 0)),
            pl.BlockSpec((1,) + kt.shape[1:], lambda b, i: (b, 0, 0, 0)),
            pl.BlockSpec((SEQ, KV_WIDTH), lambda b, i: (b, 0)),
        ],
        out_specs=pl.BlockSpec((tq, Q_WIDTH), lambda b, i: (b * nq + i, 0)),
        out_shape=jax.ShapeDtypeStruct((n_tok, Q_WIDTH), BF16),
        compiler_params=_params("parallel", "arbitrary"),
        name="global_attention",
    )(q, kt, v)


def _window_attn_kernel(sink_ref, q_ref, ktl_ref, ktc_ref, ktr_ref, vl_ref, vc_ref, vr_ref, bias_ref, o_ref):
    i = pl.program_id(1)
    nq = pl.num_programs(1)
    tq = q_ref.shape[0]
    rows = GROUP * tq
    first_col = jnp.where(i > 0, 0, tq)
    end_col = jnp.where(i < nq - 1, 3 * tq, 2 * tq)
    kt_all = jnp.concatenate([ktl_ref[0, 0], ktc_ref[0, 0], ktr_ref[0, 0]], axis=1)
    v_all = jnp.concatenate([vl_ref[...], vc_ref[...], vr_ref[...]], axis=0)
    col = lax.broadcasted_iota(jnp.int32, (rows, 3 * tq), 1)
    in_seq = (col >= first_col) & (col < end_col)
    for h in range(N_KV_HEADS):
        q4 = _stack_group(q_ref, h)
        s = jnp.dot(q4, kt_all[h * HEAD_DIM:(h + 1) * HEAD_DIM, :], preferred_element_type=F32)
        s = s + bias_ref[h * GROUP:(h + 1) * GROUP].reshape(rows, 3 * tq)
        s = jnp.where(in_seq, s, NEG_INF)
        sink = jnp.concatenate(
            [jnp.full((tq, 1), sink_ref[h * GROUP + g], F32) for g in range(GROUP)], axis=0)
        m = jnp.maximum(jnp.max(s, axis=-1, keepdims=True), sink)
        e = jnp.exp(s - m)
        p = e / (jnp.sum(e, axis=-1, keepdims=True) + jnp.exp(sink - m))
        o = jnp.dot(p.astype(BF16), v_all[:, h * HEAD_DIM:(h + 1) * HEAD_DIM], preferred_element_type=F32)
        _store_group(o_ref, h, o, tq)


def _window_attention(q, kt, v, sinks, bias):
    n_tok = q.shape[0]
    n_seq = n_tok // SEQ
    tq = Q_BLOCK
    nq = SEQ // tq
    per = ROW_TILE // tq
    kt_spec = lambda off: pl.BlockSpec(
        (1, 1, KV_WIDTH, tq),
        lambda b, i, s: (b, jnp.clip(i + off, 0, nq - 1) // per, 0, jnp.clip(i + off, 0, nq - 1) % per))
    v_spec = lambda off: pl.BlockSpec(
        (tq, KV_WIDTH), lambda b, i, s: (b * nq + jnp.clip(i + off, 0, nq - 1), 0))
    grid_spec = pltpu.PrefetchScalarGridSpec(
        num_scalar_prefetch=1,
        grid=(n_seq, nq),
        in_specs=[
            pl.BlockSpec((tq, Q_WIDTH), lambda b, i, s: (b * nq + i, 0)),
            kt_spec(-1), kt_spec(0), kt_spec(1),
            v_spec(-1), v_spec(0), v_spec(1),
            _resident((N_Q_HEADS, tq, 3 * tq), lambda b, i, s: (0, 0, 0)),
        ],
        out_specs=pl.BlockSpec((tq, Q_WIDTH), lambda b, i, s: (b * nq + i, 0)),
    )
    return pl.pallas_call(
        _window_attn_kernel,
        grid_spec=grid_spec,
        out_shape=jax.ShapeDtypeStruct((n_tok, Q_WIDTH), BF16),
        compiler_params=_params("parallel", "arbitrary"),
        name="window_attention",
    )(sinks, q, kt, kt, kt, v, v, v, bias)


def _out_proj_kernel(o_ref, w_ref, x_ref, y_ref):
    y_ref[...] = x_ref[...] + jnp.dot(o_ref[...], w_ref[...], preferred_element_type=F32)


def _out_proj(o, w, x):
    n_tok = x.shape[0]
    tm = ROW_TILE
    return pl.pallas_call(
        _out_proj_kernel,
        grid=(n_tok // tm,),
        in_specs=[
            pl.BlockSpec((tm, Q_WIDTH), lambda i: (i, 0)),
            _resident((Q_WIDTH, D_MODEL), lambda i: (0, 0)),
            pl.BlockSpec((tm, D_MODEL), lambda i: (i, 0)),
        ],
        out_specs=pl.BlockSpec((tm, D_MODEL), lambda i: (i, 0)),
        out_shape=jax.ShapeDtypeStruct((n_tok, D_MODEL), F32),
        compiler_params=_params("parallel"),
        name="out_proj",
    )(o, w, x)


def _ff_chunks(d_ff, width):
    return [(c, min(c + width, d_ff)) for c in range(0, d_ff, width)]


def _dense_ffn_kernel(x_ref, g_ref, w_in_ref, w_out_ref, y_ref):
    x = x_ref[...]
    h = _rms(x, g_ref[...]).astype(BF16)
    acc = x
    for c0, c1 in _ff_chunks(D_FF, 1024):
        gate = jnp.dot(h, w_in_ref[:, c0:c1], preferred_element_type=F32)
        up = jnp.dot(h, w_in_ref[:, D_FF + c0:D_FF + c1], preferred_element_type=F32)
        a = (gate * jax.nn.sigmoid(gate) * up).astype(BF16)
        acc = acc + jnp.dot(a, w_out_ref[c0:c1, :], preferred_element_type=F32)
    y_ref[...] = acc


def _dense_ffn(x, g, w_in, w_out):
    n_tok = x.shape[0]
    tm = ROW_TILE
    return pl.pallas_call(
        _dense_ffn_kernel,
        grid=(n_tok // tm,),
        in_specs=[
            pl.BlockSpec((tm, D_MODEL), lambda i: (i, 0)),
            _resident((1, D_MODEL), lambda i: (0, 0)),
            _resident((D_MODEL, 2 * D_FF), lambda i: (0, 0)),
            _resident((D_FF, D_MODEL), lambda i: (0, 0)),
        ],
        out_specs=pl.BlockSpec((tm, D_MODEL), lambda i: (i, 0)),
        out_shape=jax.ShapeDtypeStruct((n_tok, D_MODEL), F32),
        compiler_params=_params("parallel"),
        name="dense_ffn",
    )(x, g, w_in, w_out)


def _router_kernel(x_ref, g_ref, wr_ref, meta_ref, gate_ref, count_ref, base_ref):
    @pl.when(pl.program_id(0) == 0)
    def _():
        base_ref[...] = jnp.zeros_like(base_ref)

    hf = _rms(x_ref[...], g_ref[...])
    logits = jnp.dot(hf, wr_ref[...], preferred_element_type=F32, precision=lax.Precision.HIGHEST)
    lane = lax.broadcasted_iota(jnp.int32, logits.shape, 1)
    logits = jnp.where(lane < N_EXPERTS, logits, -jnp.inf)
    m1 = jnp.max(logits, axis=-1, keepdims=True)
    i1 = jnp.min(jnp.where(logits == m1, lane, LANES), axis=-1, keepdims=True)
    rest = jnp.where(lane == i1, -jnp.inf, logits)
    m2 = jnp.max(rest, axis=-1, keepdims=True)
    i2 = jnp.min(jnp.where(rest == m2, lane, LANES), axis=-1, keepdims=True)
    e2 = jnp.exp(m2 - m1)
    g1 = 1.0 / (1.0 + e2)
    g2 = e2 / (1.0 + e2)

    chosen = jnp.where((lane == i1) | (lane == i2), 1.0, 0.0)
    tm = chosen.shape[0]
    r_idx = lax.broadcasted_iota(jnp.int32, (tm, tm), 0)
    c_idx = lax.broadcasted_iota(jnp.int32, (tm, tm), 1)
    earlier = jnp.where(c_idx < r_idx, 1.0, 0.0).astype(BF16)
    before = jnp.dot(earlier, chosen.astype(BF16), preferred_element_type=F32) + base_ref[...]
    r1 = jnp.sum(jnp.where(lane == i1, before, 0.0), axis=-1, keepdims=True).astype(jnp.int32)
    r2 = jnp.sum(jnp.where(lane == i2, before, 0.0), axis=-1, keepdims=True).astype(jnp.int32)
    meta_ref[...] = jnp.where(lane == 0, i1, jnp.where(lane == 1, i2, jnp.where(lane == 2, r1, jnp.where(lane == 3, r2, 0))))
    gate_ref[...] = jnp.where(lane == 0, g1, jnp.where(lane == 1, g2, 0.0))
    base_ref[...] += jnp.sum(chosen, axis=0, keepdims=True)
    count_ref[...] = base_ref[...]


def _router(x, g, wr_padded):
    n_tok = x.shape[0]
    tm = ROW_TILE
    return pl.pallas_call(
        _router_kernel,
        grid=(n_tok // tm,),
        in_specs=[
            pl.BlockSpec((tm, D_MODEL), lambda i: (i, 0)),
            _resident((1, D_MODEL), lambda i: (0, 0)),
            _resident((D_MODEL, LANES), lambda i: (0, 0)),
        ],
        out_specs=[
            pl.BlockSpec((tm, LANES), lambda i: (i, 0)),
            pl.BlockSpec((tm, LANES), lambda i: (i, 0)),
            pl.BlockSpec((1, LANES), lambda i: (0, 0)),
        ],
        out_shape=[
            jax.ShapeDtypeStruct((n_tok, LANES), jnp.int32),
            jax.ShapeDtypeStruct((n_tok, LANES), F32),
            jax.ShapeDtypeStruct((1, LANES), F32),
        ],
        scratch_shapes=[pltpu.VMEM((1, LANES), F32)],
        compiler_params=_params("arbitrary"),
        name="router",
    )(x, g, wr_padded)


def _moe_plan(meta, counts):
    tm = MOE_ROW_TILE
    n_tok = meta.shape[0]
    n_tiles = (2 * n_tok) // tm + N_EXPERTS
    cnt = counts[0, :N_EXPERTS].astype(jnp.int32)
    tiles = (cnt + tm - 1) // tm
    tile_end = jnp.cumsum(tiles)
    offsets = (tile_end - tiles) * tm
    tile_ids = jnp.arange(n_tiles, dtype=jnp.int32)
    tile_expert = jnp.minimum(jnp.sum(tile_ids[:, None] >= tile_end[None, :], axis=1), N_EXPERTS - 1)
    used = tile_end[-1:].astype(jnp.int32)
    pos1 = jnp.take(offsets, meta[:, 0]) + meta[:, 2]
    pos2 = jnp.take(offsets, meta[:, 1]) + meta[:, 3]
    pos = jnp.stack([pos1.reshape(-1, ROW_TILE), pos2.reshape(-1, ROW_TILE)], axis=1).reshape(-1)
    return pos.astype(jnp.int32), tile_expert.astype(jnp.int32), used, n_tiles * tm


def _load_positions(pos_hbm, pos_smem, pos_sem):
    n = pos_smem.shape[0]
    start = pl.multiple_of(pl.program_id(0) * n, n)
    return pltpu.make_async_copy(pos_hbm.at[pl.ds(start, n)], pos_smem, pos_sem)


def _dispatch_kernel(pos_hbm, x_ref, g_ref, xs_in, xs_out, h_ref, pos_smem, sem, pos_sem):
    del xs_in
    tm = h_ref.shape[0]
    pos_copy = _load_positions(pos_hbm, pos_smem, pos_sem)
    pos_copy.start()
    h_ref[...] = _rms(x_ref[...], g_ref[...])
    pos_copy.wait()

    def row_copy(r, slot):
        return pltpu.make_async_copy(h_ref.at[pl.ds(r, 1)], xs_out.at[pl.ds(slot, 1)], sem)

    def issue(r, carry):
        row_copy(r, pos_smem[r]).start()
        row_copy(r, pos_smem[tm + r]).start()
        return carry

    lax.fori_loop(0, tm, issue, 0, unroll=8)
    for _ in range(2):
        pltpu.make_async_copy(h_ref, xs_out.at[pl.ds(0, tm)], sem).wait()


def _dispatch(pos, x, g, n_slots):
    n_tok = x.shape[0]
    tm = ROW_TILE
    xs0 = jnp.zeros((n_slots, D_MODEL), F32)
    return pl.pallas_call(
        _dispatch_kernel,
        grid=(n_tok // tm,),
        in_specs=[
            pl.BlockSpec(memory_space=pl.ANY),
            pl.BlockSpec((tm, D_MODEL), lambda i: (i, 0)),
            _resident((1, D_MODEL), lambda i: (0, 0)),
            pl.BlockSpec(memory_space=pl.ANY),
        ],
        out_specs=pl.BlockSpec(memory_space=pl.ANY),
        out_shape=jax.ShapeDtypeStruct((n_slots, D_MODEL), F32),
        scratch_shapes=[
            pltpu.VMEM((tm, D_MODEL), F32),
            pltpu.SMEM((2 * tm,), jnp.int32),
            pltpu.SemaphoreType.DMA(()),
            pltpu.SemaphoreType.DMA(()),
        ],
        input_output_aliases={3: 0},
        compiler_params=_params("arbitrary"),
        name="moe_dispatch",
    )(pos, x, g, xs0)


def _expert_kernel(te_ref, used_ref, xs_ref, wg_ref, wu_ref, wo_ref, y_ref, hb_ref):
    del te_ref
    c = pl.program_id(1)
    in_use = pl.program_id(0) < used_ref[0]

    @pl.when(jnp.logical_not(in_use) & (c == 0))
    def _():
        y_ref[...] = jnp.zeros_like(y_ref)

    @pl.when(in_use)
    def _():
        @pl.when(c == 0)
        def _():
            hb_ref[...] = xs_ref[...].astype(BF16)
            y_ref[...] = jnp.zeros_like(y_ref)

        h = hb_ref[...]
        gate = jnp.dot(h, wg_ref[0], preferred_element_type=F32)
        up = jnp.dot(h, wu_ref[0], preferred_element_type=F32)
        a = (gate * jax.nn.sigmoid(gate) * up).astype(BF16)
        y_ref[...] += jnp.dot(a, wo_ref[0], preferred_element_type=F32)


def _experts(tile_expert, used, xs, w_in, w_out):
    tm = MOE_ROW_TILE
    ck = MOE_FF_CHUNK
    n_ck = D_FF_EXPERT // ck
    n_slots = xs.shape[0]

    def chunk(i, c, te, used):
        return jnp.where(i < used[0], c, n_ck - 1)

    grid_spec = pltpu.PrefetchScalarGridSpec(
        num_scalar_prefetch=2,
        grid=(n_slots // tm, n_ck),
        in_specs=[
            pl.BlockSpec((tm, D_MODEL), lambda i, c, te, used: (i, 0)),
            pl.BlockSpec((1, D_MODEL, ck), lambda i, c, te, used: (te[i], 0, chunk(i, c, te, used))),
            pl.BlockSpec((1, D_MODEL, ck), lambda i, c, te, used: (te[i], 0, n_ck + chunk(i, c, te, used))),
            pl.BlockSpec((1, ck, D_MODEL), lambda i, c, te, used: (te[i], chunk(i, c, te, used), 0)),
        ],
        out_specs=pl.BlockSpec((tm, D_MODEL), lambda i, c, te, used: (i, 0)),
        scratch_shapes=[pltpu.VMEM((tm, D_MODEL), BF16)],
    )
    return pl.pallas_call(
        _expert_kernel,
        grid_spec=grid_spec,
        out_shape=jax.ShapeDtypeStruct((n_slots, D_MODEL), F32),
        compiler_params=_params("arbitrary", "arbitrary"),
        name="moe_experts",
    )(tile_expert, used, xs, w_in, w_in, w_out)


def _combine_kernel(pos_hbm, y_hbm, x_ref, gate_ref, gf_ref, o_ref, rows_ref, pos_smem, sem, pos_sem, *, final_norm):
    tm = x_ref.shape[0]
    pos_copy = _load_positions(pos_hbm, pos_smem, pos_sem)
    pos_copy.start()
    pos_copy.wait()

    def row_copy(k, r, slot):
        return pltpu.make_async_copy(y_hbm.at[pl.ds(slot, 1)], rows_ref.at[k, pl.ds(r, 1)], sem)

    def issue(r, carry):
        row_copy(0, r, pos_smem[r]).start()
        row_copy(1, r, pos_smem[tm + r]).start()
        return carry

    lax.fori_loop(0, tm, issue, 0, unroll=8)
    for k in range(2):
        pltpu.make_async_copy(y_hbm.at[pl.ds(0, tm)], rows_ref.at[k], sem).wait()
    gates = gate_ref[...]
    out = x_ref[...] + (gates[:, 0:1] * rows_ref[0] + gates[:, 1:2] * rows_ref[1])
    if final_norm:
        out = _rms(out, gf_ref[...])
    o_ref[...] = out


def _combine(pos, y, x, gates, g_final, *, final_norm):
    n_tok = x.shape[0]
    tm = ROW_TILE
    return pl.pallas_call(
        functools.partial(_combine_kernel, final_norm=final_norm),
        grid=(n_tok // tm,),
        in_specs=[
            pl.BlockSpec(memory_space=pl.ANY),
            pl.BlockSpec(memory_space=pl.ANY),
            pl.BlockSpec((tm, D_MODEL), lambda i: (i, 0)),
            pl.BlockSpec((tm, LANES), lambda i: (i, 0)),
            _resident((1, D_MODEL), lambda i: (0, 0)),
        ],
        out_specs=pl.BlockSpec((tm, D_MODEL), lambda i: (i, 0)),
        out_shape=jax.ShapeDtypeStruct((n_tok, D_MODEL), F32),
        scratch_shapes=[
            pltpu.VMEM((2, tm, D_MODEL), F32),
            pltpu.SMEM((2 * tm,), jnp.int32),
            pltpu.SemaphoreType.DMA(()),
            pltpu.SemaphoreType.DMA(()),
        ],
        compiler_params=_params("arbitrary"),
        name="moe_combine",
    )(pos, y, x, gates, g_final)


def _moe(x, g, w_router, w_in, w_out, g_final, *, final_norm):
    wr = jnp.pad(w_router, ((0, 0), (0, LANES - N_EXPERTS)))
    meta, gates, counts = _router(x, g, wr)
    pos, tile_expert, used, n_slots = _moe_plan(meta, counts)
    xs = _dispatch(pos, x, g, n_slots)
    y = _experts(tile_expert, used, xs, w_in, w_out)
    return _combine(pos, y, x, gates, g_final, final_norm=final_norm)


def _rope_tables():
    t = jnp.arange(SEQ)
    row = (t // GRID_W).astype(F32)
    col = (t % GRID_W).astype(F32)
    half = HEAD_DIM // 2
    inv_freq = ROPE_THETA ** (-jnp.arange(0, half, 2, dtype=F32) / half)
    ang = jnp.concatenate([row[:, None] * inv_freq[None, :], col[:, None] * inv_freq[None, :]], axis=-1)
    cos = jnp.repeat(jnp.cos(ang), 2, axis=-1)
    sign = jnp.where(jnp.arange(HEAD_DIM) % 2 == 0, -1.0, 1.0).astype(F32)
    sin = jnp.repeat(jnp.sin(ang), 2, axis=-1) * sign
    reps = LANES // HEAD_DIM
    return jnp.tile(cos, (1, reps)), jnp.tile(sin, (1, reps))


def _t5_bucket(rel):
    half = N_BUCKETS // 2
    ret = jnp.where(rel > 0, half, 0)
    n = jnp.abs(rel)
    max_exact = half // 2
    nf = jnp.maximum(n, 1).astype(F32)
    large = max_exact + (jnp.log(nf / max_exact) / math.log(MAX_DISTANCE / max_exact)
                         * (half - max_exact)).astype(jnp.int32)
    large = jnp.minimum(large, half - 1)
    return ret + jnp.where(n < max_exact, n, large)


def _window_bias(rel_bias):
    r = jnp.arange(Q_BLOCK)[:, None]
    c = jnp.arange(Q_BLOCK + 2 * WINDOW)[None, :]
    rel = c - WINDOW - r
    bias = rel_bias[_t5_bucket(rel)].astype(F32).transpose(2, 0, 1)
    return jnp.where((jnp.abs(rel) <= WINDOW)[None], bias, NEG_INF)


def _tile_gain(gain):
    return jnp.tile(gain.astype(F32), LANES // HEAD_DIM)[None, :]


def kernel(x_prompt, x_sample, norm_mix, norm_ffn, norm_final, w_qkv_a, q_gain_a, k_gain_a, w_o_a,
           w_qkv_b, sink_b, w_o_b, rel_bias, w_ff_in, w_ff_out, w_router, w_exp_in, w_exp_out):
    n_prompt = x_prompt.shape[0] * x_prompt.shape[1]
    x = jnp.concatenate([x_prompt.reshape(-1, D_MODEL), x_sample.reshape(-1, D_MODEL)], axis=0)
    cos, sin = _rope_tables()
    bias = _window_bias(rel_bias)
    ones_gain = jnp.ones((1, LANES), F32)
    for i in range(DEPTH):
        j = i // 2
        g_mix = norm_mix[i][None, :]
        g_ffn = norm_ffn[i][None, :]
        if i % 2 == 0:
            q, kt, v = _qkv_proj(x, g_mix, w_qkv_a[j].astype(BF16), cos, sin,
                                 _tile_gain(q_gain_a[j]), _tile_gain(k_gain_a[j]), axial=True)
            o = _global_attention(q, kt, v)
            x = _out_proj(o, w_o_a[j].astype(BF16), x)
            x = _dense_ffn(x, g_ffn, w_ff_in[j].astype(BF16), w_ff_out[j].astype(BF16))
        else:
            q, kt, v = _qkv_proj(x, g_mix, w_qkv_b[j].astype(BF16), cos, sin, ones_gain, ones_gain, axial=False)
            o = _window_attention(q, kt, v, sink_b[j].astype(F32), bias)
            x = _out_proj(o, w_o_b[j].astype(BF16), x)
            x = _moe(x, g_ffn, w_router[j], w_exp_in[j].astype(BF16), w_exp_out[j].astype(BF16),
                     norm_final[None, :], final_norm=(i == DEPTH - 1))
    y = x.reshape(-1, SEQ, D_MODEL)
    n_seq_prompt = n_prompt // SEQ
    return (y[:n_seq_prompt], y[n_seq_prompt:])
```

```python
import functools
import math

import jax
import jax.numpy as jnp
from jax import lax
from jax.experimental import pallas as pl
from jax.experimental.pallas import tpu as pltpu

D_MODEL = 1024
SEQ = 4096
DEPTH = 4
HEAD_DIM = 64
N_Q_HEADS = 16
N_KV_HEADS = 4
GROUP = N_Q_HEADS // N_KV_HEADS
Q_WIDTH = N_Q_HEADS * HEAD_DIM
KV_WIDTH = N_KV_HEADS * HEAD_DIM
QKV_WIDTH = Q_WIDTH + 2 * KV_WIDTH
GRID_W = 64
ROPE_THETA = 10000.0
Q_BLOCK = 128
WINDOW = 128
N_BUCKETS = 32
MAX_DISTANCE = 128
D_FF = 2816
N_EXPERTS = 8
D_FF_EXPERT = 3584
EPS = 1e-6
NEG_INF = -1e30

LANES = 128
VMEM_LIMIT = 56 * 1024 * 1024

ROW_TILE = 512
KV_TILE = 512
MOE_ROW_TILE = 1024
MOE_FF_CHUNK = 512

K_PAD_WIDTH = N_KV_HEADS * LANES
V_ONES = 16
LOG2E = math.log2(math.e)
Q_SCALE = HEAD_DIM ** -0.5 * LOG2E

F32 = jnp.float32
BF16 = jnp.bfloat16


def _params(*sem):
    return pltpu.CompilerParams(dimension_semantics=sem, vmem_limit_bytes=VMEM_LIMIT)


def _resident(shape, index_map):
    return pl.BlockSpec(shape, index_map, pipeline_mode=pl.Buffered(1))


def _rms(x, g):
    ms = jnp.mean(x * x, axis=-1, keepdims=True)
    return x * lax.rsqrt(ms + EPS) * g


def _qkv_kernel(x_ref, g_ref, w_ref, cos_ref, sin_ref, qg_ref, kg_ref, q_ref, k_ref, vt_ref, *, axial):
    h = _rms(x_ref[...], g_ref[...]).astype(BF16)
    acc = jnp.dot(h, w_ref[...], preferred_element_type=F32)
    scale = Q_SCALE
    n_q_tiles = Q_WIDTH // LANES
    n_k_tiles = KV_WIDTH // LANES
    if axial:
        lane = lax.broadcasted_iota(jnp.int32, (acc.shape[0], LANES), 1)
        low = lane < HEAD_DIM
        even = (lane & 1) == 0
        cos = cos_ref[...]
        sin = sin_ref[...]

        def norm_rope(seg, gain):
            sq = seg * seg
            s_lo = jnp.sum(jnp.where(low, sq, 0.0), axis=-1, keepdims=True)
            s_hi = jnp.sum(jnp.where(low, 0.0, sq), axis=-1, keepdims=True)
            ms = jnp.where(low, s_lo, s_hi) * (1.0 / HEAD_DIM)
            y = seg * lax.rsqrt(ms + EPS) * gain
            swapped = jnp.where(even, pltpu.roll(y, LANES - 1, 1), pltpu.roll(y, 1, 1))
            return y * cos + swapped * sin

        qg = qg_ref[...]
        kg = kg_ref[...]
        for j in range(n_q_tiles):
            seg = norm_rope(acc[:, j * LANES:(j + 1) * LANES], qg)
            q_ref[:, j * LANES:(j + 1) * LANES] = (seg * scale).astype(BF16)
        k_parts = [norm_rope(acc[:, Q_WIDTH + j * LANES:Q_WIDTH + (j + 1) * LANES], kg)
                   for j in range(n_k_tiles)]
        k = jnp.concatenate(k_parts, axis=1)
    else:
        q_ref[...] = (acc[:, :Q_WIDTH] * scale).astype(BF16)
        k = acc[:, Q_WIDTH:Q_WIDTH + KV_WIDTH]
    pad = jnp.zeros((k.shape[0], LANES - HEAD_DIM), BF16)
    for hd in range(N_KV_HEADS):
        k_ref[:, hd * LANES:hd * LANES + HEAD_DIM] = k[:, hd * HEAD_DIM:(hd + 1) * HEAD_DIM].astype(BF16)
        k_ref[:, hd * LANES + HEAD_DIM:(hd + 1) * LANES] = pad
    vt_ref[0, 0] = acc[:, Q_WIDTH + KV_WIDTH:].T.astype(BF16)


def _qkv_proj(x, g, w, cos, sin, qg, kg, *, axial):
    n_tok = x.shape[0]
    tm = ROW_TILE
    tiles_per_seq = SEQ // tm
    n_seq = n_tok // SEQ
    return pl.pallas_call(
        functools.partial(_qkv_kernel, axial=axial),
        grid=(n_tok // tm,),
        in_specs=[
            pl.BlockSpec((tm, D_MODEL), lambda i: (i, 0)),
            _resident((1, D_MODEL), lambda i: (0, 0)),
            _resident((D_MODEL, QKV_WIDTH), lambda i: (0, 0)),
            pl.BlockSpec((tm, LANES), lambda i: (i % tiles_per_seq, 0)),
            pl.BlockSpec((tm, LANES), lambda i: (i % tiles_per_seq, 0)),
            _resident((1, LANES), lambda i: (0, 0)),
            _resident((1, LANES), lambda i: (0, 0)),
        ],
        out_specs=[
            pl.BlockSpec((tm, Q_WIDTH), lambda i: (i, 0)),
            pl.BlockSpec((tm, K_PAD_WIDTH), lambda i: (i, 0)),
            pl.BlockSpec((1, 1, KV_WIDTH, tm), lambda i: (i // tiles_per_seq, i % tiles_per_seq, 0, 0)),
        ],
        out_shape=[
            jax.ShapeDtypeStruct((n_tok, Q_WIDTH), BF16),
            jax.ShapeDtypeStruct((n_tok, K_PAD_WIDTH), BF16),
            jax.ShapeDtypeStruct((n_seq, tiles_per_seq, KV_WIDTH, tm), BF16),
        ],
        compiler_params=_params("parallel"),
        name="qkv_axial" if axial else "qkv_window",
    )(x, g, w, cos, sin, qg, kg)


def _q_transposed(q_ref):
    return q_ref[...].astype(F32).T.astype(BF16)


def _group_queries(qt, h):
    base = h * GROUP * HEAD_DIM
    return jnp.concatenate(
        [qt[base + g * HEAD_DIM:base + (g + 1) * HEAD_DIM, :] for g in range(GROUP)], axis=1)


def _v_with_ones(vt_h, ones):
    return jnp.concatenate([vt_h, ones], axis=0)


def _store_heads(o_ref, outs, tq):
    rows = [o[:, g * tq:(g + 1) * tq] for o in outs for g in range(GROUP)]
    o_ref[...] = jnp.concatenate(rows, axis=0).T.astype(o_ref.dtype)


def _global_attn_kernel(q_ref, k_ref, vt_ref, o_ref, s_ref):
    tq = q_ref.shape[0]
    n_kv = vt_ref.shape[1]
    tk = vt_ref.shape[3]
    cols = GROUP * tq
    qt = _q_transposed(q_ref)
    queries = [_group_queries(qt, h) for h in range(N_KV_HEADS)]
    ones = jnp.ones((V_ONES, tk), BF16)

    def scores(h, j):
        start = pl.multiple_of(j * tk, tk)
        k_h = k_ref[pl.ds(start, tk), h * LANES:h * LANES + HEAD_DIM]
        return jnp.dot(k_h, queries[h], preferred_element_type=F32)

    def update(h, j, s, m, acc):
        m_new = jnp.maximum(m, jnp.max(s, axis=0, keepdims=True))
        p = jnp.exp2(s - m_new).astype(BF16)
        alpha = jnp.exp2(m - m_new)
        v_h = _v_with_ones(vt_ref[0, j, h * HEAD_DIM:(h + 1) * HEAD_DIM, :], ones)
        return m_new, alpha * acc + jnp.dot(v_h, p, preferred_element_type=F32)

    s_ref[...] = scores(0, 0)

    def body(j, carry):
        new = []
        s = s_ref[...]
        for h in range(N_KV_HEADS):
            if h + 1 < N_KV_HEADS:
                s_next = scores(h + 1, j)
            else:
                s_next = scores(0, jnp.minimum(j + 1, n_kv - 1))
            new.append(update(h, j, s, *carry[h]))
            s = s_next
        s_ref[...] = s
        return tuple(new)

    init = tuple((jnp.full((1, cols), -jnp.inf, F32), jnp.zeros((HEAD_DIM + V_ONES, cols), F32))
                 for _ in range(N_KV_HEADS))
    final = lax.fori_loop(0, n_kv, body, init)
    outs = [acc[:HEAD_DIM] / acc[HEAD_DIM:HEAD_DIM + 1] for _, acc in final]
    _store_heads(o_ref, outs, tq)


def _global_attention(q, k, vt):
    n_tok = q.shape[0]
    n_seq = n_tok // SEQ
    tq = Q_BLOCK
    nq = SEQ // tq
    return pl.pallas_call(
        _global_attn_kernel,
        grid=(n_seq, nq),
        in_specs=[
            pl.BlockSpec((tq, Q_WIDTH), lambda b, i: (b * nq + i, 0)),
            pl.BlockSpec((SEQ, K_PAD_WIDTH), lambda b, i: (b, 0)),
            pl.BlockSpec((1,) + vt.shape[1:], lambda b, i: (b, 0, 0, 0)),
        ],
        out_specs=pl.BlockSpec((tq, Q_WIDTH), lambda b, i: (b * nq + i, 0)),
        out_shape=jax.ShapeDtypeStruct((n_tok, Q_WIDTH), BF16),
        scratch_shapes=[pltpu.VMEM((vt.shape[3], GROUP * tq), F32)],
        compiler_params=_params("parallel", "arbitrary"),
        name="global_attention",
    )(q, k, vt)


def _window_attn_kernel(sink_ref, q_ref, kl_ref, kc_ref, kr_ref, vtl_ref, vtc_ref, vtr_ref, bias_ref, o_ref):
    i = pl.program_id(1)
    nq = pl.num_programs(1)
    tq = q_ref.shape[0]
    cols = GROUP * tq
    n_keys = 3 * tq
    first_key = jnp.where(i > 0, 0, tq)
    end_key = jnp.where(i < nq - 1, n_keys, 2 * tq)
    k_all = jnp.concatenate([kl_ref[...], kc_ref[...], kr_ref[...]], axis=0)
    vt_all = jnp.concatenate([vtl_ref[0, 0], vtc_ref[0, 0], vtr_ref[0, 0]], axis=1)
    key = lax.broadcasted_iota(jnp.int32, (n_keys, cols), 0)
    in_seq = (key >= first_key) & (key < end_key)
    qt = _q_transposed(q_ref)
    ones = jnp.ones((V_ONES, n_keys), BF16)
    outs = []

    def scores(h):
        return jnp.dot(k_all[:, h * LANES:h * LANES + HEAD_DIM], _group_queries(qt, h),
                       preferred_element_type=F32)

    s_next = scores(0)
    for h in range(N_KV_HEADS):
        s = s_next
        if h + 1 < N_KV_HEADS:
            s_next = scores(h + 1)
        s = jnp.where(in_seq, s + bias_ref[h], NEG_INF)
        sink = jnp.concatenate(
            [jnp.full((1, tq), sink_ref[h * GROUP + g], F32) for g in range(GROUP)], axis=1)
        m = jnp.maximum(jnp.max(s, axis=0, keepdims=True), sink)
        p = jnp.exp2(s - m).astype(BF16)
        acc = jnp.dot(_v_with_ones(vt_all[h * HEAD_DIM:(h + 1) * HEAD_DIM, :], ones), p,
                      preferred_element_type=F32)
        outs.append(acc[:HEAD_DIM] / (acc[HEAD_DIM:HEAD_DIM + 1] + jnp.exp2(sink - m)))
    _store_heads(o_ref, outs, tq)


def _window_attention(q, k, vt, sinks, bias):
    n_tok = q.shape[0]
    n_seq = n_tok // SEQ
    tq = Q_BLOCK
    nq = SEQ // tq
    per = ROW_TILE // tq
    vt_spec = lambda off: pl.BlockSpec(
        (1, 1, KV_WIDTH, tq),
        lambda b, i, s: (b, jnp.clip(i + off, 0, nq - 1) // per, 0, jnp.clip(i + off, 0, nq - 1) % per))
    k_spec = lambda off: pl.BlockSpec(
        (tq, K_PAD_WIDTH), lambda b, i, s: (b * nq + jnp.clip(i + off, 0, nq - 1), 0))
    grid_spec = pltpu.PrefetchScalarGridSpec(
        num_scalar_prefetch=1,
        grid=(n_seq, nq),
        in_specs=[
            pl.BlockSpec((tq, Q_WIDTH), lambda b, i, s: (b * nq + i, 0)),
            k_spec(-1), k_spec(0), k_spec(1),
            vt_spec(-1), vt_spec(0), vt_spec(1),
            _resident((N_KV_HEADS, 3 * tq, GROUP * tq), lambda b, i, s: (0, 0, 0)),
        ],
        out_specs=pl.BlockSpec((tq, Q_WIDTH), lambda b, i, s: (b * nq + i, 0)),
    )
    return pl.pallas_call(
        _window_attn_kernel,
        grid_spec=grid_spec,
        out_shape=jax.ShapeDtypeStruct((n_tok, Q_WIDTH), BF16),
        compiler_params=_params("parallel", "arbitrary"),
        name="window_attention",
    )(sinks, q, k, k, k, vt, vt, vt, bias)


def _out_proj_kernel(o_ref, w_ref, x_ref, y_ref):
    y_ref[...] = x_ref[...] + jnp.dot(o_ref[...], w_ref[...], preferred_element_type=F32)


def _out_proj(o, w, x):
    n_tok = x.shape[0]
    tm = ROW_TILE
    return pl.pallas_call(
        _out_proj_kernel,
        grid=(n_tok // tm,),
        in_specs=[
            pl.BlockSpec((tm, Q_WIDTH), lambda i: (i, 0)),
            _resident((Q_WIDTH, D_MODEL), lambda i: (0, 0)),
            pl.BlockSpec((tm, D_MODEL), lambda i: (i, 0)),
        ],
        out_specs=pl.BlockSpec((tm, D_MODEL), lambda i: (i, 0)),
        out_shape=jax.ShapeDtypeStruct((n_tok, D_MODEL), F32),
        compiler_params=_params("parallel"),
        name="out_proj",
    )(o, w, x)


def _ff_chunks(d_ff, width):
    return [(c, min(c + width, d_ff)) for c in range(0, d_ff, width)]


def _dense_ffn_kernel(x_ref, g_ref, w_in_ref, w_out_ref, y_ref):
    x = x_ref[...]
    h = _rms(x, g_ref[...]).astype(BF16)
    acc = x
    for c0, c1 in _ff_chunks(D_FF, 1024):
        gate = jnp.dot(h, w_in_ref[:, c0:c1], preferred_element_type=F32)
        up = jnp.dot(h, w_in_ref[:, D_FF + c0:D_FF + c1], preferred_element_type=F32)
        a = (gate * jax.nn.sigmoid(gate) * up).astype(BF16)
        acc = acc + jnp.dot(a, w_out_ref[c0:c1, :], preferred_element_type=F32)
    y_ref[...] = acc


def _dense_ffn(x, g, w_in, w_out):
    n_tok = x.shape[0]
    tm = ROW_TILE
    return pl.pallas_call(
        _dense_ffn_kernel,
        grid=(n_tok // tm,),
        in_specs=[
            pl.BlockSpec((tm, D_MODEL), lambda i: (i, 0)),
            _resident((1, D_MODEL), lambda i: (0, 0)),
            _resident((D_MODEL, 2 * D_FF), lambda i: (0, 0)),
            _resident((D_FF, D_MODEL), lambda i: (0, 0)),
        ],
        out_specs=pl.BlockSpec((tm, D_MODEL), lambda i: (i, 0)),
        out_shape=jax.ShapeDtypeStruct((n_tok, D_MODEL), F32),
        compiler_params=_params("parallel"),
        name="dense_ffn",
    )(x, g, w_in, w_out)


def _router_kernel(x_ref, g_ref, wr_ref, meta_ref, gate_ref, count_ref, base_ref):
    @pl.when(pl.program_id(0) == 0)
    def _():
        base_ref[...] = jnp.zeros_like(base_ref)

    hf = _rms(x_ref[...], g_ref[...])
    logits = jnp.dot(hf, wr_ref[...], preferred_element_type=F32, precision=lax.Precision.HIGHEST)
    lane = lax.broadcasted_iota(jnp.int32, logits.shape, 1)
    logits = jnp.where(lane < N_EXPERTS, logits, -jnp.inf)
    m1 = jnp.max(logits, axis=-1, keepdims=True)
    i1 = jnp.min(jnp.where(logits == m1, lane, LANES), axis=-1, keepdims=True)
    rest = jnp.where(lane == i1, -jnp.inf, logits)
    m2 = jnp.max(rest, axis=-1, keepdims=True)
    i2 = jnp.min(jnp.where(rest == m2, lane, LANES), axis=-1, keepdims=True)
    e2 = jnp.exp(m2 - m1)
    g1 = 1.0 / (1.0 + e2)
    g2 = e2 / (1.0 + e2)

    chosen = jnp.where((lane == i1) | (lane == i2), 1.0, 0.0)
    tm = chosen.shape[0]
    r_idx = lax.broadcasted_iota(jnp.int32, (tm, tm), 0)
    c_idx = lax.broadcasted_iota(jnp.int32, (tm, tm), 1)
    earlier = jnp.where(c_idx < r_idx, 1.0, 0.0).astype(BF16)
    before = jnp.dot(earlier, chosen.astype(BF16), preferred_element_type=F32) + base_ref[...]
    r1 = jnp.sum(jnp.where(lane == i1, before, 0.0), axis=-1, keepdims=True).astype(jnp.int32)
    r2 = jnp.sum(jnp.where(lane == i2, before, 0.0), axis=-1, keepdims=True).astype(jnp.int32)
    meta_ref[...] = jnp.where(lane == 0, i1, jnp.where(lane == 1, i2, jnp.where(lane == 2, r1, jnp.where(lane == 3, r2, 0))))
    gate_ref[...] = jnp.where(lane == 0, g1, jnp.where(lane == 1, g2, 0.0))
    base_ref[...] += jnp.sum(chosen, axis=0, keepdims=True)
    count_ref[...] = base_ref[...]


def _router(x, g, wr_padded):
    n_tok = x.shape[0]
    tm = ROW_TILE
    return pl.pallas_call(
        _router_kernel,
        grid=(n_tok // tm,),
        in_specs=[
            pl.BlockSpec((tm, D_MODEL), lambda i: (i, 0)),
            _resident((1, D_MODEL), lambda i: (0, 0)),
            _resident((D_MODEL, LANES), lambda i: (0, 0)),
        ],
        out_specs=[
            pl.BlockSpec((tm, LANES), lambda i: (i, 0)),
            pl.BlockSpec((tm, LANES), lambda i: (i, 0)),
            pl.BlockSpec((1, LANES), lambda i: (0, 0)),
        ],
        out_shape=[
            jax.ShapeDtypeStruct((n_tok, LANES), jnp.int32),
            jax.ShapeDtypeStruct((n_tok, LANES), F32),
            jax.ShapeDtypeStruct((1, LANES), F32),
        ],
        scratch_shapes=[pltpu.VMEM((1, LANES), F32)],
        compiler_params=_params("arbitrary"),
        name="router",
    )(x, g, wr_padded)


def _moe_plan(meta, counts):
    tm = MOE_ROW_TILE
    n_tok = meta.shape[0]
    n_tiles = (2 * n_tok) // tm + N_EXPERTS
    cnt = counts[0, :N_EXPERTS].astype(jnp.int32)
    tiles = (cnt + tm - 1) // tm
    tile_end = jnp.cumsum(tiles)
    offsets = (tile_end - tiles) * tm
    tile_ids = jnp.arange(n_tiles, dtype=jnp.int32)
    tile_expert = jnp.minimum(jnp.sum(tile_ids[:, None] >= tile_end[None, :], axis=1), N_EXPERTS - 1)
    used = tile_end[-1:].astype(jnp.int32)
    pos1 = jnp.take(offsets, meta[:, 0]) + meta[:, 2]
    pos2 = jnp.take(offsets, meta[:, 1]) + meta[:, 3]
    pos = jnp.stack([pos1.reshape(-1, ROW_TILE), pos2.reshape(-1, ROW_TILE)], axis=1).reshape(-1)
    return pos.astype(jnp.int32), tile_expert.astype(jnp.int32), used, n_tiles * tm


def _load_positions(pos_hbm, pos_smem, pos_sem):
    n = pos_smem.shape[0]
    start = pl.multiple_of(pl.program_id(0) * n, n)
    return pltpu.make_async_copy(pos_hbm.at[pl.ds(start, n)], pos_smem, pos_sem)


def _dispatch_kernel(pos_hbm, x_ref, g_ref, xs_in, xs_out, h_ref, pos_smem, sem, pos_sem):
    del xs_in
    tm = h_ref.shape[0]
    pos_copy = _load_positions(pos_hbm, pos_smem, pos_sem)
    pos_copy.start()
    h_ref[...] = _rms(x_ref[...], g_ref[...])
    pos_copy.wait()

    def row_copy(r, slot):
        return pltpu.make_async_copy(h_ref.at[pl.ds(r, 1)], xs_out.at[pl.ds(slot, 1)], sem)

    def issue(r, carry):
        row_copy(r, pos_smem[r]).start()
        row_copy(r, pos_smem[tm + r]).start()
        return carry

    lax.fori_loop(0, tm, issue, 0, unroll=8)
    for _ in range(2):
        pltpu.make_async_copy(h_ref, xs_out.at[pl.ds(0, tm)], sem).wait()


def _dispatch(pos, x, g, n_slots):
    n_tok = x.shape[0]
    tm = ROW_TILE
    xs0 = jnp.zeros((n_slots, D_MODEL), F32)
    return pl.pallas_call(
        _dispatch_kernel,
        grid=(n_tok // tm,),
        in_specs=[
            pl.BlockSpec(memory_space=pl.ANY),
            pl.BlockSpec((tm, D_MODEL), lambda i: (i, 0)),
            _resident((1, D_MODEL), lambda i: (0, 0)),
            pl.BlockSpec(memory_space=pl.ANY),
        ],
        out_specs=pl.BlockSpec(memory_space=pl.ANY),
        out_shape=jax.ShapeDtypeStruct((n_slots, D_MODEL), F32),
        scratch_shapes=[
            pltpu.VMEM((tm, D_MODEL), F32),
            pltpu.SMEM((2 * tm,), jnp.int32),
            pltpu.SemaphoreType.DMA(()),
            pltpu.SemaphoreType.DMA(()),
        ],
        input_output_aliases={3: 0},
        compiler_params=_params("arbitrary"),
        name="moe_dispatch",
    )(pos, x, g, xs0)


def _expert_kernel(te_ref, used_ref, xs_ref, wg_ref, wu_ref, wo_ref, y_ref, hb_ref):
    del te_ref
    c = pl.program_id(1)
    in_use = pl.program_id(0) < used_ref[0]

    @pl.when(jnp.logical_not(in_use) & (c == 0))
    def _():
        y_ref[...] = jnp.zeros_like(y_ref)

    @pl.when(in_use)
    def _():
        @pl.when(c == 0)
        def _():
            hb_ref[...] = xs_ref[...].astype(BF16)
            y_ref[...] = jnp.zeros_like(y_ref)

        h = hb_ref[...]
        gate = jnp.dot(h, wg_ref[0], preferred_element_type=F32)
        up = jnp.dot(h, wu_ref[0], preferred_element_type=F32)
        a = (gate * jax.nn.sigmoid(gate) * up).astype(BF16)
        y_ref[...] += jnp.dot(a, wo_ref[0], preferred_element_type=F32)


def _experts(tile_expert, used, xs, w_in, w_out):
    tm = MOE_ROW_TILE
    ck = MOE_FF_CHUNK
    n_ck = D_FF_EXPERT // ck
    n_slots = xs.shape[0]

    def chunk(i, c, te, used):
        return jnp.where(i < used[0], c, n_ck - 1)

    grid_spec = pltpu.PrefetchScalarGridSpec(
        num_scalar_prefetch=2,
        grid=(n_slots // tm, n_ck),
        in_specs=[
            pl.BlockSpec((tm, D_MODEL), lambda i, c, te, used: (i, 0)),
            pl.BlockSpec((1, D_MODEL, ck), lambda i, c, te, used: (te[i], 0, chunk(i, c, te, used))),
            pl.BlockSpec((1, D_MODEL, ck), lambda i, c, te, used: (te[i], 0, n_ck + chunk(i, c, te, used))),
            pl.BlockSpec((1, ck, D_MODEL), lambda i, c, te, used: (te[i], chunk(i, c, te, used), 0)),
        ],
        out_specs=pl.BlockSpec((tm, D_MODEL), lambda i, c, te, used: (i, 0)),
        scratch_shapes=[pltpu.VMEM((tm, D_MODEL), BF16)],
    )
    return pl.pallas_call(
        _expert_kernel,
        grid_spec=grid_spec,
        out_shape=jax.ShapeDtypeStruct((n_slots, D_MODEL), F32),
        compiler_params=_params("arbitrary", "arbitrary"),
        name="moe_experts",
    )(tile_expert, used, xs, w_in, w_in, w_out)


def _combine_kernel(pos_hbm, y_hbm, x_ref, gate_ref, gf_ref, o_ref, rows_ref, pos_smem, sem, pos_sem, *, final_norm):
    tm = x_ref.shape[0]
    pos_copy = _load_positions(pos_hbm, pos_smem, pos_sem)
    pos_copy.start()
    pos_copy.wait()

    def row_copy(k, r, slot):
        return pltpu.make_async_copy(y_hbm.at[pl.ds(slot, 1)], rows_ref.at[k, pl.ds(r, 1)], sem)

    def issue(r, carry):
        row_copy(0, r, pos_smem[r]).start()
        row_copy(1, r, pos_smem[tm + r]).start()
        return carry

    lax.fori_loop(0, tm, issue, 0, unroll=8)
    for k in range(2):
        pltpu.make_async_copy(y_hbm.at[pl.ds(0, tm)], rows_ref.at[k], sem).wait()
    gates = gate_ref[...]
    out = x_ref[...] + (gates[:, 0:1] * rows_ref[0] + gates[:, 1:2] * rows_ref[1])
    if final_norm:
        out = _rms(out, gf_ref[...])
    o_ref[...] = out


def _combine(pos, y, x, gates, g_final, *, final_norm):
    n_tok = x.shape[0]
    tm = ROW_TILE
    return pl.pallas_call(
        functools.partial(_combine_kernel, final_norm=final_norm),
        grid=(n_tok // tm,),
        in_specs=[
            pl.BlockSpec(memory_space=pl.ANY),
            pl.BlockSpec(memory_space=pl.ANY),
            pl.BlockSpec((tm, D_MODEL), lambda i: (i, 0)),
            pl.BlockSpec((tm, LANES), lambda i: (i, 0)),
            _resident((1, D_MODEL), lambda i: (0, 0)),
        ],
        out_specs=pl.BlockSpec((tm, D_MODEL), lambda i: (i, 0)),
        out_shape=jax.ShapeDtypeStruct((n_tok, D_MODEL), F32),
        scratch_shapes=[
            pltpu.VMEM((2, tm, D_MODEL), F32),
            pltpu.SMEM((2 * tm,), jnp.int32),
            pltpu.SemaphoreType.DMA(()),
            pltpu.SemaphoreType.DMA(()),
        ],
        compiler_params=_params("arbitrary"),
        name="moe_combine",
    )(pos, y, x, gates, g_final)


def _moe(x, g, w_router, w_in, w_out, g_final, *, final_norm):
    wr = jnp.pad(w_router, ((0, 0), (0, LANES - N_EXPERTS)))
    meta, gates, counts = _router(x, g, wr)
    pos, tile_expert, used, n_slots = _moe_plan(meta, counts)
    xs = _dispatch(pos, x, g, n_slots)
    y = _experts(tile_expert, used, xs, w_in, w_out)
    return _combine(pos, y, x, gates, g_final, final_norm=final_norm)


def _rope_tables():
    t = jnp.arange(SEQ)
    row = (t // GRID_W).astype(F32)
    col = (t % GRID_W).astype(F32)
    half = HEAD_DIM // 2
    inv_freq = ROPE_THETA ** (-jnp.arange(0, half, 2, dtype=F32) / half)
    ang = jnp.concatenate([row[:, None] * inv_freq[None, :], col[:, None] * inv_freq[None, :]], axis=-1)
    cos = jnp.repeat(jnp.cos(ang), 2, axis=-1)
    sign = jnp.where(jnp.arange(HEAD_DIM) % 2 == 0, -1.0, 1.0).astype(F32)
    sin = jnp.repeat(jnp.sin(ang), 2, axis=-1) * sign
    reps = LANES // HEAD_DIM
    return jnp.tile(cos, (1, reps)), jnp.tile(sin, (1, reps))


def _t5_bucket(rel):
    half = N_BUCKETS // 2
    ret = jnp.where(rel > 0, half, 0)
    n = jnp.abs(rel)
    max_exact = half // 2
    nf = jnp.maximum(n, 1).astype(F32)
    large = max_exact + (jnp.log(nf / max_exact) / math.log(MAX_DISTANCE / max_exact)
                         * (half - max_exact)).astype(jnp.int32)
    large = jnp.minimum(large, half - 1)
    return ret + jnp.where(n < max_exact, n, large)


def _window_bias(rel_bias):
    r = jnp.arange(Q_BLOCK)[:, None]
    c = jnp.arange(Q_BLOCK + 2 * WINDOW)[None, :]
    rel = c - WINDOW - r
    bias = rel_bias[_t5_bucket(rel)].astype(F32) * LOG2E
    bias = jnp.where((jnp.abs(rel) <= WINDOW)[:, :, None], bias, NEG_INF)
    bias = bias.reshape(Q_BLOCK, Q_BLOCK + 2 * WINDOW, N_KV_HEADS, GROUP).transpose(2, 1, 3, 0)
    return bias.reshape(N_KV_HEADS, Q_BLOCK + 2 * WINDOW, GROUP * Q_BLOCK)


def _tile_gain(gain):
    return jnp.tile(gain.astype(F32), LANES // HEAD_DIM)[None, :]


def kernel(x_prompt, x_sample, norm_mix, norm_ffn, norm_final, w_qkv_a, q_gain_a, k_gain_a, w_o_a,
           w_qkv_b, sink_b, w_o_b, rel_bias, w_ff_in, w_ff_out, w_router, w_exp_in, w_exp_out):
    n_prompt = x_prompt.shape[0] * x_prompt.shape[1]
    x = jnp.concatenate([x_prompt.reshape(-1, D_MODEL), x_sample.reshape(-1, D_MODEL)], axis=0)
    cos, sin = _rope_tables()
    bias = _window_bias(rel_bias)
    ones_gain = jnp.ones((1, LANES), F32)
    for i in range(DEPTH):
        j = i // 2
        g_mix = norm_mix[i][None, :]
        g_ffn = norm_ffn[i][None, :]
        if i % 2 == 0:
            q, k, vt = _qkv_proj(x, g_mix, w_qkv_a[j].astype(BF16), cos, sin,
                                 _tile_gain(q_gain_a[j]), _tile_gain(k_gain_a[j]), axial=True)
            o = _global_attention(q, k, vt)
            x = _out_proj(o, w_o_a[j].astype(BF16), x)
            x = _dense_ffn(x, g_ffn, w_ff_in[j].astype(BF16), w_ff_out[j].astype(BF16))
        else:
            q, k, vt = _qkv_proj(x, g_mix, w_qkv_b[j].astype(BF16), cos, sin, ones_gain, ones_gain, axial=False)
            o = _window_attention(q, k, vt, sink_b[j].astype(F32) * LOG2E, bias)
            x = _out_proj(o, w_o_b[j].astype(BF16), x)
            x = _moe(x, g_ffn, w_router[j], w_exp_in[j].astype(BF16), w_exp_out[j].astype(BF16),
                     norm_final[None, :], final_norm=(i == DEPTH - 1))
    y = x.reshape(-1, SEQ, D_MODEL)
    n_seq_prompt = n_prompt // SEQ
    return (y[:n_seq_prompt], y[n_seq_prompt:])
```

```python
import functools
import math

import jax
import jax.numpy as jnp
from jax import lax
from jax.experimental import pallas as pl
from jax.experimental.pallas import tpu as pltpu

D_MODEL = 1024
SEQ = 4096
DEPTH = 4
HEAD_DIM = 64
N_Q_HEADS = 16
N_KV_HEADS = 4
GROUP = N_Q_HEADS // N_KV_HEADS
Q_WIDTH = N_Q_HEADS * HEAD_DIM
KV_WIDTH = N_KV_HEADS * HEAD_DIM
QKV_WIDTH = Q_WIDTH + 2 * KV_WIDTH
GRID_W = 64
ROPE_THETA = 10000.0
Q_BLOCK = 128
WINDOW = 128
N_BUCKETS = 32
MAX_DISTANCE = 128
D_FF = 2816
N_EXPERTS = 8
D_FF_EXPERT = 3584
EPS = 1e-6
NEG_INF = -1e30

LANES = 128
VMEM_LIMIT = 56 * 1024 * 1024

ROW_TILE = 512
GLOBAL_Q_TILE = 256
MOE_ROW_TILE = 1024
MOE_FF_CHUNK = 512

K_PAD_WIDTH = N_KV_HEADS * LANES
V_ONES = 16
LOG2E = math.log2(math.e)
Q_SCALE = HEAD_DIM ** -0.5 * LOG2E

F32 = jnp.float32
BF16 = jnp.bfloat16


def _params(*sem):
    return pltpu.CompilerParams(dimension_semantics=sem, vmem_limit_bytes=VMEM_LIMIT)


def _resident(shape, index_map):
    return pl.BlockSpec(shape, index_map, pipeline_mode=pl.Buffered(1))


def _rms(x, g):
    ms = jnp.mean(x * x, axis=-1, keepdims=True)
    return x * lax.rsqrt(ms + EPS) * g


def _qkv_kernel(x_ref, g_ref, w_ref, cos_ref, sin_ref, qg_ref, kg_ref, q_ref, k_ref, vt_ref, *, axial):
    h = _rms(x_ref[...], g_ref[...]).astype(BF16)
    acc = jnp.dot(h, w_ref[...], preferred_element_type=F32)
    scale = Q_SCALE
    n_q_tiles = Q_WIDTH // LANES
    n_k_tiles = KV_WIDTH // LANES
    if axial:
        lane = lax.broadcasted_iota(jnp.int32, (acc.shape[0], LANES), 1)
        low = lane < HEAD_DIM
        even = (lane & 1) == 0
        cos = cos_ref[...]
        sin = sin_ref[...]

        def norm_rope(seg, gain):
            sq = seg * seg
            s_lo = jnp.sum(jnp.where(low, sq, 0.0), axis=-1, keepdims=True)
            s_hi = jnp.sum(jnp.where(low, 0.0, sq), axis=-1, keepdims=True)
            ms = jnp.where(low, s_lo, s_hi) * (1.0 / HEAD_DIM)
            y = seg * lax.rsqrt(ms + EPS) * gain
            swapped = jnp.where(even, pltpu.roll(y, LANES - 1, 1), pltpu.roll(y, 1, 1))
            return y * cos + swapped * sin

        qg = qg_ref[...]
        kg = kg_ref[...]
        for j in range(n_q_tiles):
            seg = norm_rope(acc[:, j * LANES:(j + 1) * LANES], qg)
            q_ref[:, j * LANES:(j + 1) * LANES] = (seg * scale).astype(BF16)
        k_parts = [norm_rope(acc[:, Q_WIDTH + j * LANES:Q_WIDTH + (j + 1) * LANES], kg)
                   for j in range(n_k_tiles)]
        k = jnp.concatenate(k_parts, axis=1)
    else:
        q_ref[...] = (acc[:, :Q_WIDTH] * scale).astype(BF16)
        k = acc[:, Q_WIDTH:Q_WIDTH + KV_WIDTH]
    pad = jnp.zeros((k.shape[0], LANES - HEAD_DIM), BF16)
    for hd in range(N_KV_HEADS):
        k_ref[:, hd * LANES:hd * LANES + HEAD_DIM] = k[:, hd * HEAD_DIM:(hd + 1) * HEAD_DIM].astype(BF16)
        k_ref[:, hd * LANES + HEAD_DIM:(hd + 1) * LANES] = pad
    vt_ref[0, 0] = acc[:, Q_WIDTH + KV_WIDTH:].T.astype(BF16)


def _qkv_proj(x, g, w, cos, sin, qg, kg, *, axial):
    n_tok = x.shape[0]
    tm = ROW_TILE
    tiles_per_seq = SEQ // tm
    n_seq = n_tok // SEQ
    return pl.pallas_call(
        functools.partial(_qkv_kernel, axial=axial),
        grid=(n_tok // tm,),
        in_specs=[
            pl.BlockSpec((tm, D_MODEL), lambda i: (i, 0)),
            _resident((1, D_MODEL), lambda i: (0, 0)),
            _resident((D_MODEL, QKV_WIDTH), lambda i: (0, 0)),
            pl.BlockSpec((tm, LANES), lambda i: (i % tiles_per_seq, 0)),
            pl.BlockSpec((tm, LANES), lambda i: (i % tiles_per_seq, 0)),
            _resident((1, LANES), lambda i: (0, 0)),
            _resident((1, LANES), lambda i: (0, 0)),
        ],
        out_specs=[
            pl.BlockSpec((tm, Q_WIDTH), lambda i: (i, 0)),
            pl.BlockSpec((tm, K_PAD_WIDTH), lambda i: (i, 0)),
            pl.BlockSpec((1, 1, KV_WIDTH, tm), lambda i: (i // tiles_per_seq, i % tiles_per_seq, 0, 0)),
        ],
        out_shape=[
            jax.ShapeDtypeStruct((n_tok, Q_WIDTH), BF16),
            jax.ShapeDtypeStruct((n_tok, K_PAD_WIDTH), BF16),
            jax.ShapeDtypeStruct((n_seq, tiles_per_seq, KV_WIDTH, tm), BF16),
        ],
        compiler_params=_params("parallel"),
        name="qkv_axial" if axial else "qkv_window",
    )(x, g, w, cos, sin, qg, kg)


def _q_transposed(q_ref):
    return q_ref[...].astype(F32).T.astype(BF16)


def _group_queries(qt, h):
    base = h * GROUP * HEAD_DIM
    return jnp.concatenate(
        [qt[base + g * HEAD_DIM:base + (g + 1) * HEAD_DIM, :] for g in range(GROUP)], axis=1)


def _v_with_ones(vt_h, ones):
    return jnp.concatenate([vt_h, ones], axis=0)


def _store_heads(o_ref, outs, tq):
    rows = [o[:, g * tq:(g + 1) * tq] for o in outs for g in range(GROUP)]
    o_ref[...] = jnp.concatenate(rows, axis=0).T.astype(o_ref.dtype)


def _global_attn_kernel(q_ref, k_ref, vt_ref, o_ref, s_ref):
    tq = q_ref.shape[0]
    n_kv = vt_ref.shape[1]
    tk = vt_ref.shape[3]
    cols = GROUP * tq
    qt = _q_transposed(q_ref)
    queries = [_group_queries(qt, h) for h in range(N_KV_HEADS)]
    ones = jnp.ones((V_ONES, tk), BF16)

    def scores(h, j):
        start = pl.multiple_of(j * tk, tk)
        k_h = k_ref[pl.ds(start, tk), h * LANES:h * LANES + HEAD_DIM]
        return jnp.dot(k_h, queries[h], preferred_element_type=F32)

    def update(h, j, s, m, acc):
        m_new = jnp.maximum(m, jnp.max(s, axis=0, keepdims=True))
        p = jnp.exp2(s - m_new).astype(BF16)
        alpha = jnp.exp2(m - m_new)
        v_h = _v_with_ones(vt_ref[0, j, h * HEAD_DIM:(h + 1) * HEAD_DIM, :], ones)
        return m_new, alpha * acc + jnp.dot(v_h, p, preferred_element_type=F32)

    s_ref[...] = scores(0, 0)

    def body(j, carry):
        new = []
        s = s_ref[...]
        for h in range(N_KV_HEADS):
            if h + 1 < N_KV_HEADS:
                s_next = scores(h + 1, j)
            else:
                s_next = scores(0, jnp.minimum(j + 1, n_kv - 1))
            new.append(update(h, j, s, *carry[h]))
            s = s_next
        s_ref[...] = s
        return tuple(new)

    init = tuple((jnp.full((1, cols), -jnp.inf, F32), jnp.zeros((HEAD_DIM + V_ONES, cols), F32))
                 for _ in range(N_KV_HEADS))
    final = lax.fori_loop(0, n_kv, body, init, unroll=2)
    outs = [acc[:HEAD_DIM] / acc[HEAD_DIM:HEAD_DIM + 1] for _, acc in final]
    _store_heads(o_ref, outs, tq)


def _global_attention(q, k, vt):
    n_tok = q.shape[0]
    n_seq = n_tok // SEQ
    tq = GLOBAL_Q_TILE
    nq = SEQ // tq
    return pl.pallas_call(
        _global_attn_kernel,
        grid=(n_seq, nq),
        in_specs=[
            pl.BlockSpec((tq, Q_WIDTH), lambda b, i: (b * nq + i, 0)),
            pl.BlockSpec((SEQ, K_PAD_WIDTH), lambda b, i: (b, 0)),
            pl.BlockSpec((1,) + vt.shape[1:], lambda b, i: (b, 0, 0, 0)),
        ],
        out_specs=pl.BlockSpec((tq, Q_WIDTH), lambda b, i: (b * nq + i, 0)),
        out_shape=jax.ShapeDtypeStruct((n_tok, Q_WIDTH), BF16),
        scratch_shapes=[pltpu.VMEM((vt.shape[3], GROUP * tq), F32)],
        compiler_params=_params("parallel", "arbitrary"),
        name="global_attention",
    )(q, k, vt)


def _window_attn_kernel(sink_ref, q_ref, kl_ref, kc_ref, kr_ref, vtl_ref, vtc_ref, vtr_ref, bias_ref, o_ref):
    tq = q_ref.shape[0]
    n_keys = 3 * tq
    k_all = jnp.concatenate([kl_ref[...], kc_ref[...], kr_ref[...]], axis=0)
    vt_all = jnp.concatenate([vtl_ref[0, 0], vtc_ref[0, 0], vtr_ref[0, 0]], axis=1)
    qt = _q_transposed(q_ref)
    ones = jnp.ones((V_ONES, n_keys), BF16)
    outs = []

    def scores(h):
        return jnp.dot(k_all[:, h * LANES:h * LANES + HEAD_DIM], _group_queries(qt, h),
                       preferred_element_type=F32)

    s_next = scores(0)
    for h in range(N_KV_HEADS):
        s = s_next
        if h + 1 < N_KV_HEADS:
            s_next = scores(h + 1)
        s = s + bias_ref[0, h]
        sink = jnp.concatenate(
            [jnp.full((1, tq), sink_ref[h * GROUP + g], F32) for g in range(GROUP)], axis=1)
        m = jnp.maximum(jnp.max(s, axis=0, keepdims=True), sink)
        p = jnp.exp2(s - m).astype(BF16)
        acc = jnp.dot(_v_with_ones(vt_all[h * HEAD_DIM:(h + 1) * HEAD_DIM, :], ones), p,
                      preferred_element_type=F32)
        outs.append(acc[:HEAD_DIM] / (acc[HEAD_DIM:HEAD_DIM + 1] + jnp.exp2(sink - m)))
    _store_heads(o_ref, outs, tq)


def _window_attention(q, k, vt, sinks, bias):
    n_tok = q.shape[0]
    n_seq = n_tok // SEQ
    tq = Q_BLOCK
    nq = SEQ // tq
    per = ROW_TILE // tq
    vt_spec = lambda off: pl.BlockSpec(
        (1, 1, KV_WIDTH, tq),
        lambda b, i, s: (b, jnp.clip(i + off, 0, nq - 1) // per, 0, jnp.clip(i + off, 0, nq - 1) % per))
    k_spec = lambda off: pl.BlockSpec(
        (tq, K_PAD_WIDTH), lambda b, i, s: (b * nq + jnp.clip(i + off, 0, nq - 1), 0))
    grid_spec = pltpu.PrefetchScalarGridSpec(
        num_scalar_prefetch=1,
        grid=(n_seq, nq),
        in_specs=[
            pl.BlockSpec((tq, Q_WIDTH), lambda b, i, s: (b * nq + i, 0)),
            k_spec(-1), k_spec(0), k_spec(1),
            vt_spec(-1), vt_spec(0), vt_spec(1),
            pl.BlockSpec((1, N_KV_HEADS, 3 * tq, GROUP * tq),
                         lambda b, i, s: (jnp.where(i == 0, 1, jnp.where(i == nq - 1, 2, 0)), 0, 0, 0)),
        ],
        out_specs=pl.BlockSpec((tq, Q_WIDTH), lambda b, i, s: (b * nq + i, 0)),
    )
    return pl.pallas_call(
        _window_attn_kernel,
        grid_spec=grid_spec,
        out_shape=jax.ShapeDtypeStruct((n_tok, Q_WIDTH), BF16),
        compiler_params=_params("parallel", "arbitrary"),
        name="window_attention",
    )(sinks, q, k, k, k, vt, vt, vt, bias)


def _attn_residual(o_ref, wo_ref, x_ref):
    return x_ref[...] + jnp.dot(o_ref[...], wo_ref[...], preferred_element_type=F32)


def _attn_residual_specs(tm):
    return [
        pl.BlockSpec((tm, Q_WIDTH), lambda i: (i, 0)),
        _resident((Q_WIDTH, D_MODEL), lambda i: (0, 0)),
        pl.BlockSpec((tm, D_MODEL), lambda i: (i, 0)),
    ]


def _ff_chunks(d_ff, width):
    return [(c, min(c + width, d_ff)) for c in range(0, d_ff, width)]


def _dense_ffn_kernel(o_ref, wo_ref, x_ref, g_ref, w_in_ref, w_out_ref, y_ref):
    x = _attn_residual(o_ref, wo_ref, x_ref)
    h = _rms(x, g_ref[...]).astype(BF16)
    acc = x
    for c0, c1 in _ff_chunks(D_FF, 1024):
        gate = jnp.dot(h, w_in_ref[:, c0:c1], preferred_element_type=F32)
        up = jnp.dot(h, w_in_ref[:, D_FF + c0:D_FF + c1], preferred_element_type=F32)
        a = (gate * jax.nn.sigmoid(gate) * up).astype(BF16)
        acc = acc + jnp.dot(a, w_out_ref[c0:c1, :], preferred_element_type=F32)
    y_ref[...] = acc


def _dense_ffn(o, w_o, x, g, w_in, w_out):
    n_tok = x.shape[0]
    tm = ROW_TILE
    return pl.pallas_call(
        _dense_ffn_kernel,
        grid=(n_tok // tm,),
        in_specs=_attn_residual_specs(tm) + [
            _resident((1, D_MODEL), lambda i: (0, 0)),
            _resident((D_MODEL, 2 * D_FF), lambda i: (0, 0)),
            _resident((D_FF, D_MODEL), lambda i: (0, 0)),
        ],
        out_specs=pl.BlockSpec((tm, D_MODEL), lambda i: (i, 0)),
        out_shape=jax.ShapeDtypeStruct((n_tok, D_MODEL), F32),
        compiler_params=_params("parallel"),
        name="dense_ffn",
    )(o, w_o, x, g, w_in, w_out)


def _router_kernel(o_ref, wo_ref, x_ref, g_ref, wr_ref, xn_ref, meta_ref, gate_ref, count_ref, base_ref):
    @pl.when(pl.program_id(0) == 0)
    def _():
        base_ref[...] = jnp.zeros_like(base_ref)

    x = _attn_residual(o_ref, wo_ref, x_ref)
    xn_ref[...] = x
    hf = _rms(x, g_ref[...])
    logits = jnp.dot(hf, wr_ref[...], preferred_element_type=F32, precision=lax.Precision.HIGHEST)
    lane = lax.broadcasted_iota(jnp.int32, logits.shape, 1)
    logits = jnp.where(lane < N_EXPERTS, logits, -jnp.inf)
    m1 = jnp.max(logits, axis=-1, keepdims=True)
    i1 = jnp.min(jnp.where(logits == m1, lane, LANES), axis=-1, keepdims=True)
    rest = jnp.where(lane == i1, -jnp.inf, logits)
    m2 = jnp.max(rest, axis=-1, keepdims=True)
    i2 = jnp.min(jnp.where(rest == m2, lane, LANES), axis=-1, keepdims=True)
    e2 = jnp.exp(m2 - m1)
    g1 = 1.0 / (1.0 + e2)
    g2 = e2 / (1.0 + e2)

    chosen = jnp.where((lane == i1) | (lane == i2), 1.0, 0.0)
    tm = chosen.shape[0]
    r_idx = lax.broadcasted_iota(jnp.int32, (tm, tm), 0)
    c_idx = lax.broadcasted_iota(jnp.int32, (tm, tm), 1)
    earlier = jnp.where(c_idx < r_idx, 1.0, 0.0).astype(BF16)
    before = jnp.dot(earlier, chosen.astype(BF16), preferred_element_type=F32) + base_ref[...]
    r1 = jnp.sum(jnp.where(lane == i1, before, 0.0), axis=-1, keepdims=True).astype(jnp.int32)
    r2 = jnp.sum(jnp.where(lane == i2, before, 0.0), axis=-1, keepdims=True).astype(jnp.int32)
    meta_ref[...] = jnp.where(lane == 0, i1, jnp.where(lane == 1, i2, jnp.where(lane == 2, r1, jnp.where(lane == 3, r2, 0))))
    gate_ref[...] = jnp.where(lane == 0, g1, jnp.where(lane == 1, g2, 0.0))
    base_ref[...] += jnp.sum(chosen, axis=0, keepdims=True)
    count_ref[...] = base_ref[...]


def _router(o, w_o, x, g, wr_padded):
    n_tok = x.shape[0]
    tm = ROW_TILE
    return pl.pallas_call(
        _router_kernel,
        grid=(n_tok // tm,),
        in_specs=_attn_residual_specs(tm) + [
            _resident((1, D_MODEL), lambda i: (0, 0)),
            _resident((D_MODEL, LANES), lambda i: (0, 0)),
        ],
        out_specs=[
            pl.BlockSpec((tm, D_MODEL), lambda i: (i, 0)),
            pl.BlockSpec((tm, LANES), lambda i: (i, 0)),
            pl.BlockSpec((tm, LANES), lambda i: (i, 0)),
            pl.BlockSpec((1, LANES), lambda i: (0, 0)),
        ],
        out_shape=[
            jax.ShapeDtypeStruct((n_tok, D_MODEL), F32),
            jax.ShapeDtypeStruct((n_tok, LANES), jnp.int32),
            jax.ShapeDtypeStruct((n_tok, LANES), F32),
            jax.ShapeDtypeStruct((1, LANES), F32),
        ],
        scratch_shapes=[pltpu.VMEM((1, LANES), F32)],
        compiler_params=_params("arbitrary"),
        name="router",
    )(o, w_o, x, g, wr_padded)


def _moe_plan(meta, counts):
    tm = MOE_ROW_TILE
    n_tok = meta.shape[0]
    n_tiles = (2 * n_tok) // tm + N_EXPERTS
    cnt = counts[0, :N_EXPERTS].astype(jnp.int32)
    tiles = (cnt + tm - 1) // tm
    tile_end = jnp.cumsum(tiles)
    offsets = (tile_end - tiles) * tm
    tile_ids = jnp.arange(n_tiles, dtype=jnp.int32)
    tile_expert = jnp.minimum(jnp.sum(tile_ids[:, None] >= tile_end[None, :], axis=1), N_EXPERTS - 1)
    used = tile_end[-1:].astype(jnp.int32)
    pos1 = jnp.take(offsets, meta[:, 0]) + meta[:, 2]
    pos2 = jnp.take(offsets, meta[:, 1]) + meta[:, 3]
    pos = jnp.stack([pos1.reshape(-1, ROW_TILE), pos2.reshape(-1, ROW_TILE)], axis=1).reshape(-1)
    return pos.astype(jnp.int32), tile_expert.astype(jnp.int32), used, n_tiles * tm


def _load_positions(pos_hbm, pos_smem, pos_sem, tile):
    n = pos_smem.shape[0]
    start = pl.multiple_of(tile * n, n)
    return pltpu.make_async_copy(pos_hbm.at[pl.ds(start, n)], pos_smem, pos_sem)


def _dispatch_kernel(pos_hbm, x_ref, g_ref, xs_in, xs_out, h_ref, pos_smem, sems, pos_sem):
    del xs_in
    i = pl.program_id(0)
    tm = h_ref.shape[1]
    buf = i % 2
    pos_copy = _load_positions(pos_hbm, pos_smem, pos_sem, i)
    pos_copy.start()
    h_ref[buf] = _rms(x_ref[...], g_ref[...])
    pos_copy.wait()

    def issue(r, carry):
        for k in range(2):
            pltpu.make_async_copy(h_ref.at[buf, pl.ds(r, 1)], xs_out.at[pl.ds(pos_smem[k * tm + r], 1)],
                                  sems.at[buf]).start()
        return carry

    lax.fori_loop(0, tm, issue, 0, unroll=8)

    def drain(b):
        for _ in range(2):
            pltpu.make_async_copy(h_ref.at[b], xs_out.at[pl.ds(0, tm)], sems.at[b]).wait()

    @pl.when(i > 0)
    def _():
        drain(1 - buf)

    @pl.when(i == pl.num_programs(0) - 1)
    def _():
        drain(buf)


def _dispatch(pos, x, g, n_slots):
    n_tok = x.shape[0]
    tm = ROW_TILE
    xs0 = jnp.zeros((n_slots, D_MODEL), F32)
    return pl.pallas_call(
        _dispatch_kernel,
        grid=(n_tok // tm,),
        in_specs=[
            pl.BlockSpec(memory_space=pl.ANY),
            pl.BlockSpec((tm, D_MODEL), lambda i: (i, 0)),
            _resident((1, D_MODEL), lambda i: (0, 0)),
            pl.BlockSpec(memory_space=pl.ANY),
        ],
        out_specs=pl.BlockSpec(memory_space=pl.ANY),
        out_shape=jax.ShapeDtypeStruct((n_slots, D_MODEL), F32),
        scratch_shapes=[
            pltpu.VMEM((2, tm, D_MODEL), F32),
            pltpu.SMEM((2 * tm,), jnp.int32),
            pltpu.SemaphoreType.DMA((2,)),
            pltpu.SemaphoreType.DMA(()),
        ],
        input_output_aliases={3: 0},
        compiler_params=_params("arbitrary"),
        name="moe_dispatch",
    )(pos, x, g, xs0)


def _expert_kernel(te_ref, used_ref, xs_ref, wg_ref, wu_ref, wo_ref, y_ref, hb_ref):
    del te_ref
    c = pl.program_id(1)
    in_use = pl.program_id(0) < used_ref[0]

    @pl.when(jnp.logical_not(in_use) & (c == 0))
    def _():
        y_ref[...] = jnp.zeros_like(y_ref)

    @pl.when(in_use)
    def _():
        @pl.when(c == 0)
        def _():
            hb_ref[...] = xs_ref[...].astype(BF16)
            y_ref[...] = jnp.zeros_like(y_ref)

        h = hb_ref[...]
        gate = jnp.dot(h, wg_ref[0], preferred_element_type=F32)
        up = jnp.dot(h, wu_ref[0], preferred_element_type=F32)
        a = (gate * jax.nn.sigmoid(gate) * up).astype(BF16)
        y_ref[...] += jnp.dot(a, wo_ref[0], preferred_element_type=F32)


def _experts(tile_expert, used, xs, w_in, w_out):
    tm = MOE_ROW_TILE
    ck = MOE_FF_CHUNK
    n_ck = D_FF_EXPERT // ck
    n_slots = xs.shape[0]

    def chunk(i, c, te, used):
        return jnp.where(i < used[0], c, n_ck - 1)

    grid_spec = pltpu.PrefetchScalarGridSpec(
        num_scalar_prefetch=2,
        grid=(n_slots // tm, n_ck),
        in_specs=[
            pl.BlockSpec((tm, D_MODEL), lambda i, c, te, used: (i, 0)),
            pl.BlockSpec((1, D_MODEL, ck), lambda i, c, te, used: (te[i], 0, chunk(i, c, te, used))),
            pl.BlockSpec((1, D_MODEL, ck), lambda i, c, te, used: (te[i], 0, n_ck + chunk(i, c, te, used))),
            pl.BlockSpec((1, ck, D_MODEL), lambda i, c, te, used: (te[i], chunk(i, c, te, used), 0)),
        ],
        out_specs=pl.BlockSpec((tm, D_MODEL), lambda i, c, te, used: (i, 0)),
        scratch_shapes=[pltpu.VMEM((tm, D_MODEL), BF16)],
    )
    return pl.pallas_call(
        _expert_kernel,
        grid_spec=grid_spec,
        out_shape=jax.ShapeDtypeStruct((n_slots, D_MODEL), F32),
        compiler_params=_params("arbitrary", "arbitrary"),
        name="moe_experts",
    )(tile_expert, used, xs, w_in, w_in, w_out)


def _combine_kernel(pos_hbm, y_hbm, x_ref, gate_ref, gf_ref, *rest, split):
    out_refs = rest[:1 if split is None else 2]
    rows_ref, pos_smem, sems, pos_sem = rest[len(out_refs):]
    i = pl.program_id(0)
    n_tiles = pl.num_programs(0)
    tm = x_ref.shape[0]

    def gather(tile, buf):
        pos_copy = _load_positions(pos_hbm, pos_smem, pos_sem, tile)
        pos_copy.start()
        pos_copy.wait()

        def issue(r, carry):
            for k in range(2):
                pltpu.make_async_copy(y_hbm.at[pl.ds(pos_smem[k * tm + r], 1)],
                                      rows_ref.at[buf, k, pl.ds(r, 1)], sems.at[buf]).start()
            return carry

        lax.fori_loop(0, tm, issue, 0, unroll=8)

    @pl.when(i == 0)
    def _():
        gather(0, 0)

    @pl.when(i + 1 < n_tiles)
    def _():
        gather(i + 1, (i + 1) % 2)

    buf = i % 2
    for k in range(2):
        pltpu.make_async_copy(y_hbm.at[pl.ds(0, tm)], rows_ref.at[buf, k], sems.at[buf]).wait()
    gates = gate_ref[...]
    out = x_ref[...] + (gates[:, 0:1] * rows_ref[buf, 0] + gates[:, 1:2] * rows_ref[buf, 1])
    if split is None:
        out_refs[0][...] = out
    else:
        out = _rms(out, gf_ref[...])

        @pl.when(i < split)
        def _():
            out_refs[0][...] = out

        @pl.when(i >= split)
        def _():
            out_refs[1][...] = out


def _combine(pos, y, x, gates, g_final, *, split):
    n_tok = x.shape[0]
    tm = ROW_TILE
    n_tiles = n_tok // tm
    if split is None:
        out_specs = [pl.BlockSpec((tm, D_MODEL), lambda i: (i, 0))]
        out_shape = [jax.ShapeDtypeStruct((n_tok, D_MODEL), F32)]
    else:
        out_specs = [pl.BlockSpec((tm, D_MODEL), lambda i: (jnp.minimum(i, split - 1), 0)),
                     pl.BlockSpec((tm, D_MODEL), lambda i: (jnp.maximum(i - split, 0), 0))]
        out_shape = [jax.ShapeDtypeStruct((split * tm, D_MODEL), F32),
                     jax.ShapeDtypeStruct(((n_tiles - split) * tm, D_MODEL), F32)]
    return pl.pallas_call(
        functools.partial(_combine_kernel, split=split),
        grid=(n_tiles,),
        in_specs=[
            pl.BlockSpec(memory_space=pl.ANY),
            pl.BlockSpec(memory_space=pl.ANY),
            pl.BlockSpec((tm, D_MODEL), lambda i: (i, 0)),
            pl.BlockSpec((tm, LANES), lambda i: (i, 0)),
            _resident((1, D_MODEL), lambda i: (0, 0)),
        ],
        out_specs=out_specs,
        out_shape=out_shape,
        scratch_shapes=[
            pltpu.VMEM((2, 2, tm, D_MODEL), F32),
            pltpu.SMEM((2 * tm,), jnp.int32),
            pltpu.SemaphoreType.DMA((2,)),
            pltpu.SemaphoreType.DMA(()),
        ],
        compiler_params=_params("arbitrary"),
        name="moe_combine",
    )(pos, y, x, gates, g_final)


def _moe(o, w_o, x, g, w_router, w_in, w_out, g_final, *, split):
    wr = jnp.pad(w_router, ((0, 0), (0, LANES - N_EXPERTS)))
    x, meta, gates, counts = _router(o, w_o, x, g, wr)
    pos, tile_expert, used, n_slots = _moe_plan(meta, counts)
    xs = _dispatch(pos, x, g, n_slots)
    y = _experts(tile_expert, used, xs, w_in, w_out)
    return _combine(pos, y, x, gates, g_final, split=split)


def _rope_tables():
    t = jnp.arange(SEQ)
    row = (t // GRID_W).astype(F32)
    col = (t % GRID_W).astype(F32)
    half = HEAD_DIM // 2
    inv_freq = ROPE_THETA ** (-jnp.arange(0, half, 2, dtype=F32) / half)
    ang = jnp.concatenate([row[:, None] * inv_freq[None, :], col[:, None] * inv_freq[None, :]], axis=-1)
    cos = jnp.repeat(jnp.cos(ang), 2, axis=-1)
    sign = jnp.where(jnp.arange(HEAD_DIM) % 2 == 0, -1.0, 1.0).astype(F32)
    sin = jnp.repeat(jnp.sin(ang), 2, axis=-1) * sign
    reps = LANES // HEAD_DIM
    return jnp.tile(cos, (1, reps)), jnp.tile(sin, (1, reps))


def _t5_bucket(rel):
    half = N_BUCKETS // 2
    ret = jnp.where(rel > 0, half, 0)
    n = jnp.abs(rel)
    max_exact = half // 2
    nf = jnp.maximum(n, 1).astype(F32)
    large = max_exact + (jnp.log(nf / max_exact) / math.log(MAX_DISTANCE / max_exact)
                         * (half - max_exact)).astype(jnp.int32)
    large = jnp.minimum(large, half - 1)
    return ret + jnp.where(n < max_exact, n, large)


def _window_bias(rel_bias):
    r = jnp.arange(Q_BLOCK)[:, None]
    c = jnp.arange(Q_BLOCK + 2 * WINDOW)[None, :]
    rel = c - WINDOW - r
    bias = rel_bias[_t5_bucket(rel)].astype(F32) * LOG2E
    bias = jnp.where((jnp.abs(rel) <= WINDOW)[:, :, None], bias, NEG_INF)
    bias = bias.reshape(Q_BLOCK, Q_BLOCK + 2 * WINDOW, N_KV_HEADS, GROUP).transpose(2, 1, 3, 0)
    bias = bias.reshape(N_KV_HEADS, Q_BLOCK + 2 * WINDOW, GROUP * Q_BLOCK)
    key = jnp.arange(Q_BLOCK + 2 * WINDOW)[None, :, None]
    no_left = jnp.where(key < WINDOW, NEG_INF, bias)
    no_right = jnp.where(key >= Q_BLOCK + WINDOW, NEG_INF, bias)
    return jnp.stack([bias, no_left, no_right])


def _tile_gain(gain):
    return jnp.tile(gain.astype(F32), LANES // HEAD_DIM)[None, :]


def kernel(x_prompt, x_sample, norm_mix, norm_ffn, norm_final, w_qkv_a, q_gain_a, k_gain_a, w_o_a,
           w_qkv_b, sink_b, w_o_b, rel_bias, w_ff_in, w_ff_out, w_router, w_exp_in, w_exp_out):
    n_prompt = x_prompt.shape[0] * x_prompt.shape[1]
    x = jnp.concatenate([x_prompt.reshape(-1, D_MODEL), x_sample.reshape(-1, D_MODEL)], axis=0)
    cos, sin = _rope_tables()
    bias = _window_bias(rel_bias)
    ones_gain = jnp.ones((1, LANES), F32)
    for i in range(DEPTH):
        j = i // 2
        g_mix = norm_mix[i][None, :]
        g_ffn = norm_ffn[i][None, :]
        if i % 2 == 0:
            q, k, vt = _qkv_proj(x, g_mix, w_qkv_a[j].astype(BF16), cos, sin,
                                 _tile_gain(q_gain_a[j]), _tile_gain(k_gain_a[j]), axial=True)
            o = _global_attention(q, k, vt)
            x = _dense_ffn(o, w_o_a[j].astype(BF16), x, g_ffn, w_ff_in[j].astype(BF16), w_ff_out[j].astype(BF16))
        else:
            q, k, vt = _qkv_proj(x, g_mix, w_qkv_b[j].astype(BF16), cos, sin, ones_gain, ones_gain, axial=False)
            o = _window_attention(q, k, vt, sink_b[j].astype(F32) * LOG2E, bias)
            last = i == DEPTH - 1
            x = _moe(o, w_o_b[j].astype(BF16), x, g_ffn, w_router[j], w_exp_in[j].astype(BF16),
                     w_exp_out[j].astype(BF16), norm_final[None, :],
                     split=n_prompt // ROW_TILE if last else None)
            if not last:
                x = x[0]
    y_prompt, y_sample = x
    return (y_prompt.reshape(x_prompt.shape), y_sample.reshape(x_sample.shape))
```

```python
import functools
import math

import jax
import jax.numpy as jnp
from jax import lax
from jax.experimental import pallas as pl
from jax.experimental.pallas import tpu as pltpu

D_MODEL = 1024
SEQ = 4096
DEPTH = 4
HEAD_DIM = 64
N_Q_HEADS = 16
N_KV_HEADS = 4
GROUP = N_Q_HEADS // N_KV_HEADS
Q_WIDTH = N_Q_HEADS * HEAD_DIM
KV_WIDTH = N_KV_HEADS * HEAD_DIM
QKV_WIDTH = Q_WIDTH + 2 * KV_WIDTH
GRID_W = 64
ROPE_THETA = 10000.0
Q_BLOCK = 128
WINDOW = 128
N_BUCKETS = 32
MAX_DISTANCE = 128
D_FF = 2816
N_EXPERTS = 8
D_FF_EXPERT = 3584
EPS = 1e-6
NEG_INF = -1e30

LANES = 128
VMEM_LIMIT = 56 * 1024 * 1024

ROW_TILE = 512
WINDOW_Q_TILE = 512
GLOBAL_Q_TILE = 256
MOE_ROW_TILE = 1024
MOE_FF_CHUNK = 512
META_ROWS = 8

K_PAD_WIDTH = N_KV_HEADS * LANES
V_ONES = 16
LOG2E = math.log2(math.e)
Q_SCALE = HEAD_DIM ** -0.5 * LOG2E

F32 = jnp.float32
BF16 = jnp.bfloat16


def _params(*sem):
    return pltpu.CompilerParams(dimension_semantics=sem, vmem_limit_bytes=VMEM_LIMIT)


def _resident(shape, index_map):
    return pl.BlockSpec(shape, index_map, pipeline_mode=pl.Buffered(1))


def _rms(x, g):
    ms = jnp.mean(x * x, axis=-1, keepdims=True)
    return x * lax.rsqrt(ms + EPS) * g


def _qkv_kernel(x_ref, g_ref, w_ref, cos_ref, sin_ref, qg_ref, kg_ref, q_ref, k_ref, vt_ref, *, axial):
    h = _rms(x_ref[...], g_ref[...]).astype(BF16)
    acc = jnp.dot(h, w_ref[...], preferred_element_type=F32)
    scale = Q_SCALE
    n_q_tiles = Q_WIDTH // LANES
    n_k_tiles = KV_WIDTH // LANES
    if axial:
        lane = lax.broadcasted_iota(jnp.int32, (acc.shape[0], LANES), 1)
        low = lane < HEAD_DIM
        even = (lane & 1) == 0
        cos = cos_ref[...]
        sin = sin_ref[...]

        def norm_rope(seg, gain):
            sq = seg * seg
            s_lo = jnp.sum(jnp.where(low, sq, 0.0), axis=-1, keepdims=True)
            s_hi = jnp.sum(jnp.where(low, 0.0, sq), axis=-1, keepdims=True)
            ms = jnp.where(low, s_lo, s_hi) * (1.0 / HEAD_DIM)
            y = seg * lax.rsqrt(ms + EPS) * gain
            swapped = jnp.where(even, pltpu.roll(y, LANES - 1, 1), pltpu.roll(y, 1, 1))
            return y * cos + swapped * sin

        qg = qg_ref[...]
        kg = kg_ref[...]
        for j in range(n_q_tiles):
            seg = norm_rope(acc[:, j * LANES:(j + 1) * LANES], qg)
            q_ref[:, j * LANES:(j + 1) * LANES] = (seg * scale).astype(BF16)
        k_parts = [norm_rope(acc[:, Q_WIDTH + j * LANES:Q_WIDTH + (j + 1) * LANES], kg)
                   for j in range(n_k_tiles)]
        k = jnp.concatenate(k_parts, axis=1)
    else:
        q_ref[...] = (acc[:, :Q_WIDTH] * scale).astype(BF16)
        k = acc[:, Q_WIDTH:Q_WIDTH + KV_WIDTH]
    pad = jnp.zeros((k.shape[0], LANES - HEAD_DIM), BF16)
    for hd in range(N_KV_HEADS):
        k_ref[:, hd * LANES:hd * LANES + HEAD_DIM] = k[:, hd * HEAD_DIM:(hd + 1) * HEAD_DIM].astype(BF16)
        k_ref[:, hd * LANES + HEAD_DIM:(hd + 1) * LANES] = pad
    vt_ref[0, 0] = acc[:, Q_WIDTH + KV_WIDTH:].T.astype(BF16)


def _qkv_proj(x, g, w, cos, sin, qg, kg, *, axial):
    n_tok = x.shape[0]
    tm = ROW_TILE
    tiles_per_seq = SEQ // tm
    n_seq = n_tok // SEQ
    return pl.pallas_call(
        functools.partial(_qkv_kernel, axial=axial),
        grid=(n_tok // tm,),
        in_specs=[
            pl.BlockSpec((tm, D_MODEL), lambda i: (i, 0)),
            _resident((1, D_MODEL), lambda i: (0, 0)),
            _resident((D_MODEL, QKV_WIDTH), lambda i: (0, 0)),
            pl.BlockSpec((tm, LANES), lambda i: (i % tiles_per_seq, 0)),
            pl.BlockSpec((tm, LANES), lambda i: (i % tiles_per_seq, 0)),
            _resident((1, LANES), lambda i: (0, 0)),
            _resident((1, LANES), lambda i: (0, 0)),
        ],
        out_specs=[
            pl.BlockSpec((tm, Q_WIDTH), lambda i: (i, 0)),
            pl.BlockSpec((tm, K_PAD_WIDTH), lambda i: (i, 0)),
            pl.BlockSpec((1, 1, KV_WIDTH, tm), lambda i: (i // tiles_per_seq, i % tiles_per_seq, 0, 0)),
        ],
        out_shape=[
            jax.ShapeDtypeStruct((n_tok, Q_WIDTH), BF16),
            jax.ShapeDtypeStruct((n_tok, K_PAD_WIDTH), BF16),
            jax.ShapeDtypeStruct((n_seq, tiles_per_seq, KV_WIDTH, tm), BF16),
        ],
        compiler_params=_params("parallel"),
        name="qkv_axial" if axial else "qkv_window",
    )(x, g, w, cos, sin, qg, kg)


def _q_transposed(q_ref):
    return q_ref[...].astype(F32).T.astype(BF16)


def _group_queries(qt, h):
    base = h * GROUP * HEAD_DIM
    return jnp.concatenate(
        [qt[base + g * HEAD_DIM:base + (g + 1) * HEAD_DIM, :] for g in range(GROUP)], axis=1)


def _v_with_ones(vt_h, ones):
    return jnp.concatenate([vt_h, ones], axis=0)


def _store_heads(o_ref, outs, tq):
    rows = [o[:, g * tq:(g + 1) * tq] for o in outs for g in range(GROUP)]
    o_ref[...] = jnp.concatenate(rows, axis=0).T.astype(o_ref.dtype)


def _global_attn_kernel(q_ref, k_ref, vt_ref, o_ref, s_ref):
    tq = q_ref.shape[0]
    n_kv = vt_ref.shape[1]
    tk = vt_ref.shape[3]
    cols = GROUP * tq
    qt = _q_transposed(q_ref)
    queries = [_group_queries(qt, h) for h in range(N_KV_HEADS)]
    ones = jnp.ones((V_ONES, tk), BF16)

    def scores(h, j):
        start = pl.multiple_of(j * tk, tk)
        k_h = k_ref[pl.ds(start, tk), h * LANES:h * LANES + HEAD_DIM]
        return jnp.dot(k_h, queries[h], preferred_element_type=F32)

    def update(h, j, s, m, acc):
        m_new = jnp.maximum(m, jnp.max(s, axis=0, keepdims=True))
        p = jnp.exp2(s - m_new).astype(BF16)
        alpha = jnp.exp2(m - m_new)
        v_h = _v_with_ones(vt_ref[0, j, h * HEAD_DIM:(h + 1) * HEAD_DIM, :], ones)
        return m_new, alpha * acc + jnp.dot(v_h, p, preferred_element_type=F32)

    s_ref[...] = scores(0, 0)

    def body(j, carry):
        new = []
        s = s_ref[...]
        for h in range(N_KV_HEADS):
            if h + 1 < N_KV_HEADS:
                s_next = scores(h + 1, j)
            else:
                s_next = scores(0, jnp.minimum(j + 1, n_kv - 1))
            new.append(update(h, j, s, *carry[h]))
            s = s_next
        s_ref[...] = s
        return tuple(new)

    init = tuple((jnp.full((1, cols), -jnp.inf, F32), jnp.zeros((HEAD_DIM + V_ONES, cols), F32))
                 for _ in range(N_KV_HEADS))
    final = lax.fori_loop(0, n_kv, body, init, unroll=2)
    outs = [acc[:HEAD_DIM] / acc[HEAD_DIM:HEAD_DIM + 1] for _, acc in final]
    _store_heads(o_ref, outs, tq)


def _global_attention(q, k, vt):
    n_tok = q.shape[0]
    n_seq = n_tok // SEQ
    tq = GLOBAL_Q_TILE
    nq = SEQ // tq
    return pl.pallas_call(
        _global_attn_kernel,
        grid=(n_seq, nq),
        in_specs=[
            pl.BlockSpec((tq, Q_WIDTH), lambda b, i: (b * nq + i, 0)),
            pl.BlockSpec((SEQ, K_PAD_WIDTH), lambda b, i: (b, 0)),
            pl.BlockSpec((1,) + vt.shape[1:], lambda b, i: (b, 0, 0, 0)),
        ],
        out_specs=pl.BlockSpec((tq, Q_WIDTH), lambda b, i: (b * nq + i, 0)),
        out_shape=jax.ShapeDtypeStruct((n_tok, Q_WIDTH), BF16),
        scratch_shapes=[pltpu.VMEM((vt.shape[3], GROUP * tq), F32)],
        compiler_params=_params("parallel", "arbitrary"),
        name="global_attention",
    )(q, k, vt)


def _window_attn_kernel(sink_ref, q_ref, kl_ref, kc_ref, kr_ref, vtl_ref, vtc_ref, vtr_ref, bias_ref, o_ref):
    i = pl.program_id(1)
    qb = Q_BLOCK
    n_sub = q_ref.shape[0] // qb
    n_keys = 3 * qb
    k_all = jnp.concatenate([kl_ref[...], kc_ref[...], kr_ref[...]], axis=0)
    vt_all = jnp.concatenate([vtl_ref[0, 0], vtc_ref[0, 0], vtr_ref[0, 0]], axis=1)
    qt = _q_transposed(q_ref)
    ones = jnp.ones((V_ONES, n_keys), BF16)
    variants = [jnp.where(i == 0, 1, 0)] + [0] * (n_sub - 2) + [jnp.where(i == pl.num_programs(1) - 1, 2, 0)]
    units = [(u, h) for u in range(n_sub) for h in range(N_KV_HEADS)]

    def scores(u, h):
        keys = k_all[u * qb:u * qb + n_keys, h * LANES:h * LANES + HEAD_DIM]
        return jnp.dot(keys, _group_queries(qt[:, u * qb:(u + 1) * qb], h), preferred_element_type=F32)

    outs = []
    s_next = scores(*units[0])
    for n, (u, h) in enumerate(units):
        s = s_next
        if n + 1 < len(units):
            s_next = scores(*units[n + 1])
        s = s + bias_ref[variants[u], h]
        sink = jnp.concatenate(
            [jnp.full((1, qb), sink_ref[h * GROUP + g], F32) for g in range(GROUP)], axis=1)
        m = jnp.maximum(jnp.max(s, axis=0, keepdims=True), sink)
        p = jnp.exp2(s - m).astype(BF16)
        v_h = _v_with_ones(vt_all[h * HEAD_DIM:(h + 1) * HEAD_DIM, u * qb:u * qb + n_keys], ones)
        acc = jnp.dot(v_h, p, preferred_element_type=F32)
        outs.append(acc[:HEAD_DIM] / (acc[HEAD_DIM:HEAD_DIM + 1] + jnp.exp2(sink - m)))
    blocks = []
    for u in range(n_sub):
        heads = outs[u * N_KV_HEADS:(u + 1) * N_KV_HEADS]
        blocks.append(jnp.concatenate([o[:, g * qb:(g + 1) * qb] for o in heads for g in range(GROUP)], axis=0))
    o_ref[...] = jnp.concatenate(blocks, axis=1).T.astype(o_ref.dtype)


def _window_attention(q, k, vt, sinks, bias):
    n_tok = q.shape[0]
    n_seq = n_tok // SEQ
    qb = Q_BLOCK
    tq = WINDOW_Q_TILE
    nq = SEQ // tq
    nb = SEQ // qb
    sub = tq // qb
    assert sub >= 2, "the first and last query block of a step take different bias variants"

    def edge_block(b, i, off):
        return jnp.clip(i * sub + off, 0, nb - 1)

    def k_edge(off):
        return pl.BlockSpec((qb, K_PAD_WIDTH), lambda b, i, s: (b * nb + edge_block(b, i, off), 0))

    def vt_edge(off):
        per = ROW_TILE // qb
        return pl.BlockSpec((1, 1, KV_WIDTH, qb),
                            lambda b, i, s: (b, edge_block(b, i, off) // per, 0, edge_block(b, i, off) % per))

    per_tile = ROW_TILE // tq
    grid_spec = pltpu.PrefetchScalarGridSpec(
        num_scalar_prefetch=1,
        grid=(n_seq, nq),
        in_specs=[
            pl.BlockSpec((tq, Q_WIDTH), lambda b, i, s: (b * nq + i, 0)),
            k_edge(-1),
            pl.BlockSpec((tq, K_PAD_WIDTH), lambda b, i, s: (b * nq + i, 0)),
            k_edge(sub),
            vt_edge(-1),
            pl.BlockSpec((1, 1, KV_WIDTH, tq), lambda b, i, s: (b, i // per_tile, 0, i % per_tile)),
            vt_edge(sub),
            _resident(bias.shape, lambda b, i, s: (0, 0, 0, 0)),
        ],
        out_specs=pl.BlockSpec((tq, Q_WIDTH), lambda b, i, s: (b * nq + i, 0)),
    )
    return pl.pallas_call(
        _window_attn_kernel,
        grid_spec=grid_spec,
        out_shape=jax.ShapeDtypeStruct((n_tok, Q_WIDTH), BF16),
        compiler_params=_params("parallel", "arbitrary"),
        name="window_attention",
    )(sinks, q, k, k, k, vt, vt, vt, bias)


def _attn_residual(o_ref, wo_ref, x_ref):
    return x_ref[...] + jnp.dot(o_ref[...], wo_ref[...], preferred_element_type=F32)


def _attn_residual_specs(tm):
    return [
        pl.BlockSpec((tm, Q_WIDTH), lambda i: (i, 0)),
        _resident((Q_WIDTH, D_MODEL), lambda i: (0, 0)),
        pl.BlockSpec((tm, D_MODEL), lambda i: (i, 0)),
    ]


def _ff_chunks(d_ff, width):
    return [(c, min(c + width, d_ff)) for c in range(0, d_ff, width)]


def _dense_ffn_kernel(o_ref, wo_ref, x_ref, g_ref, w_in_ref, w_out_ref, y_ref):
    x = _attn_residual(o_ref, wo_ref, x_ref)
    h = _rms(x, g_ref[...]).astype(BF16)
    acc = x
    for c0, c1 in _ff_chunks(D_FF, 1024):
        gate = jnp.dot(h, w_in_ref[:, c0:c1], preferred_element_type=F32)
        up = jnp.dot(h, w_in_ref[:, D_FF + c0:D_FF + c1], preferred_element_type=F32)
        a = (gate * jax.nn.sigmoid(gate) * up).astype(BF16)
        acc = acc + jnp.dot(a, w_out_ref[c0:c1, :], preferred_element_type=F32)
    y_ref[...] = acc


def _dense_ffn(o, w_o, x, g, w_in, w_out):
    n_tok = x.shape[0]
    tm = ROW_TILE
    return pl.pallas_call(
        _dense_ffn_kernel,
        grid=(n_tok // tm,),
        in_specs=_attn_residual_specs(tm) + [
            _resident((1, D_MODEL), lambda i: (0, 0)),
            _resident((D_MODEL, 2 * D_FF), lambda i: (0, 0)),
            _resident((D_FF, D_MODEL), lambda i: (0, 0)),
        ],
        out_specs=pl.BlockSpec((tm, D_MODEL), lambda i: (i, 0)),
        out_shape=jax.ShapeDtypeStruct((n_tok, D_MODEL), F32),
        compiler_params=_params("parallel"),
        name="dense_ffn",
    )(o, w_o, x, g, w_in, w_out)


def _router_kernel(o_ref, wo_ref, x_ref, g_ref, wr_ref, earlier_ref, xn_ref, meta_ref, gate_ref, count_ref,
                   base_ref):
    @pl.when(pl.program_id(0) == 0)
    def _():
        base_ref[...] = jnp.zeros_like(base_ref)

    x = _attn_residual(o_ref, wo_ref, x_ref)
    xn_ref[...] = x
    hf = _rms(x, g_ref[...])
    h_hi = hf.astype(BF16)
    h_lo = (hf - h_hi.astype(F32)).astype(BF16)
    by_hi = jnp.dot(h_hi, wr_ref[...], preferred_element_type=F32)
    by_lo = jnp.dot(h_lo, wr_ref[:, :LANES], preferred_element_type=F32)
    logits = by_hi[:, :LANES] + (by_hi[:, LANES:] + by_lo)
    lane = lax.broadcasted_iota(jnp.int32, logits.shape, 1)
    logits = jnp.where(lane < N_EXPERTS, logits, -jnp.inf)
    m1 = jnp.max(logits, axis=-1, keepdims=True)
    i1 = jnp.min(jnp.where(logits == m1, lane, LANES), axis=-1, keepdims=True)
    rest = jnp.where(lane == i1, -jnp.inf, logits)
    m2 = jnp.max(rest, axis=-1, keepdims=True)
    i2 = jnp.min(jnp.where(rest == m2, lane, LANES), axis=-1, keepdims=True)
    e2 = jnp.exp(m2 - m1)
    g1 = 1.0 / (1.0 + e2)
    g2 = e2 / (1.0 + e2)

    chosen = jnp.where((lane == i1) | (lane == i2), 1.0, 0.0)
    before = jnp.dot(earlier_ref[...], chosen.astype(BF16), preferred_element_type=F32) + base_ref[...]
    r1 = jnp.sum(jnp.where(lane == i1, before, 0.0), axis=-1, keepdims=True)
    r2 = jnp.sum(jnp.where(lane == i2, before, 0.0), axis=-1, keepdims=True)
    meta = jnp.where(lane == 0, i1.astype(F32), jnp.where(lane == 1, i2.astype(F32),
                                                          jnp.where(lane == 2, r1, jnp.where(lane == 3, r2, 0.0))))
    meta_ref[...] = meta.T[:META_ROWS].astype(jnp.int32)
    gate_ref[...] = jnp.where(lane == 0, g1, jnp.where(lane == 1, g2, 0.0))
    base_ref[...] += jnp.sum(chosen, axis=0, keepdims=True)
    count_ref[...] = base_ref[...]


def _router(o, w_o, x, g, w_router):
    n_tok = x.shape[0]
    tm = ROW_TILE
    w_hi = w_router.astype(BF16)
    w_lo = (w_router - w_hi.astype(F32)).astype(BF16)
    pad = ((0, 0), (0, LANES - N_EXPERTS))
    wr = jnp.concatenate([jnp.pad(w_hi, pad), jnp.pad(w_lo, pad)], axis=1)
    earlier = jnp.tril(jnp.ones((tm, tm), BF16), -1)
    return pl.pallas_call(
        _router_kernel,
        grid=(n_tok // tm,),
        in_specs=_attn_residual_specs(tm) + [
            _resident((1, D_MODEL), lambda i: (0, 0)),
            _resident((D_MODEL, 2 * LANES), lambda i: (0, 0)),
            _resident((tm, tm), lambda i: (0, 0)),
        ],
        out_specs=[
            pl.BlockSpec((tm, D_MODEL), lambda i: (i, 0)),
            pl.BlockSpec((META_ROWS, tm), lambda i: (0, i)),
            pl.BlockSpec((tm, LANES), lambda i: (i, 0)),
            pl.BlockSpec((1, LANES), lambda i: (0, 0)),
        ],
        out_shape=[
            jax.ShapeDtypeStruct((n_tok, D_MODEL), F32),
            jax.ShapeDtypeStruct((META_ROWS, n_tok), jnp.int32),
            jax.ShapeDtypeStruct((n_tok, LANES), F32),
            jax.ShapeDtypeStruct((1, LANES), F32),
        ],
        scratch_shapes=[pltpu.VMEM((1, LANES), F32)],
        compiler_params=_params("arbitrary"),
        name="router",
    )(o, w_o, x, g, wr, earlier)


def _moe_plan(meta, counts):
    tm = MOE_ROW_TILE
    n_tok = meta.shape[1]
    n_tiles = (2 * n_tok) // tm + N_EXPERTS
    cnt = counts[0, :N_EXPERTS].astype(jnp.int32)
    tiles = (cnt + tm - 1) // tm
    tile_end = jnp.cumsum(tiles)
    offsets = (tile_end - tiles) * tm
    tile_ids = jnp.arange(n_tiles, dtype=jnp.int32)
    tile_expert = jnp.minimum(jnp.sum(tile_ids[:, None] >= tile_end[None, :], axis=1), N_EXPERTS - 1)
    used = tile_end[-1:].astype(jnp.int32)
    experts = jnp.arange(N_EXPERTS, dtype=jnp.int32)

    def slot(expert, rank):
        first = jnp.sum(jnp.where(expert[:, None] == experts[None, :], offsets[None, :], 0), axis=1)
        return first + rank

    pos1 = slot(meta[0], meta[2])
    pos2 = slot(meta[1], meta[3])
    pos = jnp.stack([pos1.reshape(-1, ROW_TILE), pos2.reshape(-1, ROW_TILE)], axis=1).reshape(-1)
    pad_start = (offsets + cnt).astype(jnp.int32)
    pad_count = (tiles * tm - cnt).astype(jnp.int32)
    return (pos.astype(jnp.int32), tile_expert.astype(jnp.int32), used, pad_start, pad_count, n_tiles * tm)


def _load_positions(pos_hbm, pos_smem, pos_sem, tile):
    n = pos_smem.shape[0]
    start = pl.multiple_of(tile * n, n)
    return pltpu.make_async_copy(pos_hbm.at[pl.ds(start, n)], pos_smem, pos_sem)


def _dispatch_kernel(pad_start_ref, pad_count_ref, used_ref, pos_hbm, x_ref, g_ref, xs_out, h_ref, zero_ref,
                     pos_smem, sems, pos_sem, pad_sem):
    i = pl.program_id(0)
    tm = h_ref.shape[1]
    buf = i % 2
    pos_copy = _load_positions(pos_hbm, pos_smem, pos_sem, i)
    pos_copy.start()
    h_ref[buf] = _rms(x_ref[...], g_ref[...])
    pos_copy.wait()

    @pl.when(i == 0)
    def _():
        zero_ref[...] = jnp.zeros_like(zero_ref)

        def zero_copy(slot):
            return pltpu.make_async_copy(zero_ref.at[pl.ds(0, 1)], xs_out.at[pl.ds(slot, 1)], pad_sem)

        for e in range(N_EXPERTS):
            def fill(r, carry):
                zero_copy(pad_start_ref[e] + r).start()
                return carry

            lax.fori_loop(0, pad_count_ref[e], fill, 0)

        tail_start = used_ref[0] * MOE_ROW_TILE
        n_tail = (xs_out.shape[0] - tail_start) // tm

        def tail_copy(t):
            start = pl.multiple_of(tail_start + t * tm, tm)
            return pltpu.make_async_copy(zero_ref, xs_out.at[pl.ds(start, tm)], pad_sem)

        def fill_tail(t, carry):
            tail_copy(t).start()
            return carry

        lax.fori_loop(0, n_tail, fill_tail, 0)
        for e in range(N_EXPERTS):
            def fill_done(r, carry):
                zero_copy(pad_start_ref[e] + r).wait()
                return carry

            lax.fori_loop(0, pad_count_ref[e], fill_done, 0)

        def tail_done(t, carry):
            tail_copy(t).wait()
            return carry

        lax.fori_loop(0, n_tail, tail_done, 0)

    def issue(r, carry):
        for k in range(2):
            pltpu.make_async_copy(h_ref.at[buf, pl.ds(r, 1)], xs_out.at[pl.ds(pos_smem[k * tm + r], 1)],
                                  sems.at[buf]).start()
        return carry

    lax.fori_loop(0, tm, issue, 0, unroll=8)

    def drain(b):
        for _ in range(2):
            pltpu.make_async_copy(h_ref.at[b], xs_out.at[pl.ds(0, tm)], sems.at[b]).wait()

    @pl.when(i > 0)
    def _():
        drain(1 - buf)

    @pl.when(i == pl.num_programs(0) - 1)
    def _():
        drain(buf)


def _dispatch(pad_start, pad_count, used, pos, x, g, n_slots):
    n_tok = x.shape[0]
    tm = ROW_TILE
    grid_spec = pltpu.PrefetchScalarGridSpec(
        num_scalar_prefetch=3,
        grid=(n_tok // tm,),
        in_specs=[
            pl.BlockSpec(memory_space=pl.ANY),
            pl.BlockSpec((tm, D_MODEL), lambda i, ps, pc, u: (i, 0)),
            _resident((1, D_MODEL), lambda i, ps, pc, u: (0, 0)),
        ],
        out_specs=pl.BlockSpec(memory_space=pl.ANY),
        scratch_shapes=[
            pltpu.VMEM((2, tm, D_MODEL), F32),
            pltpu.VMEM((tm, D_MODEL), F32),
            pltpu.SMEM((2 * tm,), jnp.int32),
            pltpu.SemaphoreType.DMA((2,)),
            pltpu.SemaphoreType.DMA(()),
            pltpu.SemaphoreType.DMA(()),
        ],
    )
    return pl.pallas_call(
        _dispatch_kernel,
        grid_spec=grid_spec,
        out_shape=jax.ShapeDtypeStruct((n_slots, D_MODEL), F32),
        compiler_params=_params("arbitrary"),
        name="moe_dispatch",
    )(pad_start, pad_count, used, pos, x, g)


def _expert_kernel(te_ref, used_ref, xs_ref, wg_ref, wu_ref, wo_ref, y_ref, hb_ref):
    del te_ref
    c = pl.program_id(1)
    in_use = pl.program_id(0) < used_ref[0]

    @pl.when(jnp.logical_not(in_use) & (c == 0))
    def _():
        y_ref[...] = jnp.zeros_like(y_ref)

    @pl.when(in_use)
    def _():
        @pl.when(c == 0)
        def _():
            hb_ref[...] = xs_ref[...].astype(BF16)
            y_ref[...] = jnp.zeros_like(y_ref)

        h = hb_ref[...]
        gate = jnp.dot(h, wg_ref[0], preferred_element_type=F32)
        up = jnp.dot(h, wu_ref[0], preferred_element_type=F32)
        a = (gate * jax.nn.sigmoid(gate) * up).astype(BF16)
        y_ref[...] += jnp.dot(a, wo_ref[0], preferred_element_type=F32)


def _experts(tile_expert, used, xs, w_in, w_out):
    tm = MOE_ROW_TILE
    ck = MOE_FF_CHUNK
    n_ck = D_FF_EXPERT // ck
    n_slots = xs.shape[0]

    def chunk(i, c, te, used):
        return jnp.where(i < used[0], c, n_ck - 1)

    grid_spec = pltpu.PrefetchScalarGridSpec(
        num_scalar_prefetch=2,
        grid=(n_slots // tm, n_ck),
        in_specs=[
            pl.BlockSpec((tm, D_MODEL), lambda i, c, te, used: (jnp.minimum(i, used[0] - 1), 0)),
            pl.BlockSpec((1, D_MODEL, ck), lambda i, c, te, used: (te[i], 0, chunk(i, c, te, used))),
            pl.BlockSpec((1, D_MODEL, ck), lambda i, c, te, used: (te[i], 0, n_ck + chunk(i, c, te, used))),
            pl.BlockSpec((1, ck, D_MODEL), lambda i, c, te, used: (te[i], chunk(i, c, te, used), 0)),
        ],
        out_specs=pl.BlockSpec((tm, D_MODEL), lambda i, c, te, used: (i, 0)),
        scratch_shapes=[pltpu.VMEM((tm, D_MODEL), BF16)],
    )
    return pl.pallas_call(
        _expert_kernel,
        grid_spec=grid_spec,
        out_shape=jax.ShapeDtypeStruct((n_slots, D_MODEL), F32),
        compiler_params=_params("arbitrary", "arbitrary"),
        name="moe_experts",
    )(tile_expert, used, xs, w_in, w_in, w_out)


def _combine_kernel(pos_hbm, y_hbm, x_ref, gate_ref, gf_ref, *rest, split):
    out_refs = rest[:1 if split is None else 2]
    rows_ref, pos_smem, sems, pos_sem = rest[len(out_refs):]
    i = pl.program_id(0)
    n_tiles = pl.num_programs(0)
    tm = x_ref.shape[0]

    def gather(tile, buf):
        pos_copy = _load_positions(pos_hbm, pos_smem, pos_sem, tile)
        pos_copy.start()
        pos_copy.wait()

        def issue(r, carry):
            for k in range(2):
                pltpu.make_async_copy(y_hbm.at[pl.ds(pos_smem[k * tm + r], 1)],
                                      rows_ref.at[buf, k, pl.ds(r, 1)], sems.at[buf]).start()
            return carry

        lax.fori_loop(0, tm, issue, 0, unroll=8)

    @pl.when(i == 0)
    def _():
        gather(0, 0)

    @pl.when(i + 1 < n_tiles)
    def _():
        gather(i + 1, (i + 1) % 2)

    buf = i % 2
    for k in range(2):
        pltpu.make_async_copy(y_hbm.at[pl.ds(0, tm)], rows_ref.at[buf, k], sems.at[buf]).wait()
    gates = gate_ref[...]
    out = x_ref[...] + (gates[:, 0:1] * rows_ref[buf, 0] + gates[:, 1:2] * rows_ref[buf, 1])
    if split is None:
        out_refs[0][...] = out
    else:
        out = _rms(out, gf_ref[...])

        @pl.when(i < split)
        def _():
            out_refs[0][...] = out

        @pl.when(i >= split)
        def _():
            out_refs[1][...] = out


def _combine(pos, y, x, gates, g_final, *, split):
    n_tok = x.shape[0]
    tm = ROW_TILE
    n_tiles = n_tok // tm
    if split is None:
        out_specs = [pl.BlockSpec((tm, D_MODEL), lambda i: (i, 0))]
        out_shape = [jax.ShapeDtypeStruct((n_tok, D_MODEL), F32)]
    else:
        out_specs = [pl.BlockSpec((tm, D_MODEL), lambda i: (jnp.minimum(i, split - 1), 0)),
                     pl.BlockSpec((tm, D_MODEL), lambda i: (jnp.maximum(i - split, 0), 0))]
        out_shape = [jax.ShapeDtypeStruct((split * tm, D_MODEL), F32),
                     jax.ShapeDtypeStruct(((n_tiles - split) * tm, D_MODEL), F32)]
    return pl.pallas_call(
        functools.partial(_combine_kernel, split=split),
        grid=(n_tiles,),
        in_specs=[
            pl.BlockSpec(memory_space=pl.ANY),
            pl.BlockSpec(memory_space=pl.ANY),
            pl.BlockSpec((tm, D_MODEL), lambda i: (i, 0)),
            pl.BlockSpec((tm, LANES), lambda i: (i, 0)),
            _resident((1, D_MODEL), lambda i: (0, 0)),
        ],
        out_specs=out_specs,
        out_shape=out_shape,
        scratch_shapes=[
            pltpu.VMEM((2, 2, tm, D_MODEL), F32),
            pltpu.SMEM((2 * tm,), jnp.int32),
            pltpu.SemaphoreType.DMA((2,)),
            pltpu.SemaphoreType.DMA(()),
        ],
        compiler_params=_params("arbitrary"),
        name="moe_combine",
    )(pos, y, x, gates, g_final)


def _moe(o, w_o, x, g, w_router, w_in, w_out, g_final, *, split):
    x, meta, gates, counts = _router(o, w_o, x, g, w_router)
    pos, tile_expert, used, pad_start, pad_count, n_slots = _moe_plan(meta, counts)
    xs = _dispatch(pad_start, pad_count, used, pos, x, g, n_slots)
    y = _experts(tile_expert, used, xs, w_in, w_out)
    return _combine(pos, y, x, gates, g_final, split=split)


def _rope_tables():
    t = jnp.arange(SEQ)
    row = (t // GRID_W).astype(F32)
    col = (t % GRID_W).astype(F32)
    half = HEAD_DIM // 2
    inv_freq = ROPE_THETA ** (-jnp.arange(0, half, 2, dtype=F32) / half)
    ang = jnp.concatenate([row[:, None] * inv_freq[None, :], col[:, None] * inv_freq[None, :]], axis=-1)
    cos = jnp.repeat(jnp.cos(ang), 2, axis=-1)
    sign = jnp.where(jnp.arange(HEAD_DIM) % 2 == 0, -1.0, 1.0).astype(F32)
    sin = jnp.repeat(jnp.sin(ang), 2, axis=-1) * sign
    reps = LANES // HEAD_DIM
    return jnp.tile(cos, (1, reps)), jnp.tile(sin, (1, reps))


def _t5_bucket(rel):
    half = N_BUCKETS // 2
    ret = jnp.where(rel > 0, half, 0)
    n = jnp.abs(rel)
    max_exact = half // 2
    nf = jnp.maximum(n, 1).astype(F32)
    large = max_exact + (jnp.log(nf / max_exact) / math.log(MAX_DISTANCE / max_exact)
                         * (half - max_exact)).astype(jnp.int32)
    large = jnp.minimum(large, half - 1)
    return ret + jnp.where(n < max_exact, n, large)


def _window_bias(rel_bias):
    r = jnp.arange(Q_BLOCK)[:, None]
    c = jnp.arange(Q_BLOCK + 2 * WINDOW)[None, :]
    rel = c - WINDOW - r
    bucket = _t5_bucket(rel)[:, :, None]
    bias = jnp.zeros(rel.shape + (N_Q_HEADS,), F32)
    for b in range(N_BUCKETS):
        bias = jnp.where(bucket == b, rel_bias[b].astype(F32)[None, None, :], bias)
    bias = bias * LOG2E
    bias = jnp.where((jnp.abs(rel) <= WINDOW)[:, :, None], bias, NEG_INF)
    bias = bias.reshape(Q_BLOCK, Q_BLOCK + 2 * WINDOW, N_KV_HEADS, GROUP).transpose(2, 1, 3, 0)
    bias = bias.reshape(N_KV_HEADS, Q_BLOCK + 2 * WINDOW, GROUP * Q_BLOCK)
    key = jnp.arange(Q_BLOCK + 2 * WINDOW)[None, :, None]
    no_left = jnp.where(key < WINDOW, NEG_INF, bias)
    no_right = jnp.where(key >= Q_BLOCK + WINDOW, NEG_INF, bias)
    return jnp.stack([bias, no_left, no_right])


def _tile_gain(gain):
    return jnp.tile(gain.astype(F32), LANES // HEAD_DIM)[None, :]


def kernel(x_prompt, x_sample, norm_mix, norm_ffn, norm_final, w_qkv_a, q_gain_a, k_gain_a, w_o_a,
           w_qkv_b, sink_b, w_o_b, rel_bias, w_ff_in, w_ff_out, w_router, w_exp_in, w_exp_out):
    n_prompt = x_prompt.shape[0] * x_prompt.shape[1]
    x = jnp.concatenate([x_prompt.reshape(-1, D_MODEL), x_sample.reshape(-1, D_MODEL)], axis=0)
    cos, sin = _rope_tables()
    bias = _window_bias(rel_bias)
    ones_gain = jnp.ones((1, LANES), F32)
    for i in range(DEPTH):
        j = i // 2
        g_mix = norm_mix[i][None, :]
        g_ffn = norm_ffn[i][None, :]
        if i % 2 == 0:
            q, k, vt = _qkv_proj(x, g_mix, w_qkv_a[j].astype(BF16), cos, sin,
                                 _tile_gain(q_gain_a[j]), _tile_gain(k_gain_a[j]), axial=True)
            o = _global_attention(q, k, vt)
            x = _dense_ffn(o, w_o_a[j].astype(BF16), x, g_ffn, w_ff_in[j].astype(BF16), w_ff_out[j].astype(BF16))
        else:
            q, k, vt = _qkv_proj(x, g_mix, w_qkv_b[j].astype(BF16), cos, sin, ones_gain, ones_gain, axial=False)
            o = _window_attention(q, k, vt, sink_b[j].astype(F32) * LOG2E, bias)
            last = i == DEPTH - 1
            x = _moe(o, w_o_b[j].astype(BF16), x, g_ffn, w_router[j], w_exp_in[j].astype(BF16),
                     w_exp_out[j].astype(BF16), norm_final[None, :],
                     split=n_prompt // ROW_TILE if last else None)
            if not last:
                x = x[0]
    y_prompt, y_sample = x
    return (y_prompt.reshape(x_prompt.shape), y_sample.reshape(x_sample.shape))
```

```python
import functools
import math

import jax
import jax.numpy as jnp
from jax import lax
from jax.experimental import pallas as pl
from jax.experimental.pallas import tpu as pltpu

D_MODEL = 1024
SEQ = 4096
DEPTH = 4
HEAD_DIM = 64
N_Q_HEADS = 16
N_KV_HEADS = 4
GROUP = N_Q_HEADS // N_KV_HEADS
Q_WIDTH = N_Q_HEADS * HEAD_DIM
KV_WIDTH = N_KV_HEADS * HEAD_DIM
QK_WIDTH = Q_WIDTH + KV_WIDTH
QKV_WIDTH = Q_WIDTH + 2 * KV_WIDTH
GRID_W = 64
ROPE_THETA = 10000.0
Q_BLOCK = 128
WINDOW = 128
N_BUCKETS = 32
MAX_DISTANCE = 128
D_FF = 2816
N_EXPERTS = 8
D_FF_EXPERT = 3584
EPS = 1e-6
NEG_INF = -1e30

LANES = 128
VMEM_LIMIT = 56 * 1024 * 1024

ROW_TILE = 512
WINDOW_Q_TILE = 512
WINDOW_LOOKAHEAD = 1
GLOBAL_KV_ROW_TILES = 1
SCORE_LOOKAHEAD = 1
GLOBAL_Q_TILE = 256
MOE_ROW_TILE = 1024
MOE_FF_CHUNK = 512
META_ROWS = 8

K_PAD_WIDTH = N_KV_HEADS * LANES
V_ONES = 16
LOG2E = math.log2(math.e)
Q_SCALE = HEAD_DIM ** -0.5 * LOG2E

F32 = jnp.float32
BF16 = jnp.bfloat16


def _params(*sem):
    return pltpu.CompilerParams(dimension_semantics=sem, vmem_limit_bytes=VMEM_LIMIT)


def _resident(shape, index_map):
    return pl.BlockSpec(shape, index_map, pipeline_mode=pl.Buffered(1))


def _rms(x, g):
    ms = jnp.mean(x * x, axis=-1, keepdims=True)
    return x * lax.rsqrt(ms + EPS) * g


def _store_k_v(k, v, k_ref, vt_ref):
    pad = jnp.zeros((k.shape[0], LANES - HEAD_DIM), BF16)
    for hd in range(N_KV_HEADS):
        k_ref[:, hd * LANES:hd * LANES + HEAD_DIM] = k[:, hd * HEAD_DIM:(hd + 1) * HEAD_DIM].astype(BF16)
        k_ref[:, hd * LANES + HEAD_DIM:(hd + 1) * LANES] = pad
    vt_ref[0, 0] = v.T.astype(BF16)


def _qkv_window_kernel(x_ref, g_ref, w_ref, q_ref, k_ref, vt_ref):
    h = _rms(x_ref[...], g_ref[...]).astype(BF16)
    acc = jnp.dot(h, w_ref[...], preferred_element_type=F32)
    q_ref[...] = (acc[:, :Q_WIDTH] * Q_SCALE).astype(BF16)
    _store_k_v(acc[:, Q_WIDTH:QK_WIDTH], acc[:, QK_WIDTH:], k_ref, vt_ref)


def _x_specs(x, tm):
    if not isinstance(x, tuple):
        return [pl.BlockSpec((tm, D_MODEL), lambda i: (i, 0))]
    split = x[0].shape[0] // tm
    return [pl.BlockSpec((tm, D_MODEL), lambda i: (jnp.minimum(i, split - 1), 0)),
            pl.BlockSpec((tm, D_MODEL), lambda i: (jnp.maximum(i - split, 0), 0))]


def _x_split(x, tm):
    return x[0].shape[0] // tm if isinstance(x, tuple) else None


def _take_x(refs, split):
    if split is None:
        return refs[0][...], refs[1:]
    return jnp.where(pl.program_id(0) < split, refs[0][...], refs[1][...]), refs[2:]


def _qkv_axial_kernel(*refs, split):
    x, (g_ref, w_ref, cq_ref, sq_ref, ck_ref, sk_ref, q_ref, k_ref, vt_ref) = _take_x(refs, split)
    h = _rms(x, g_ref[...]).astype(BF16)
    acc = jnp.dot(h, w_ref[...], preferred_element_type=F32)
    low = lax.broadcasted_iota(jnp.int32, (acc.shape[0], LANES), 1) < HEAD_DIM

    def norm_rope(col, c, s):
        seg = acc[:, col:col + LANES]
        partner = acc[:, QKV_WIDTH + col:QKV_WIDTH + col + LANES]
        sq = seg * seg
        s_all = jnp.sum(sq, axis=-1, keepdims=True)
        s_lo = jnp.sum(jnp.where(low, sq, 0.0), axis=-1, keepdims=True)
        ms = jnp.where(low, s_lo, s_all - s_lo) * (1.0 / HEAD_DIM)
        return (seg * c + partner * s) * lax.rsqrt(ms + EPS)

    cq, sq_, ck, sk = cq_ref[...], sq_ref[...], ck_ref[...], sk_ref[...]
    for j in range(Q_WIDTH // LANES):
        q_ref[:, j * LANES:(j + 1) * LANES] = norm_rope(j * LANES, cq, sq_).astype(BF16)
    k = jnp.concatenate([norm_rope(Q_WIDTH + j * LANES, ck, sk) for j in range(KV_WIDTH // LANES)], axis=1)
    _store_k_v(k, acc[:, QK_WIDTH:QKV_WIDTH], k_ref, vt_ref)


def _qkv_proj(x, g, w, tables=None):
    xs = x if isinstance(x, tuple) else (x,)
    n_tok = sum(a.shape[0] for a in xs)
    tm = ROW_TILE
    tiles_per_seq = SEQ // tm
    n_seq = n_tok // SEQ
    axial = tables is not None
    tables = tuple(tables) if axial else ()
    body = functools.partial(_qkv_axial_kernel, split=_x_split(x, tm)) if axial else _qkv_window_kernel
    return pl.pallas_call(
        body,
        grid=(n_tok // tm,),
        in_specs=_x_specs(x, tm) + [
            _resident((1, D_MODEL), lambda i: (0, 0)),
            _resident(w.shape, lambda i: (0, 0)),
        ] + [pl.BlockSpec((tm, LANES), lambda i: (i % tiles_per_seq, 0)) for _ in tables],
        out_specs=[
            pl.BlockSpec((tm, Q_WIDTH), lambda i: (i, 0)),
            pl.BlockSpec((tm, K_PAD_WIDTH), lambda i: (i, 0)),
            pl.BlockSpec((1, 1, KV_WIDTH, tm), lambda i: (i // tiles_per_seq, i % tiles_per_seq, 0, 0)),
        ],
        out_shape=[
            jax.ShapeDtypeStruct((n_tok, Q_WIDTH), BF16),
            jax.ShapeDtypeStruct((n_tok, K_PAD_WIDTH), BF16),
            jax.ShapeDtypeStruct((n_seq, tiles_per_seq, KV_WIDTH, tm), BF16),
        ],
        compiler_params=_params("parallel"),
        name="qkv_axial" if axial else "qkv_window",
    )(*xs, g, w, *tables)


def _q_transposed(q_ref):
    return q_ref[...].astype(F32).T.astype(BF16)


def _group_queries(qt, h):
    base = h * GROUP * HEAD_DIM
    return jnp.concatenate(
        [qt[base + g * HEAD_DIM:base + (g + 1) * HEAD_DIM, :] for g in range(GROUP)], axis=1)


def _v_with_ones(vt_h, ones):
    return jnp.concatenate([vt_h, ones], axis=0)


def _store_heads(o_ref, outs, tq):
    rows = [o[:, g * tq:(g + 1) * tq] for o in outs for g in range(GROUP)]
    o_ref[...] = jnp.concatenate(rows, axis=0).T.astype(o_ref.dtype)


def _global_attn_kernel(q_ref, k_ref, vt_ref, o_ref, s_ref):
    tq = q_ref.shape[0]
    n_kv = vt_ref.shape[1] // GLOBAL_KV_ROW_TILES
    tk = vt_ref.shape[3] * GLOBAL_KV_ROW_TILES
    cols = GROUP * tq
    qt = _q_transposed(q_ref)
    queries = [_group_queries(qt, h) for h in range(N_KV_HEADS)]
    ones = jnp.ones((V_ONES, tk), BF16)

    def scores(h, j):
        start = pl.multiple_of(j * tk, tk)
        k_h = k_ref[pl.ds(start, tk), h * LANES:h * LANES + HEAD_DIM]
        return jnp.dot(k_h, queries[h], preferred_element_type=F32)

    def update(h, j, s, m, acc):
        m_new = jnp.maximum(m, jnp.max(s, axis=0, keepdims=True))
        p = jnp.exp2(s - m_new).astype(BF16)
        alpha = jnp.exp2(m - m_new)
        vt_h = jnp.concatenate([vt_ref[0, j * GLOBAL_KV_ROW_TILES + t, h * HEAD_DIM:(h + 1) * HEAD_DIM, :]
                                for t in range(GLOBAL_KV_ROW_TILES)], axis=1)
        v_h = _v_with_ones(vt_h, ones)
        return m_new, alpha * acc + jnp.dot(v_h, p, preferred_element_type=F32)

    ahead = s_ref.shape[0]
    for n in range(ahead):
        s_ref[n] = scores(n, 0)

    def body(j, carry):
        new = []
        in_flight = [s_ref[n] for n in range(ahead)]
        for h in range(N_KV_HEADS):
            h_next = h + ahead
            if h_next < N_KV_HEADS:
                in_flight.append(scores(h_next, j))
            else:
                in_flight.append(scores(h_next - N_KV_HEADS, jnp.minimum(j + 1, n_kv - 1)))
            new.append(update(h, j, in_flight.pop(0), *carry[h]))
        for n in range(ahead):
            s_ref[n] = in_flight[n]
        return tuple(new)

    init = tuple((jnp.full((1, cols), -jnp.inf, F32), jnp.zeros((HEAD_DIM + V_ONES, cols), F32))
                 for _ in range(N_KV_HEADS))
    final = lax.fori_loop(0, n_kv, body, init, unroll=2)
    outs = [acc[:HEAD_DIM] / acc[HEAD_DIM:HEAD_DIM + 1] for _, acc in final]
    _store_heads(o_ref, outs, tq)


def _global_attention(q, k, vt):
    n_tok = q.shape[0]
    n_seq = n_tok // SEQ
    tq = GLOBAL_Q_TILE
    nq = SEQ // tq
    return pl.pallas_call(
        _global_attn_kernel,
        grid=(n_seq, nq),
        in_specs=[
            pl.BlockSpec((tq, Q_WIDTH), lambda b, i: (b * nq + i, 0)),
            pl.BlockSpec((SEQ, K_PAD_WIDTH), lambda b, i: (b, 0)),
            pl.BlockSpec((1,) + vt.shape[1:], lambda b, i: (b, 0, 0, 0)),
        ],
        out_specs=pl.BlockSpec((tq, Q_WIDTH), lambda b, i: (b * nq + i, 0)),
        out_shape=jax.ShapeDtypeStruct((n_tok, Q_WIDTH), BF16),
        scratch_shapes=[pltpu.VMEM((SCORE_LOOKAHEAD, GLOBAL_KV_ROW_TILES * vt.shape[3], GROUP * tq), F32)],
        compiler_params=_params("parallel", "arbitrary"),
        name="global_attention",
    )(q, k, vt)


def _window_attn_kernel(sink_ref, q_ref, kl_ref, kc_ref, kr_ref, vtl_ref, vtc_ref, vtr_ref, bias_ref, o_ref):
    i = pl.program_id(1)
    qb = Q_BLOCK
    n_sub = q_ref.shape[0] // qb
    n_keys = 3 * qb
    k_all = jnp.concatenate([kl_ref[...], kc_ref[...], kr_ref[...]], axis=0)
    vt_all = jnp.concatenate([vtl_ref[0, 0], vtc_ref[0, 0], vtr_ref[0, 0]], axis=1)
    qt = _q_transposed(q_ref)
    ones = jnp.ones((V_ONES, n_keys), BF16)
    variants = [jnp.where(i == 0, 1, 0)] + [0] * (n_sub - 2) + [jnp.where(i == pl.num_programs(1) - 1, 2, 0)]
    units = [(u, h) for u in range(n_sub) for h in range(N_KV_HEADS)]

    def scores(u, h):
        keys = k_all[u * qb:u * qb + n_keys, h * LANES:h * LANES + HEAD_DIM]
        return jnp.dot(keys, _group_queries(qt[:, u * qb:(u + 1) * qb], h), preferred_element_type=F32)

    outs = []
    in_flight = [scores(*unit) for unit in units[:WINDOW_LOOKAHEAD]]
    for n, (u, h) in enumerate(units):
        if n + WINDOW_LOOKAHEAD < len(units):
            in_flight.append(scores(*units[n + WINDOW_LOOKAHEAD]))
        s = in_flight.pop(0) + bias_ref[variants[u], h]
        sink = jnp.concatenate(
            [jnp.full((1, qb), sink_ref[h * GROUP + g], F32) for g in range(GROUP)], axis=1)
        m = jnp.maximum(jnp.max(s, axis=0, keepdims=True), sink)
        p = jnp.exp2(s - m).astype(BF16)
        v_h = _v_with_ones(vt_all[h * HEAD_DIM:(h + 1) * HEAD_DIM, u * qb:u * qb + n_keys], ones)
        acc = jnp.dot(v_h, p, preferred_element_type=F32)
        outs.append(acc[:HEAD_DIM] / (acc[HEAD_DIM:HEAD_DIM + 1] + jnp.exp2(sink - m)))
    blocks = []
    for u in range(n_sub):
        heads = outs[u * N_KV_HEADS:(u + 1) * N_KV_HEADS]
        blocks.append(jnp.concatenate([o[:, g * qb:(g + 1) * qb] for o in heads for g in range(GROUP)], axis=0))
    o_ref[...] = jnp.concatenate(blocks, axis=1).T.astype(o_ref.dtype)


def _window_attention(q, k, vt, sinks, bias):
    n_tok = q.shape[0]
    n_seq = n_tok // SEQ
    qb = Q_BLOCK
    tq = WINDOW_Q_TILE
    nq = SEQ // tq
    nb = SEQ // qb
    sub = tq // qb
    assert sub >= 2, "the first and last query block of a step take different bias variants"

    def edge_block(b, i, off):
        return jnp.clip(i * sub + off, 0, nb - 1)

    def k_edge(off):
        return pl.BlockSpec((qb, K_PAD_WIDTH), lambda b, i, s: (b * nb + edge_block(b, i, off), 0))

    def vt_edge(off):
        per = ROW_TILE // qb
        return pl.BlockSpec((1, 1, KV_WIDTH, qb),
                            lambda b, i, s: (b, edge_block(b, i, off) // per, 0, edge_block(b, i, off) % per))

    per_tile = ROW_TILE // tq
    grid_spec = pltpu.PrefetchScalarGridSpec(
        num_scalar_prefetch=1,
        grid=(n_seq, nq),
        in_specs=[
            pl.BlockSpec((tq, Q_WIDTH), lambda b, i, s: (b * nq + i, 0)),
            k_edge(-1),
            pl.BlockSpec((tq, K_PAD_WIDTH), lambda b, i, s: (b * nq + i, 0)),
            k_edge(sub),
            vt_edge(-1),
            pl.BlockSpec((1, 1, KV_WIDTH, tq), lambda b, i, s: (b, i // per_tile, 0, i % per_tile)),
            vt_edge(sub),
            _resident(bias.shape, lambda b, i, s: (0, 0, 0, 0)),
        ],
        out_specs=pl.BlockSpec((tq, Q_WIDTH), lambda b, i, s: (b * nq + i, 0)),
    )
    return pl.pallas_call(
        _window_attn_kernel,
        grid_spec=grid_spec,
        out_shape=jax.ShapeDtypeStruct((n_tok, Q_WIDTH), BF16),
        compiler_params=_params("parallel", "arbitrary"),
        name="window_attention",
    )(sinks, q, k, k, k, vt, vt, vt, bias)


def _attn_residual(o_ref, wo_ref, x):
    return x + jnp.dot(o_ref[...], wo_ref[...], preferred_element_type=F32)


def _attn_residual_specs(x, tm):
    return [
        pl.BlockSpec((tm, Q_WIDTH), lambda i: (i, 0)),
        _resident((Q_WIDTH, D_MODEL), lambda i: (0, 0)),
    ] + _x_specs(x, tm)


def _ff_chunks(d_ff, width):
    return [(c, min(c + width, d_ff)) for c in range(0, d_ff, width)]


def _dense_ffn_kernel(o_ref, wo_ref, *refs, split):
    x_in, (g_ref, w_in_ref, w_out_ref, y_ref) = _take_x(refs, split)
    x = _attn_residual(o_ref, wo_ref, x_in)
    h = _rms(x, g_ref[...]).astype(BF16)
    acc = x
    for c0, c1 in _ff_chunks(D_FF, 1024):
        gate = jnp.dot(h, w_in_ref[:, c0:c1], preferred_element_type=F32)
        up = jnp.dot(h, w_in_ref[:, D_FF + c0:D_FF + c1], preferred_element_type=F32)
        a = (gate * jax.nn.sigmoid(gate) * up).astype(BF16)
        acc = acc + jnp.dot(a, w_out_ref[c0:c1, :], preferred_element_type=F32)
    y_ref[...] = acc


def _dense_ffn(o, w_o, x, g, w_in, w_out):
    xs = x if isinstance(x, tuple) else (x,)
    n_tok = o.shape[0]
    tm = ROW_TILE
    return pl.pallas_call(
        functools.partial(_dense_ffn_kernel, split=_x_split(x, tm)),
        grid=(n_tok // tm,),
        in_specs=_attn_residual_specs(x, tm) + [
            _resident((1, D_MODEL), lambda i: (0, 0)),
            _resident((D_MODEL, 2 * D_FF), lambda i: (0, 0)),
            _resident((D_FF, D_MODEL), lambda i: (0, 0)),
        ],
        out_specs=pl.BlockSpec((tm, D_MODEL), lambda i: (i, 0)),
        out_shape=jax.ShapeDtypeStruct((n_tok, D_MODEL), F32),
        compiler_params=_params("parallel"),
        name="dense_ffn",
    )(o, w_o, *xs, g, w_in, w_out)


def _router_kernel(o_ref, wo_ref, x_ref, g_ref, wr_ref, earlier_ref, xn_ref, meta_ref, gate_ref, count_ref,
                   base_ref):
    @pl.when(pl.program_id(0) == 0)
    def _():
        base_ref[...] = jnp.zeros_like(base_ref)

    x = _attn_residual(o_ref, wo_ref, x_ref[...])
    xn_ref[...] = x
    hf = _rms(x, g_ref[...])
    h_hi = hf.astype(BF16)
    h_lo = (hf - h_hi.astype(F32)).astype(BF16)
    by_hi = jnp.dot(h_hi, wr_ref[...], preferred_element_type=F32)
    by_lo = jnp.dot(h_lo, wr_ref[:, :LANES], preferred_element_type=F32)
    logits = by_hi[:, :LANES] + (by_hi[:, LANES:] + by_lo)
    lane = lax.broadcasted_iota(jnp.int32, logits.shape, 1)
    logits = jnp.where(lane < N_EXPERTS, logits, -jnp.inf)
    m1 = jnp.max(logits, axis=-1, keepdims=True)
    i1 = jnp.min(jnp.where(logits == m1, lane, LANES), axis=-1, keepdims=True)
    rest = jnp.where(lane == i1, -jnp.inf, logits)
    m2 = jnp.max(rest, axis=-1, keepdims=True)
    i2 = jnp.min(jnp.where(rest == m2, lane, LANES), axis=-1, keepdims=True)
    e2 = jnp.exp(m2 - m1)
    g1 = 1.0 / (1.0 + e2)
    g2 = e2 / (1.0 + e2)

    chosen = jnp.where((lane == i1) | (lane == i2), 1.0, 0.0)
    before = jnp.dot(earlier_ref[...], chosen.astype(BF16), preferred_element_type=F32) + base_ref[...]
    r1 = jnp.sum(jnp.where(lane == i1, before, 0.0), axis=-1, keepdims=True)
    r2 = jnp.sum(jnp.where(lane == i2, before, 0.0), axis=-1, keepdims=True)
    meta = jnp.where(lane == 0, i1.astype(F32), jnp.where(lane == 1, i2.astype(F32),
                                                          jnp.where(lane == 2, r1, jnp.where(lane == 3, r2, 0.0))))
    meta_ref[...] = meta.T[:META_ROWS].astype(jnp.int32)
    gate_ref[...] = jnp.where(lane == 0, g1, jnp.where(lane == 1, g2, 0.0))
    base_ref[...] += jnp.sum(chosen, axis=0, keepdims=True)
    count_ref[...] = base_ref[...]


def _router(o, w_o, x, g, w_router):
    n_tok = x.shape[0]
    tm = ROW_TILE
    w_hi = w_router.astype(BF16)
    w_lo = (w_router - w_hi.astype(F32)).astype(BF16)
    pad = ((0, 0), (0, LANES - N_EXPERTS))
    wr = jnp.concatenate([jnp.pad(w_hi, pad), jnp.pad(w_lo, pad)], axis=1)
    earlier = jnp.tril(jnp.ones((tm, tm), BF16), -1)
    return pl.pallas_call(
        _router_kernel,
        grid=(n_tok // tm,),
        in_specs=_attn_residual_specs(x, tm) + [
            _resident((1, D_MODEL), lambda i: (0, 0)),
            _resident((D_MODEL, 2 * LANES), lambda i: (0, 0)),
            _resident((tm, tm), lambda i: (0, 0)),
        ],
        out_specs=[
            pl.BlockSpec((tm, D_MODEL), lambda i: (i, 0)),
            pl.BlockSpec((META_ROWS, tm), lambda i: (0, i)),
            pl.BlockSpec((tm, LANES), lambda i: (i, 0)),
            pl.BlockSpec((1, LANES), lambda i: (0, 0)),
        ],
        out_shape=[
            jax.ShapeDtypeStruct((n_tok, D_MODEL), F32),
            jax.ShapeDtypeStruct((META_ROWS, n_tok), jnp.int32),
            jax.ShapeDtypeStruct((n_tok, LANES), F32),
            jax.ShapeDtypeStruct((1, LANES), F32),
        ],
        scratch_shapes=[pltpu.VMEM((1, LANES), F32)],
        compiler_params=_params("arbitrary"),
        name="router",
    )(o, w_o, x, g, wr, earlier)


def _moe_plan(meta, counts):
    tm = MOE_ROW_TILE
    n_tok = meta.shape[1]
    n_tiles = (2 * n_tok) // tm + N_EXPERTS
    cnt = counts[0, :N_EXPERTS].astype(jnp.int32)
    tiles = (cnt + tm - 1) // tm
    tile_end = jnp.cumsum(tiles)
    offsets = (tile_end - tiles) * tm
    tile_ids = jnp.arange(n_tiles, dtype=jnp.int32)
    tile_expert = jnp.minimum(jnp.sum(tile_ids[:, None] >= tile_end[None, :], axis=1), N_EXPERTS - 1)
    used = tile_end[-1:].astype(jnp.int32)
    experts = jnp.arange(N_EXPERTS, dtype=jnp.int32)

    def slot(expert, rank):
        first = jnp.sum(jnp.where(expert[:, None] == experts[None, :], offsets[None, :], 0), axis=1)
        return first + rank

    pos1 = slot(meta[0], meta[2])
    pos2 = slot(meta[1], meta[3])
    pos = jnp.stack([pos1.reshape(-1, ROW_TILE), pos2.reshape(-1, ROW_TILE)], axis=1).reshape(-1)
    pad_start = (offsets + cnt).astype(jnp.int32)
    pad_count = (tiles * tm - cnt).astype(jnp.int32)
    return (pos.astype(jnp.int32), tile_expert.astype(jnp.int32), used, pad_start, pad_count, n_tiles * tm)


def _load_positions(pos_hbm, pos_smem, pos_sem, tile):
    n = pos_smem.shape[0]
    start = pl.multiple_of(tile * n, n)
    return pltpu.make_async_copy(pos_hbm.at[pl.ds(start, n)], pos_smem, pos_sem)


def _dispatch_kernel(pad_start_ref, pad_count_ref, used_ref, pos_hbm, x_ref, g_ref, xs_out, h_ref, zero_ref,
                     pos_smem, sems, pos_sem, pad_sem):
    i = pl.program_id(0)
    tm = h_ref.shape[1]
    buf = i % 2
    pos_copy = _load_positions(pos_hbm, pos_smem, pos_sem, i)
    pos_copy.start()
    h_ref[buf] = _rms(x_ref[...], g_ref[...])
    pos_copy.wait()

    @pl.when(i == 0)
    def _():
        zero_ref[...] = jnp.zeros_like(zero_ref)

        def zero_copy(slot):
            return pltpu.make_async_copy(zero_ref.at[pl.ds(0, 1)], xs_out.at[pl.ds(slot, 1)], pad_sem)

        for e in range(N_EXPERTS):
            def fill(r, carry):
                zero_copy(pad_start_ref[e] + r).start()
                return carry

            lax.fori_loop(0, pad_count_ref[e], fill, 0)

        tail_start = used_ref[0] * MOE_ROW_TILE
        n_tail = (xs_out.shape[0] - tail_start) // tm

        def tail_copy(t):
            start = pl.multiple_of(tail_start + t * tm, tm)
            return pltpu.make_async_copy(zero_ref, xs_out.at[pl.ds(start, tm)], pad_sem)

        def fill_tail(t, carry):
            tail_copy(t).start()
            return carry

        lax.fori_loop(0, n_tail, fill_tail, 0)
        for e in range(N_EXPERTS):
            def fill_done(r, carry):
                zero_copy(pad_start_ref[e] + r).wait()
                return carry

            lax.fori_loop(0, pad_count_ref[e], fill_done, 0)

        def tail_done(t, carry):
            tail_copy(t).wait()
            return carry

        lax.fori_loop(0, n_tail, tail_done, 0)

    def issue(r, carry):
        for k in range(2):
            pltpu.make_async_copy(h_ref.at[buf, pl.ds(r, 1)], xs_out.at[pl.ds(pos_smem[k * tm + r], 1)],
                                  sems.at[buf]).start()
        return carry

    lax.fori_loop(0, tm, issue, 0, unroll=8)

    def drain(b):
        for _ in range(2):
            pltpu.make_async_copy(h_ref.at[b], xs_out.at[pl.ds(0, tm)], sems.at[b]).wait()

    @pl.when(i > 0)
    def _():
        drain(1 - buf)

    @pl.when(i == pl.num_programs(0) - 1)
    def _():
        drain(buf)


def _dispatch(pad_start, pad_count, used, pos, x, g, n_slots):
    n_tok = x.shape[0]
    tm = ROW_TILE
    grid_spec = pltpu.PrefetchScalarGridSpec(
        num_scalar_prefetch=3,
        grid=(n_tok // tm,),
        in_specs=[
            pl.BlockSpec(memory_space=pl.ANY),
            pl.BlockSpec((tm, D_MODEL), lambda i, ps, pc, u: (i, 0)),
            _resident((1, D_MODEL), lambda i, ps, pc, u: (0, 0)),
        ],
        out_specs=pl.BlockSpec(memory_space=pl.ANY),
        scratch_shapes=[
            pltpu.VMEM((2, tm, D_MODEL), F32),
            pltpu.VMEM((tm, D_MODEL), F32),
            pltpu.SMEM((2 * tm,), jnp.int32),
            pltpu.SemaphoreType.DMA((2,)),
            pltpu.SemaphoreType.DMA(()),
            pltpu.SemaphoreType.DMA(()),
        ],
    )
    return pl.pallas_call(
        _dispatch_kernel,
        grid_spec=grid_spec,
        out_shape=jax.ShapeDtypeStruct((n_slots, D_MODEL), F32),
        compiler_params=_params("arbitrary"),
        name="moe_dispatch",
    )(pad_start, pad_count, used, pos, x, g)


def _expert_kernel(te_ref, used_ref, xs_ref, wg_ref, wu_ref, wo_ref, y_ref, hb_ref):
    del te_ref
    c = pl.program_id(1)
    in_use = pl.program_id(0) < used_ref[0]

    @pl.when(jnp.logical_not(in_use) & (c == 0))
    def _():
        y_ref[...] = jnp.zeros_like(y_ref)

    @pl.when(in_use)
    def _():
        @pl.when(c == 0)
        def _():
            hb_ref[...] = xs_ref[...].astype(BF16)
            y_ref[...] = jnp.zeros_like(y_ref)

        h = hb_ref[...]
        gate = jnp.dot(h, wg_ref[0], preferred_element_type=F32)
        up = jnp.dot(h, wu_ref[0], preferred_element_type=F32)
        a = (gate * jax.nn.sigmoid(gate) * up).astype(BF16)
        y_ref[...] += jnp.dot(a, wo_ref[0], preferred_element_type=F32)


def _experts(tile_expert, used, xs, w_in, w_out):
    tm = MOE_ROW_TILE
    ck = MOE_FF_CHUNK
    n_ck = D_FF_EXPERT // ck
    n_slots = xs.shape[0]

    def chunk(i, c, te, used):
        return jnp.where(i < used[0], c, n_ck - 1)

    grid_spec = pltpu.PrefetchScalarGridSpec(
        num_scalar_prefetch=2,
        grid=(n_slots // tm, n_ck),
        in_specs=[
            pl.BlockSpec((tm, D_MODEL), lambda i, c, te, used: (jnp.minimum(i, used[0] - 1), 0)),
            pl.BlockSpec((1, D_MODEL, ck), lambda i, c, te, used: (te[i], 0, chunk(i, c, te, used))),
            pl.BlockSpec((1, D_MODEL, ck), lambda i, c, te, used: (te[i], 0, n_ck + chunk(i, c, te, used))),
            pl.BlockSpec((1, ck, D_MODEL), lambda i, c, te, used: (te[i], chunk(i, c, te, used), 0)),
        ],
        out_specs=pl.BlockSpec((tm, D_MODEL), lambda i, c, te, used: (i, 0)),
        scratch_shapes=[pltpu.VMEM((tm, D_MODEL), BF16)],
    )
    return pl.pallas_call(
        _expert_kernel,
        grid_spec=grid_spec,
        out_shape=jax.ShapeDtypeStruct((n_slots, D_MODEL), F32),
        compiler_params=_params("arbitrary", "arbitrary"),
        name="moe_experts",
    )(tile_expert, used, xs, w_in, w_in, w_out)


def _combine_kernel(pos_hbm, y_hbm, x_ref, gate_ref, gf_ref, *rest, split):
    out_refs = rest[:1 if split is None else 2]
    rows_ref, pos_smem, sems, pos_sem = rest[len(out_refs):]
    i = pl.program_id(0)
    n_tiles = pl.num_programs(0)
    tm = x_ref.shape[0]

    def gather(tile, buf):
        pos_copy = _load_positions(pos_hbm, pos_smem, pos_sem, tile)
        pos_copy.start()
        pos_copy.wait()

        def issue(r, carry):
            for k in range(2):
                pltpu.make_async_copy(y_hbm.at[pl.ds(pos_smem[k * tm + r], 1)],
                                      rows_ref.at[buf, k, pl.ds(r, 1)], sems.at[buf]).start()
            return carry

        lax.fori_loop(0, tm, issue, 0, unroll=8)

    @pl.when(i == 0)
    def _():
        gather(0, 0)

    @pl.when(i + 1 < n_tiles)
    def _():
        gather(i + 1, (i + 1) % 2)

    buf = i % 2
    for k in range(2):
        pltpu.make_async_copy(y_hbm.at[pl.ds(0, tm)], rows_ref.at[buf, k], sems.at[buf]).wait()
    gates = gate_ref[...]
    out = x_ref[...] + (gates[:, 0:1] * rows_ref[buf, 0] + gates[:, 1:2] * rows_ref[buf, 1])
    if split is None:
        out_refs[0][...] = out
    else:
        out = _rms(out, gf_ref[...])

        @pl.when(i < split)
        def _():
            out_refs[0][...] = out

        @pl.when(i >= split)
        def _():
            out_refs[1][...] = out


def _combine(pos, y, x, gates, g_final, *, split):
    n_tok = x.shape[0]
    tm = ROW_TILE
    n_tiles = n_tok // tm
    if split is None:
        out_specs = [pl.BlockSpec((tm, D_MODEL), lambda i: (i, 0))]
        out_shape = [jax.ShapeDtypeStruct((n_tok, D_MODEL), F32)]
    else:
        out_specs = [pl.BlockSpec((tm, D_MODEL), lambda i: (jnp.minimum(i, split - 1), 0)),
                     pl.BlockSpec((tm, D_MODEL), lambda i: (jnp.maximum(i - split, 0), 0))]
        out_shape = [jax.ShapeDtypeStruct((split * tm, D_MODEL), F32),
                     jax.ShapeDtypeStruct(((n_tiles - split) * tm, D_MODEL), F32)]
    return pl.pallas_call(
        functools.partial(_combine_kernel, split=split),
        grid=(n_tiles,),
        in_specs=[
            pl.BlockSpec(memory_space=pl.ANY),
            pl.BlockSpec(memory_space=pl.ANY),
            pl.BlockSpec((tm, D_MODEL), lambda i: (i, 0)),
            pl.BlockSpec((tm, LANES), lambda i: (i, 0)),
            _resident((1, D_MODEL), lambda i: (0, 0)),
        ],
        out_specs=out_specs,
        out_shape=out_shape,
        scratch_shapes=[
            pltpu.VMEM((2, 2, tm, D_MODEL), F32),
            pltpu.SMEM((2 * tm,), jnp.int32),
            pltpu.SemaphoreType.DMA((2,)),
            pltpu.SemaphoreType.DMA(()),
        ],
        compiler_params=_params("arbitrary"),
        name="moe_combine",
    )(pos, y, x, gates, g_final)


def _moe(o, w_o, x, g, w_router, w_in, w_out, g_final, *, split):
    x, meta, gates, counts = _router(o, w_o, x, g, w_router)
    pos, tile_expert, used, pad_start, pad_count, n_slots = _moe_plan(meta, counts)
    xs = _dispatch(pad_start, pad_count, used, pos, x, g, n_slots)
    y = _experts(tile_expert, used, xs, w_in, w_out)
    return _combine(pos, y, x, gates, g_final, split=split)


def _rope_tables():
    t = jnp.arange(SEQ)
    row = (t // GRID_W).astype(F32)
    col = (t % GRID_W).astype(F32)
    half = HEAD_DIM // 2
    inv_freq = ROPE_THETA ** (-jnp.arange(0, half, 2, dtype=F32) / half)
    ang = jnp.concatenate([row[:, None] * inv_freq[None, :], col[:, None] * inv_freq[None, :]], axis=-1)
    cos = jnp.repeat(jnp.cos(ang), 2, axis=-1)
    sign = jnp.where(jnp.arange(HEAD_DIM) % 2 == 0, -1.0, 1.0).astype(F32)
    sin = jnp.repeat(jnp.sin(ang), 2, axis=-1) * sign
    reps = LANES // HEAD_DIM
    return jnp.tile(cos, (1, reps)), jnp.tile(sin, (1, reps))


def _t5_bucket(rel):
    half = N_BUCKETS // 2
    ret = jnp.where(rel > 0, half, 0)
    n = jnp.abs(rel)
    max_exact = half // 2
    nf = jnp.maximum(n, 1).astype(F32)
    large = max_exact + (jnp.log(nf / max_exact) / math.log(MAX_DISTANCE / max_exact)
                         * (half - max_exact)).astype(jnp.int32)
    large = jnp.minimum(large, half - 1)
    return ret + jnp.where(n < max_exact, n, large)


def _window_bias(rel_bias):
    r = jnp.arange(Q_BLOCK)[:, None]
    c = jnp.arange(Q_BLOCK + 2 * WINDOW)[None, :]
    rel = c - WINDOW - r
    bucket = _t5_bucket(rel)[:, :, None]
    bias = jnp.zeros(rel.shape + (N_Q_HEADS,), F32)
    for b in range(N_BUCKETS):
        bias = jnp.where(bucket == b, rel_bias[b].astype(F32)[None, None, :], bias)
    bias = bias * LOG2E
    bias = jnp.where((jnp.abs(rel) <= WINDOW)[:, :, None], bias, NEG_INF)
    bias = bias.reshape(Q_BLOCK, Q_BLOCK + 2 * WINDOW, N_KV_HEADS, GROUP).transpose(2, 1, 3, 0)
    bias = bias.reshape(N_KV_HEADS, Q_BLOCK + 2 * WINDOW, GROUP * Q_BLOCK)
    key = jnp.arange(Q_BLOCK + 2 * WINDOW)[None, :, None]
    no_left = jnp.where(key < WINDOW, NEG_INF, bias)
    no_right = jnp.where(key >= Q_BLOCK + WINDOW, NEG_INF, bias)
    return jnp.stack([bias, no_left, no_right])


def _swap_pairs(a):
    return a.reshape(a.shape[:-1] + (a.shape[-1] // 2, 2))[..., ::-1].reshape(a.shape)


def _axial_tables(cos, sin, q_gain, k_gain):
    def pair(gain, scale):
        g = jnp.tile(gain.astype(F32), LANES // HEAD_DIM)[None, :]
        return cos * (g * scale), sin * (_swap_pairs(g) * scale)

    return pair(q_gain, Q_SCALE) + pair(k_gain, 1.0)


def _axial_qkv_weights(w):
    return jnp.concatenate([w, _swap_pairs(w[:, :QK_WIDTH])], axis=1).astype(BF16)


def kernel(x_prompt, x_sample, norm_mix, norm_ffn, norm_final, w_qkv_a, q_gain_a, k_gain_a, w_o_a,
           w_qkv_b, sink_b, w_o_b, rel_bias, w_ff_in, w_ff_out, w_router, w_exp_in, w_exp_out):
    n_prompt = x_prompt.shape[0] * x_prompt.shape[1]
    x = (x_prompt.reshape(-1, D_MODEL), x_sample.reshape(-1, D_MODEL))
    cos, sin = _rope_tables()
    bias = _window_bias(rel_bias)
    for i in range(DEPTH):
        j = i // 2
        g_mix = norm_mix[i][None, :]
        g_ffn = norm_ffn[i][None, :]
        if i % 2 == 0:
            q, k, vt = _qkv_proj(x, g_mix, _axial_qkv_weights(w_qkv_a[j]),
                                 _axial_tables(cos, sin, q_gain_a[j], k_gain_a[j]))
            o = _global_attention(q, k, vt)
            x = _dense_ffn(o, w_o_a[j].astype(BF16), x, g_ffn, w_ff_in[j].astype(BF16), w_ff_out[j].astype(BF16))
        else:
            q, k, vt = _qkv_proj(x, g_mix, w_qkv_b[j].astype(BF16))
            o = _window_attention(q, k, vt, sink_b[j].astype(F32) * LOG2E, bias)
            last = i == DEPTH - 1
            x = _moe(o, w_o_b[j].astype(BF16), x, g_ffn, w_router[j], w_exp_in[j].astype(BF16),
                     w_exp_out[j].astype(BF16), norm_final[None, :],
                     split=n_prompt // ROW_TILE if last else None)
            if not last:
                x = x[0]
    y_prompt, y_sample = x
    return (y_prompt.reshape(x_prompt.shape), y_sample.reshape(x_sample.shape))
```

```python
import functools
import math

import jax
import jax.numpy as jnp
from jax import lax
from jax.experimental import pallas as pl
from jax.experimental.pallas import tpu as pltpu

D_MODEL = 1024
SEQ = 4096
DEPTH = 4
HEAD_DIM = 64
N_Q_HEADS = 16
N_KV_HEADS = 4
GROUP = N_Q_HEADS // N_KV_HEADS
Q_WIDTH = N_Q_HEADS * HEAD_DIM
KV_WIDTH = N_KV_HEADS * HEAD_DIM
QK_WIDTH = Q_WIDTH + KV_WIDTH
QKV_WIDTH = Q_WIDTH + 2 * KV_WIDTH
GRID_W = 64
ROPE_THETA = 10000.0
Q_BLOCK = 128
WINDOW = 128
N_BUCKETS = 32
MAX_DISTANCE = 128
D_FF = 2816
N_EXPERTS = 8
D_FF_EXPERT = 3584
EPS = 1e-6
NEG_INF = -1e30

LANES = 128
VMEM_LIMIT = 56 * 1024 * 1024

ROW_TILE = 512
WINDOW_Q_TILE = 512
WINDOW_LOOKAHEAD = 1
GLOBAL_KV_ROW_TILES = 1
SCORE_LOOKAHEAD = 1
GLOBAL_Q_TILE = 256
MOE_ROW_TILE = 1024
MOE_FF_CHUNK = 1792
MOE_INNER_CHUNK = 512
META_ROWS = 8

K_PAD_WIDTH = N_KV_HEADS * LANES
V_ONES = 16
LOG2E = math.log2(math.e)
Q_SCALE = HEAD_DIM ** -0.5 * LOG2E

F32 = jnp.float32
BF16 = jnp.bfloat16


def _params(*sem):
    return pltpu.CompilerParams(dimension_semantics=sem, vmem_limit_bytes=VMEM_LIMIT)


def _resident(shape, index_map):
    return pl.BlockSpec(shape, index_map, pipeline_mode=pl.Buffered(1))


def _rms(x, g):
    ms = jnp.mean(x * x, axis=-1, keepdims=True)
    return x * lax.rsqrt(ms + EPS) * g


def _store_k_v(k, v, k_ref, vt_ref):
    pad = jnp.zeros((k.shape[0], LANES - HEAD_DIM), BF16)
    for hd in range(N_KV_HEADS):
        k_ref[:, hd * LANES:hd * LANES + HEAD_DIM] = k[:, hd * HEAD_DIM:(hd + 1) * HEAD_DIM].astype(BF16)
        k_ref[:, hd * LANES + HEAD_DIM:(hd + 1) * LANES] = pad
    vt_ref[0, 0] = v.T.astype(BF16)


def _qkv_window_kernel(x_ref, g_ref, w_ref, q_ref, k_ref, vt_ref):
    h = _rms(x_ref[...], g_ref[...]).astype(BF16)
    acc = jnp.dot(h, w_ref[...], preferred_element_type=F32)
    q_ref[...] = (acc[:, :Q_WIDTH] * Q_SCALE).astype(BF16)
    _store_k_v(acc[:, Q_WIDTH:QK_WIDTH], acc[:, QK_WIDTH:], k_ref, vt_ref)


def _x_specs(x, tm):
    if not isinstance(x, tuple):
        return [pl.BlockSpec((tm, D_MODEL), lambda i: (i, 0))]
    split = x[0].shape[0] // tm
    return [pl.BlockSpec((tm, D_MODEL), lambda i: (jnp.minimum(i, split - 1), 0)),
            pl.BlockSpec((tm, D_MODEL), lambda i: (jnp.maximum(i - split, 0), 0))]


def _x_split(x, tm):
    return x[0].shape[0] // tm if isinstance(x, tuple) else None


def _take_x(refs, split):
    if split is None:
        return refs[0][...], refs[1:]
    return jnp.where(pl.program_id(0) < split, refs[0][...], refs[1][...]), refs[2:]


def _qkv_axial_kernel(*refs, split):
    x, (g_ref, w_ref, cq_ref, sq_ref, ck_ref, sk_ref, q_ref, k_ref, vt_ref) = _take_x(refs, split)
    h = _rms(x, g_ref[...]).astype(BF16)
    acc = jnp.dot(h, w_ref[...], preferred_element_type=F32)
    low = lax.broadcasted_iota(jnp.int32, (acc.shape[0], LANES), 1) < HEAD_DIM

    def norm_rope(col, c, s):
        seg = acc[:, col:col + LANES]
        partner = acc[:, QKV_WIDTH + col:QKV_WIDTH + col + LANES]
        sq = seg * seg
        s_all = jnp.sum(sq, axis=-1, keepdims=True)
        s_lo = jnp.sum(jnp.where(low, sq, 0.0), axis=-1, keepdims=True)
        ms = jnp.where(low, s_lo, s_all - s_lo) * (1.0 / HEAD_DIM)
        return (seg * c + partner * s) * lax.rsqrt(ms + EPS)

    cq, sq_, ck, sk = cq_ref[...], sq_ref[...], ck_ref[...], sk_ref[...]
    for j in range(Q_WIDTH // LANES):
        q_ref[:, j * LANES:(j + 1) * LANES] = norm_rope(j * LANES, cq, sq_).astype(BF16)
    k = jnp.concatenate([norm_rope(Q_WIDTH + j * LANES, ck, sk) for j in range(KV_WIDTH // LANES)], axis=1)
    _store_k_v(k, acc[:, QK_WIDTH:QKV_WIDTH], k_ref, vt_ref)


def _qkv_proj(x, g, w, tables=None):
    xs = x if isinstance(x, tuple) else (x,)
    n_tok = sum(a.shape[0] for a in xs)
    tm = ROW_TILE
    tiles_per_seq = SEQ // tm
    n_seq = n_tok // SEQ
    axial = tables is not None
    tables = tuple(tables) if axial else ()
    body = functools.partial(_qkv_axial_kernel, split=_x_split(x, tm)) if axial else _qkv_window_kernel
    return pl.pallas_call(
        body,
        grid=(n_tok // tm,),
        in_specs=_x_specs(x, tm) + [
            _resident((1, D_MODEL), lambda i: (0, 0)),
            _resident(w.shape, lambda i: (0, 0)),
        ] + [pl.BlockSpec((tm, LANES), lambda i: (i % tiles_per_seq, 0)) for _ in tables],
        out_specs=[
            pl.BlockSpec((tm, Q_WIDTH), lambda i: (i, 0)),
            pl.BlockSpec((tm, K_PAD_WIDTH), lambda i: (i, 0)),
            pl.BlockSpec((1, 1, KV_WIDTH, tm), lambda i: (i // tiles_per_seq, i % tiles_per_seq, 0, 0)),
        ],
        out_shape=[
            jax.ShapeDtypeStruct((n_tok, Q_WIDTH), BF16),
            jax.ShapeDtypeStruct((n_tok, K_PAD_WIDTH), BF16),
            jax.ShapeDtypeStruct((n_seq, tiles_per_seq, KV_WIDTH, tm), BF16),
        ],
        compiler_params=_params("parallel"),
        name="qkv_axial" if axial else "qkv_window",
    )(*xs, g, w, *tables)


def _q_transposed(q_ref):
    return q_ref[...].astype(F32).T.astype(BF16)


def _group_queries(qt, h):
    base = h * GROUP * HEAD_DIM
    return jnp.concatenate(
        [qt[base + g * HEAD_DIM:base + (g + 1) * HEAD_DIM, :] for g in range(GROUP)], axis=1)


def _v_with_ones(vt_h, ones):
    return jnp.concatenate([vt_h, ones], axis=0)


def _store_heads(o_ref, outs, tq):
    rows = [o[:, g * tq:(g + 1) * tq] for o in outs for g in range(GROUP)]
    o_ref[...] = jnp.concatenate(rows, axis=0).T.astype(o_ref.dtype)


def _global_attn_kernel(q_ref, k_ref, vt_ref, o_ref, s_ref):
    tq = q_ref.shape[0]
    n_kv = vt_ref.shape[1] // GLOBAL_KV_ROW_TILES
    tk = vt_ref.shape[3] * GLOBAL_KV_ROW_TILES
    cols = GROUP * tq
    qt = _q_transposed(q_ref)
    queries = [_group_queries(qt, h) for h in range(N_KV_HEADS)]
    ones = jnp.ones((V_ONES, tk), BF16)

    def scores(h, j):
        start = pl.multiple_of(j * tk, tk)
        k_h = k_ref[pl.ds(start, tk), h * LANES:h * LANES + HEAD_DIM]
        return jnp.dot(k_h, queries[h], preferred_element_type=F32)

    def update(h, j, s, m, acc):
        m_new = jnp.maximum(m, jnp.max(s, axis=0, keepdims=True))
        p = jnp.exp2(s - m_new).astype(BF16)
        alpha = jnp.exp2(m - m_new)
        vt_h = jnp.concatenate([vt_ref[0, j * GLOBAL_KV_ROW_TILES + t, h * HEAD_DIM:(h + 1) * HEAD_DIM, :]
                                for t in range(GLOBAL_KV_ROW_TILES)], axis=1)
        v_h = _v_with_ones(vt_h, ones)
        return m_new, alpha * acc + jnp.dot(v_h, p, preferred_element_type=F32)

    ahead = s_ref.shape[0]
    for n in range(ahead):
        s_ref[n] = scores(n, 0)

    def body(j, carry):
        new = []
        in_flight = [s_ref[n] for n in range(ahead)]
        for h in range(N_KV_HEADS):
            h_next = h + ahead
            if h_next < N_KV_HEADS:
                in_flight.append(scores(h_next, j))
            else:
                in_flight.append(scores(h_next - N_KV_HEADS, jnp.minimum(j + 1, n_kv - 1)))
            new.append(update(h, j, in_flight.pop(0), *carry[h]))
        for n in range(ahead):
            s_ref[n] = in_flight[n]
        return tuple(new)

    init = tuple((jnp.full((1, cols), -jnp.inf, F32), jnp.zeros((HEAD_DIM + V_ONES, cols), F32))
                 for _ in range(N_KV_HEADS))
    final = lax.fori_loop(0, n_kv, body, init, unroll=2)
    outs = [acc[:HEAD_DIM] / acc[HEAD_DIM:HEAD_DIM + 1] for _, acc in final]
    _store_heads(o_ref, outs, tq)


def _global_attention(q, k, vt):
    n_tok = q.shape[0]
    n_seq = n_tok // SEQ
    tq = GLOBAL_Q_TILE
    nq = SEQ // tq
    return pl.pallas_call(
        _global_attn_kernel,
        grid=(n_seq, nq),
        in_specs=[
            pl.BlockSpec((tq, Q_WIDTH), lambda b, i: (b * nq + i, 0)),
            pl.BlockSpec((SEQ, K_PAD_WIDTH), lambda b, i: (b, 0)),
            pl.BlockSpec((1,) + vt.shape[1:], lambda b, i: (b, 0, 0, 0)),
        ],
        out_specs=pl.BlockSpec((tq, Q_WIDTH), lambda b, i: (b * nq + i, 0)),
        out_shape=jax.ShapeDtypeStruct((n_tok, Q_WIDTH), BF16),
        scratch_shapes=[pltpu.VMEM((SCORE_LOOKAHEAD, GLOBAL_KV_ROW_TILES * vt.shape[3], GROUP * tq), F32)],
        compiler_params=_params("parallel", "arbitrary"),
        name="global_attention",
    )(q, k, vt)


def _window_attn_kernel(sink_ref, q_ref, kl_ref, kc_ref, kr_ref, vtl_ref, vtc_ref, vtr_ref, bias_ref, o_ref):
    i = pl.program_id(1)
    qb = Q_BLOCK
    n_sub = q_ref.shape[0] // qb
    n_keys = 3 * qb
    k_all = jnp.concatenate([kl_ref[...], kc_ref[...], kr_ref[...]], axis=0)
    vt_all = jnp.concatenate([vtl_ref[0, 0], vtc_ref[0, 0], vtr_ref[0, 0]], axis=1)
    qt = _q_transposed(q_ref)
    ones = jnp.ones((V_ONES, n_keys), BF16)
    variants = [jnp.where(i == 0, 1, 0)] + [0] * (n_sub - 2) + [jnp.where(i == pl.num_programs(1) - 1, 2, 0)]
    units = [(u, h) for u in range(n_sub) for h in range(N_KV_HEADS)]

    def scores(u, h):
        keys = k_all[u * qb:u * qb + n_keys, h * LANES:h * LANES + HEAD_DIM]
        return jnp.dot(keys, _group_queries(qt[:, u * qb:(u + 1) * qb], h), preferred_element_type=F32)

    outs = []
    in_flight = [scores(*unit) for unit in units[:WINDOW_LOOKAHEAD]]
    for n, (u, h) in enumerate(units):
        if n + WINDOW_LOOKAHEAD < len(units):
            in_flight.append(scores(*units[n + WINDOW_LOOKAHEAD]))
        s = in_flight.pop(0) + bias_ref[variants[u], h]
        sink = jnp.concatenate(
            [jnp.full((1, qb), sink_ref[h * GROUP + g], F32) for g in range(GROUP)], axis=1)
        m = jnp.maximum(jnp.max(s, axis=0, keepdims=True), sink)
        p = jnp.exp2(s - m).astype(BF16)
        v_h = _v_with_ones(vt_all[h * HEAD_DIM:(h + 1) * HEAD_DIM, u * qb:u * qb + n_keys], ones)
        acc = jnp.dot(v_h, p, preferred_element_type=F32)
        outs.append(acc[:HEAD_DIM] / (acc[HEAD_DIM:HEAD_DIM + 1] + jnp.exp2(sink - m)))
    blocks = []
    for u in range(n_sub):
        heads = outs[u * N_KV_HEADS:(u + 1) * N_KV_HEADS]
        blocks.append(jnp.concatenate([o[:, g * qb:(g + 1) * qb] for o in heads for g in range(GROUP)], axis=0))
    o_ref[...] = jnp.concatenate(blocks, axis=1).T.astype(o_ref.dtype)


def _window_attention(q, k, vt, sinks, bias):
    n_tok = q.shape[0]
    n_seq = n_tok // SEQ
    qb = Q_BLOCK
    tq = WINDOW_Q_TILE
    nq = SEQ // tq
    nb = SEQ // qb
    sub = tq // qb
    assert sub >= 2, "the first and last query block of a step take different bias variants"

    def edge_block(b, i, off):
        return jnp.clip(i * sub + off, 0, nb - 1)

    def k_edge(off):
        return pl.BlockSpec((qb, K_PAD_WIDTH), lambda b, i, s: (b * nb + edge_block(b, i, off), 0))

    def vt_edge(off):
        per = ROW_TILE // qb
        return pl.BlockSpec((1, 1, KV_WIDTH, qb),
                            lambda b, i, s: (b, edge_block(b, i, off) // per, 0, edge_block(b, i, off) % per))

    per_tile = ROW_TILE // tq
    grid_spec = pltpu.PrefetchScalarGridSpec(
        num_scalar_prefetch=1,
        grid=(n_seq, nq),
        in_specs=[
            pl.BlockSpec((tq, Q_WIDTH), lambda b, i, s: (b * nq + i, 0)),
            k_edge(-1),
            pl.BlockSpec((tq, K_PAD_WIDTH), lambda b, i, s: (b * nq + i, 0)),
            k_edge(sub),
            vt_edge(-1),
            pl.BlockSpec((1, 1, KV_WIDTH, tq), lambda b, i, s: (b, i // per_tile, 0, i % per_tile)),
            vt_edge(sub),
            _resident(bias.shape, lambda b, i, s: (0, 0, 0, 0)),
        ],
        out_specs=pl.BlockSpec((tq, Q_WIDTH), lambda b, i, s: (b * nq + i, 0)),
    )
    return pl.pallas_call(
        _window_attn_kernel,
        grid_spec=grid_spec,
        out_shape=jax.ShapeDtypeStruct((n_tok, Q_WIDTH), BF16),
        compiler_params=_params("parallel", "arbitrary"),
        name="window_attention",
    )(sinks, q, k, k, k, vt, vt, vt, bias)


def _attn_residual(o_ref, wo_ref, x):
    return x + jnp.dot(o_ref[...], wo_ref[...], preferred_element_type=F32)


def _attn_residual_specs(x, tm):
    return [
        pl.BlockSpec((tm, Q_WIDTH), lambda i: (i, 0)),
        _resident((Q_WIDTH, D_MODEL), lambda i: (0, 0)),
    ] + _x_specs(x, tm)


def _ff_chunks(d_ff, width):
    return [(c, min(c + width, d_ff)) for c in range(0, d_ff, width)]


def _dense_ffn_kernel(o_ref, wo_ref, *refs, split):
    x_in, (g_ref, w_in_ref, w_out_ref, y_ref) = _take_x(refs, split)
    x = _attn_residual(o_ref, wo_ref, x_in)
    h = _rms(x, g_ref[...]).astype(BF16)
    acc = x
    for c0, c1 in _ff_chunks(D_FF, 1024):
        gate = jnp.dot(h, w_in_ref[:, c0:c1], preferred_element_type=F32)
        up = jnp.dot(h, w_in_ref[:, D_FF + c0:D_FF + c1], preferred_element_type=F32)
        a = (gate * jax.nn.sigmoid(gate) * up).astype(BF16)
        acc = acc + jnp.dot(a, w_out_ref[c0:c1, :], preferred_element_type=F32)
    y_ref[...] = acc


def _dense_ffn(o, w_o, x, g, w_in, w_out):
    xs = x if isinstance(x, tuple) else (x,)
    n_tok = o.shape[0]
    tm = ROW_TILE
    return pl.pallas_call(
        functools.partial(_dense_ffn_kernel, split=_x_split(x, tm)),
        grid=(n_tok // tm,),
        in_specs=_attn_residual_specs(x, tm) + [
            _resident((1, D_MODEL), lambda i: (0, 0)),
            _resident((D_MODEL, 2 * D_FF), lambda i: (0, 0)),
            _resident((D_FF, D_MODEL), lambda i: (0, 0)),
        ],
        out_specs=pl.BlockSpec((tm, D_MODEL), lambda i: (i, 0)),
        out_shape=jax.ShapeDtypeStruct((n_tok, D_MODEL), F32),
        compiler_params=_params("parallel"),
        name="dense_ffn",
    )(o, w_o, *xs, g, w_in, w_out)


def _router_kernel(o_ref, wo_ref, x_ref, g_ref, wr_ref, earlier_ref, xn_ref, meta_ref, gate_ref, count_ref,
                   base_ref):
    @pl.when(pl.program_id(0) == 0)
    def _():
        base_ref[...] = jnp.zeros_like(base_ref)

    x = _attn_residual(o_ref, wo_ref, x_ref[...])
    xn_ref[...] = x
    hf = _rms(x, g_ref[...])
    h_hi = hf.astype(BF16)
    h_lo = (hf - h_hi.astype(F32)).astype(BF16)
    by_hi = jnp.dot(h_hi, wr_ref[...], preferred_element_type=F32)
    by_lo = jnp.dot(h_lo, wr_ref[:, :LANES], preferred_element_type=F32)
    logits = by_hi[:, :LANES] + (by_hi[:, LANES:] + by_lo)
    lane = lax.broadcasted_iota(jnp.int32, logits.shape, 1)
    logits = jnp.where(lane < N_EXPERTS, logits, -jnp.inf)
    m1 = jnp.max(logits, axis=-1, keepdims=True)
    i1 = jnp.min(jnp.where(logits == m1, lane, LANES), axis=-1, keepdims=True)
    rest = jnp.where(lane == i1, -jnp.inf, logits)
    m2 = jnp.max(rest, axis=-1, keepdims=True)
    i2 = jnp.min(jnp.where(rest == m2, lane, LANES), axis=-1, keepdims=True)
    e2 = jnp.exp(m2 - m1)
    g1 = 1.0 / (1.0 + e2)
    g2 = e2 / (1.0 + e2)

    chosen = jnp.where((lane == i1) | (lane == i2), 1.0, 0.0)
    before = jnp.dot(earlier_ref[...], chosen.astype(BF16), preferred_element_type=F32) + base_ref[...]
    r1 = jnp.sum(jnp.where(lane == i1, before, 0.0), axis=-1, keepdims=True)
    r2 = jnp.sum(jnp.where(lane == i2, before, 0.0), axis=-1, keepdims=True)
    meta = jnp.where(lane == 0, i1.astype(F32), jnp.where(lane == 1, i2.astype(F32),
                                                          jnp.where(lane == 2, r1, jnp.where(lane == 3, r2, 0.0))))
    meta_ref[...] = meta.T[:META_ROWS].astype(jnp.int32)
    gate_ref[...] = jnp.where(lane == 0, g1, jnp.where(lane == 1, g2, 0.0))
    base_ref[...] += jnp.sum(chosen, axis=0, keepdims=True)
    count_ref[...] = base_ref[...]


def _router(o, w_o, x, g, w_router):
    n_tok = x.shape[0]
    tm = ROW_TILE
    w_hi = w_router.astype(BF16)
    w_lo = (w_router - w_hi.astype(F32)).astype(BF16)
    pad = ((0, 0), (0, LANES - N_EXPERTS))
    wr = jnp.concatenate([jnp.pad(w_hi, pad), jnp.pad(w_lo, pad)], axis=1)
    earlier = jnp.tril(jnp.ones((tm, tm), BF16), -1)
    return pl.pallas_call(
        _router_kernel,
        grid=(n_tok // tm,),
        in_specs=_attn_residual_specs(x, tm) + [
            _resident((1, D_MODEL), lambda i: (0, 0)),
            _resident((D_MODEL, 2 * LANES), lambda i: (0, 0)),
            _resident((tm, tm), lambda i: (0, 0)),
        ],
        out_specs=[
            pl.BlockSpec((tm, D_MODEL), lambda i: (i, 0)),
            pl.BlockSpec((META_ROWS, tm), lambda i: (0, i)),
            pl.BlockSpec((tm, LANES), lambda i: (i, 0)),
            pl.BlockSpec((1, LANES), lambda i: (0, 0)),
        ],
        out_shape=[
            jax.ShapeDtypeStruct((n_tok, D_MODEL), F32),
            jax.ShapeDtypeStruct((META_ROWS, n_tok), jnp.int32),
            jax.ShapeDtypeStruct((n_tok, LANES), F32),
            jax.ShapeDtypeStruct((1, LANES), F32),
        ],
        scratch_shapes=[pltpu.VMEM((1, LANES), F32)],
        compiler_params=_params("arbitrary"),
        name="router",
    )(o, w_o, x, g, wr, earlier)


def _moe_plan(meta, counts):
    tm = MOE_ROW_TILE
    n_tok = meta.shape[1]
    n_tiles = (2 * n_tok) // tm + N_EXPERTS
    cnt = counts[0, :N_EXPERTS].astype(jnp.int32)
    tiles = (cnt + tm - 1) // tm
    tile_end = jnp.cumsum(tiles)
    offsets = (tile_end - tiles) * tm
    tile_ids = jnp.arange(n_tiles, dtype=jnp.int32)
    tile_expert = jnp.minimum(jnp.sum(tile_ids[:, None] >= tile_end[None, :], axis=1), N_EXPERTS - 1)
    used = tile_end[-1:].astype(jnp.int32)
    experts = jnp.arange(N_EXPERTS, dtype=jnp.int32)

    def slot(expert, rank):
        first = jnp.sum(jnp.where(expert[:, None] == experts[None, :], offsets[None, :], 0), axis=1)
        return first + rank

    pos1 = slot(meta[0], meta[2])
    pos2 = slot(meta[1], meta[3])
    pos = jnp.stack([pos1.reshape(-1, ROW_TILE), pos2.reshape(-1, ROW_TILE)], axis=1).reshape(-1)
    pad_start = (offsets + cnt).astype(jnp.int32)
    pad_count = (tiles * tm - cnt).astype(jnp.int32)
    return (pos.astype(jnp.int32), tile_expert.astype(jnp.int32), used, pad_start, pad_count, n_tiles * tm)


def _load_positions(pos_hbm, pos_smem, pos_sem, tile):
    n = pos_smem.shape[0]
    start = pl.multiple_of(tile * n, n)
    return pltpu.make_async_copy(pos_hbm.at[pl.ds(start, n)], pos_smem, pos_sem)


def _dispatch_kernel(pad_start_ref, pad_count_ref, used_ref, pos_hbm, x_ref, g_ref, xs_out, h_ref, zero_ref,
                     pos_smem, sems, pos_sem, pad_sem):
    i = pl.program_id(0)
    tm = h_ref.shape[1]
    buf = i % 2
    pos_copy = _load_positions(pos_hbm, pos_smem, pos_sem, i)
    pos_copy.start()
    h_ref[buf] = _rms(x_ref[...], g_ref[...])
    pos_copy.wait()

    @pl.when(i == 0)
    def _():
        zero_ref[...] = jnp.zeros_like(zero_ref)

        def zero_copy(slot):
            return pltpu.make_async_copy(zero_ref.at[pl.ds(0, 1)], xs_out.at[pl.ds(slot, 1)], pad_sem)

        for e in range(N_EXPERTS):
            def fill(r, carry):
                zero_copy(pad_start_ref[e] + r).start()
                return carry

            lax.fori_loop(0, pad_count_ref[e], fill, 0)

        tail_start = used_ref[0] * MOE_ROW_TILE
        n_tail = (xs_out.shape[0] - tail_start) // tm

        def tail_copy(t):
            start = pl.multiple_of(tail_start + t * tm, tm)
            return pltpu.make_async_copy(zero_ref, xs_out.at[pl.ds(start, tm)], pad_sem)

        def fill_tail(t, carry):
            tail_copy(t).start()
            return carry

        lax.fori_loop(0, n_tail, fill_tail, 0)
        for e in range(N_EXPERTS):
            def fill_done(r, carry):
                zero_copy(pad_start_ref[e] + r).wait()
                return carry

            lax.fori_loop(0, pad_count_ref[e], fill_done, 0)

        def tail_done(t, carry):
            tail_copy(t).wait()
            return carry

        lax.fori_loop(0, n_tail, tail_done, 0)

    def issue(r, carry):
        for k in range(2):
            pltpu.make_async_copy(h_ref.at[buf, pl.ds(r, 1)], xs_out.at[pl.ds(pos_smem[k * tm + r], 1)],
                                  sems.at[buf]).start()
        return carry

    lax.fori_loop(0, tm, issue, 0, unroll=8)

    def drain(b):
        for _ in range(2):
            pltpu.make_async_copy(h_ref.at[b], xs_out.at[pl.ds(0, tm)], sems.at[b]).wait()

    @pl.when(i > 0)
    def _():
        drain(1 - buf)

    @pl.when(i == pl.num_programs(0) - 1)
    def _():
        drain(buf)


def _dispatch(pad_start, pad_count, used, pos, x, g, n_slots):
    n_tok = x.shape[0]
    tm = ROW_TILE
    grid_spec = pltpu.PrefetchScalarGridSpec(
        num_scalar_prefetch=3,
        grid=(n_tok // tm,),
        in_specs=[
            pl.BlockSpec(memory_space=pl.ANY),
            pl.BlockSpec((tm, D_MODEL), lambda i, ps, pc, u: (i, 0)),
            _resident((1, D_MODEL), lambda i, ps, pc, u: (0, 0)),
        ],
        out_specs=pl.BlockSpec(memory_space=pl.ANY),
        scratch_shapes=[
            pltpu.VMEM((2, tm, D_MODEL), F32),
            pltpu.VMEM((tm, D_MODEL), F32),
            pltpu.SMEM((2 * tm,), jnp.int32),
            pltpu.SemaphoreType.DMA((2,)),
            pltpu.SemaphoreType.DMA(()),
            pltpu.SemaphoreType.DMA(()),
        ],
    )
    return pl.pallas_call(
        _dispatch_kernel,
        grid_spec=grid_spec,
        out_shape=jax.ShapeDtypeStruct((n_slots, D_MODEL), F32),
        compiler_params=_params("arbitrary"),
        name="moe_dispatch",
    )(pad_start, pad_count, used, pos, x, g)


def _expert_kernel(te_ref, used_ref, xs_ref, wg_ref, wu_ref, wo_ref, y_ref, hb_ref):
    del te_ref
    c = pl.program_id(1)
    in_use = pl.program_id(0) < used_ref[0]

    @pl.when(jnp.logical_not(in_use) & (c == 0))
    def _():
        y_ref[...] = jnp.zeros_like(y_ref)

    @pl.when(in_use)
    def _():
        @pl.when(c == 0)
        def _():
            hb_ref[...] = xs_ref[...].astype(BF16)
            y_ref[...] = jnp.zeros_like(y_ref)

        h = hb_ref[...]
        y = y_ref[...]
        for c0, c1 in _ff_chunks(wg_ref.shape[3], MOE_INNER_CHUNK):
            gate = jnp.dot(h, wg_ref[0, 0, :, c0:c1], preferred_element_type=F32)
            up = jnp.dot(h, wu_ref[0, 0, :, c0:c1], preferred_element_type=F32)
            a = (gate * jax.nn.sigmoid(gate) * up).astype(BF16)
            y = y + jnp.dot(a, wo_ref[0, 0, c0:c1, :], preferred_element_type=F32)
        y_ref[...] = y


def _experts(tile_expert, used, xs, w_in, w_out, layer):
    tm = MOE_ROW_TILE
    ck = MOE_FF_CHUNK
    n_ck = D_FF_EXPERT // ck
    n_slots = xs.shape[0]

    def chunk(i, c, te, used):
        return jnp.where(i < used[0], c, n_ck - 1)

    grid_spec = pltpu.PrefetchScalarGridSpec(
        num_scalar_prefetch=2,
        grid=(n_slots // tm, n_ck),
        in_specs=[
            pl.BlockSpec((tm, D_MODEL), lambda i, c, te, used: (jnp.minimum(i, used[0] - 1), 0)),
            pl.BlockSpec((1, 1, D_MODEL, ck), lambda i, c, te, used: (layer, te[i], 0, chunk(i, c, te, used))),
            pl.BlockSpec((1, 1, D_MODEL, ck),
                         lambda i, c, te, used: (layer, te[i], 0, n_ck + chunk(i, c, te, used))),
            pl.BlockSpec((1, 1, ck, D_MODEL), lambda i, c, te, used: (layer, te[i], chunk(i, c, te, used), 0)),
        ],
        out_specs=pl.BlockSpec((tm, D_MODEL), lambda i, c, te, used: (i, 0)),
        scratch_shapes=[pltpu.VMEM((tm, D_MODEL), BF16)],
    )
    return pl.pallas_call(
        _expert_kernel,
        grid_spec=grid_spec,
        out_shape=jax.ShapeDtypeStruct((n_slots, D_MODEL), F32),
        compiler_params=_params("arbitrary", "arbitrary"),
        name="moe_experts",
    )(tile_expert, used, xs, w_in, w_in, w_out)


def _combine_kernel(pos_hbm, y_hbm, x_ref, gate_ref, gf_ref, *rest, split):
    out_refs = rest[:1 if split is None else 2]
    rows_ref, pos_smem, sems, pos_sem = rest[len(out_refs):]
    i = pl.program_id(0)
    n_tiles = pl.num_programs(0)
    tm = x_ref.shape[0]

    def gather(tile, buf):
        pos_copy = _load_positions(pos_hbm, pos_smem, pos_sem, tile)
        pos_copy.start()
        pos_copy.wait()

        def issue(r, carry):
            for k in range(2):
                pltpu.make_async_copy(y_hbm.at[pl.ds(pos_smem[k * tm + r], 1)],
                                      rows_ref.at[buf, k, pl.ds(r, 1)], sems.at[buf]).start()
            return carry

        lax.fori_loop(0, tm, issue, 0, unroll=8)

    @pl.when(i == 0)
    def _():
        gather(0, 0)

    @pl.when(i + 1 < n_tiles)
    def _():
        gather(i + 1, (i + 1) % 2)

    buf = i % 2
    for k in range(2):
        pltpu.make_async_copy(y_hbm.at[pl.ds(0, tm)], rows_ref.at[buf, k], sems.at[buf]).wait()
    gates = gate_ref[...]
    out = x_ref[...] + (gates[:, 0:1] * rows_ref[buf, 0] + gates[:, 1:2] * rows_ref[buf, 1])
    if split is None:
        out_refs[0][...] = out
    else:
        out = _rms(out, gf_ref[...])

        @pl.when(i < split)
        def _():
            out_refs[0][...] = out

        @pl.when(i >= split)
        def _():
            out_refs[1][...] = out


def _combine(pos, y, x, gates, g_final, *, split):
    n_tok = x.shape[0]
    tm = ROW_TILE
    n_tiles = n_tok // tm
    if split is None:
        out_specs = [pl.BlockSpec((tm, D_MODEL), lambda i: (i, 0))]
        out_shape = [jax.ShapeDtypeStruct((n_tok, D_MODEL), F32)]
    else:
        out_specs = [pl.BlockSpec((tm, D_MODEL), lambda i: (jnp.minimum(i, split - 1), 0)),
                     pl.BlockSpec((tm, D_MODEL), lambda i: (jnp.maximum(i - split, 0), 0))]
        out_shape = [jax.ShapeDtypeStruct((split * tm, D_MODEL), F32),
                     jax.ShapeDtypeStruct(((n_tiles - split) * tm, D_MODEL), F32)]
    return pl.pallas_call(
        functools.partial(_combine_kernel, split=split),
        grid=(n_tiles,),
        in_specs=[
            pl.BlockSpec(memory_space=pl.ANY),
            pl.BlockSpec(memory_space=pl.ANY),
            pl.BlockSpec((tm, D_MODEL), lambda i: (i, 0)),
            pl.BlockSpec((tm, LANES), lambda i: (i, 0)),
            _resident((1, D_MODEL), lambda i: (0, 0)),
        ],
        out_specs=out_specs,
        out_shape=out_shape,
        scratch_shapes=[
            pltpu.VMEM((2, 2, tm, D_MODEL), F32),
            pltpu.SMEM((2 * tm,), jnp.int32),
            pltpu.SemaphoreType.DMA((2,)),
            pltpu.SemaphoreType.DMA(()),
        ],
        compiler_params=_params("arbitrary"),
        name="moe_combine",
    )(pos, y, x, gates, g_final)


def _moe(o, w_o, x, g, w_router, w_in, w_out, layer, g_final, *, split):
    x, meta, gates, counts = _router(o, w_o, x, g, w_router)
    pos, tile_expert, used, pad_start, pad_count, n_slots = _moe_plan(meta, counts)
    xs = _dispatch(pad_start, pad_count, used, pos, x, g, n_slots)
    y = _experts(tile_expert, used, xs, w_in, w_out, layer)
    return _combine(pos, y, x, gates, g_final, split=split)


def _rope_tables():
    t = jnp.arange(SEQ)
    row = (t // GRID_W).astype(F32)
    col = (t % GRID_W).astype(F32)
    half = HEAD_DIM // 2
    inv_freq = ROPE_THETA ** (-jnp.arange(0, half, 2, dtype=F32) / half)
    ang = jnp.concatenate([row[:, None] * inv_freq[None, :], col[:, None] * inv_freq[None, :]], axis=-1)
    cos = jnp.repeat(jnp.cos(ang), 2, axis=-1)
    sign = jnp.where(jnp.arange(HEAD_DIM) % 2 == 0, -1.0, 1.0).astype(F32)
    sin = jnp.repeat(jnp.sin(ang), 2, axis=-1) * sign
    reps = LANES // HEAD_DIM
    return jnp.tile(cos, (1, reps)), jnp.tile(sin, (1, reps))


def _t5_bucket(rel):
    half = N_BUCKETS // 2
    ret = jnp.where(rel > 0, half, 0)
    n = jnp.abs(rel)
    max_exact = half // 2
    nf = jnp.maximum(n, 1).astype(F32)
    large = max_exact + (jnp.log(nf / max_exact) / math.log(MAX_DISTANCE / max_exact)
                         * (half - max_exact)).astype(jnp.int32)
    large = jnp.minimum(large, half - 1)
    return ret + jnp.where(n < max_exact, n, large)


def _window_bias(rel_bias):
    r = jnp.arange(Q_BLOCK)[:, None]
    c = jnp.arange(Q_BLOCK + 2 * WINDOW)[None, :]
    rel = c - WINDOW - r
    bucket = _t5_bucket(rel)[:, :, None]
    bias = jnp.zeros(rel.shape + (N_Q_HEADS,), F32)
    for b in range(N_BUCKETS):
        bias = jnp.where(bucket == b, rel_bias[b].astype(F32)[None, None, :], bias)
    bias = bias * LOG2E
    bias = jnp.where((jnp.abs(rel) <= WINDOW)[:, :, None], bias, NEG_INF)
    bias = bias.reshape(Q_BLOCK, Q_BLOCK + 2 * WINDOW, N_KV_HEADS, GROUP).transpose(2, 1, 3, 0)
    bias = bias.reshape(N_KV_HEADS, Q_BLOCK + 2 * WINDOW, GROUP * Q_BLOCK)
    key = jnp.arange(Q_BLOCK + 2 * WINDOW)[None, :, None]
    no_left = jnp.where(key < WINDOW, NEG_INF, bias)
    no_right = jnp.where(key >= Q_BLOCK + WINDOW, NEG_INF, bias)
    return jnp.stack([bias, no_left, no_right])


def _swap_pairs(a):
    return a.reshape(a.shape[:-1] + (a.shape[-1] // 2, 2))[..., ::-1].reshape(a.shape)


def _axial_tables(cos, sin, q_gain, k_gain):
    def pair(gain, scale):
        g = jnp.tile(gain.astype(F32), LANES // HEAD_DIM)[None, :]
        return cos * (g * scale), sin * (_swap_pairs(g) * scale)

    return pair(q_gain, Q_SCALE) + pair(k_gain, 1.0)


def _axial_qkv_weights(w):
    return jnp.concatenate([w, _swap_pairs(w[:, :QK_WIDTH])], axis=1).astype(BF16)


def kernel(x_prompt, x_sample, norm_mix, norm_ffn, norm_final, w_qkv_a, q_gain_a, k_gain_a, w_o_a,
           w_qkv_b, sink_b, w_o_b, rel_bias, w_ff_in, w_ff_out, w_router, w_exp_in, w_exp_out):
    n_prompt = x_prompt.shape[0] * x_prompt.shape[1]
    x = (x_prompt.reshape(-1, D_MODEL), x_sample.reshape(-1, D_MODEL))
    cos, sin = _rope_tables()
    bias = _window_bias(rel_bias)
    w_exp_in_bf16 = w_exp_in.astype(BF16)
    w_exp_out_bf16 = w_exp_out.astype(BF16)
    for i in range(DEPTH):
        j = i // 2
        g_mix = norm_mix[i][None, :]
        g_ffn = norm_ffn[i][None, :]
        if i % 2 == 0:
            q, k, vt = _qkv_proj(x, g_mix, _axial_qkv_weights(w_qkv_a[j]),
                                 _axial_tables(cos, sin, q_gain_a[j], k_gain_a[j]))
            o = _global_attention(q, k, vt)
            x = _dense_ffn(o, w_o_a[j].astype(BF16), x, g_ffn, w_ff_in[j].astype(BF16), w_ff_out[j].astype(BF16))
        else:
            q, k, vt = _qkv_proj(x, g_mix, w_qkv_b[j].astype(BF16))
            o = _window_attention(q, k, vt, sink_b[j].astype(F32) * LOG2E, bias)
            last = i == DEPTH - 1
            x = _moe(o, w_o_b[j].astype(BF16), x, g_ffn, w_router[j], w_exp_in_bf16, w_exp_out_bf16, j,
                     norm_final[None, :],
                     split=n_prompt // ROW_TILE if last else None)
            if not last:
                x = x[0]
    y_prompt, y_sample = x
    return (y_prompt.reshape(x_prompt.shape), y_sample.reshape(x_sample.shape))
```

```python
import functools
import math

import jax
import jax.numpy as jnp
from jax import lax
from jax.experimental import pallas as pl
from jax.experimental.pallas import tpu as pltpu

D_MODEL = 1024
SEQ = 4096
DEPTH = 4
HEAD_DIM = 64
N_Q_HEADS = 16
N_KV_HEADS = 4
GROUP = N_Q_HEADS // N_KV_HEADS
Q_WIDTH = N_Q_HEADS * HEAD_DIM
KV_WIDTH = N_KV_HEADS * HEAD_DIM
QK_WIDTH = Q_WIDTH + KV_WIDTH
QKV_WIDTH = Q_WIDTH + 2 * KV_WIDTH
GRID_W = 64
ROPE_THETA = 10000.0
Q_BLOCK = 128
WINDOW = 128
N_BUCKETS = 32
MAX_DISTANCE = 128
D_FF = 2816
N_EXPERTS = 8
D_FF_EXPERT = 3584
EPS = 1e-6
NEG_INF = -1e30

LANES = 128
VMEM_LIMIT = 56 * 1024 * 1024

ROW_TILE = 512
WINDOW_Q_TILE = 512
WINDOW_LOOKAHEAD = 1
GLOBAL_KV_ROW_TILES = 1
SCORE_LOOKAHEAD = 1
GLOBAL_Q_TILE = 256
MOE_ROW_TILE = 1024
MOE_FF_CHUNK = 1792
MOE_INNER_CHUNK = 256
DENSE_FF_CHUNK = 1024
META_ROWS = 8

K_PAD_WIDTH = N_KV_HEADS * LANES
V_ONES = 16
LOG2E = math.log2(math.e)
Q_SCALE = HEAD_DIM ** -0.5 * LOG2E

F32 = jnp.float32
BF16 = jnp.bfloat16


def _params(*sem):
    return pltpu.CompilerParams(dimension_semantics=sem, vmem_limit_bytes=VMEM_LIMIT)


def _resident(shape, index_map):
    return pl.BlockSpec(shape, index_map, pipeline_mode=pl.Buffered(1))


def _rms(x, g):
    ms = jnp.mean(x * x, axis=-1, keepdims=True)
    return x * lax.rsqrt(ms + EPS) * g


def _store_k_v(k, v, k_ref, vt_ref):
    pad = jnp.zeros((k.shape[0], LANES - HEAD_DIM), BF16)
    for hd in range(N_KV_HEADS):
        k_ref[:, hd * LANES:hd * LANES + HEAD_DIM] = k[:, hd * HEAD_DIM:(hd + 1) * HEAD_DIM].astype(BF16)
        k_ref[:, hd * LANES + HEAD_DIM:(hd + 1) * LANES] = pad
    vt_ref[0, 0] = v.astype(BF16).T


def _qkv_window_kernel(x_ref, g_ref, w_ref, q_ref, k_ref, vt_ref):
    h = _rms(x_ref[...], g_ref[...]).astype(BF16)
    acc = jnp.dot(h, w_ref[...], preferred_element_type=F32)
    q_ref[...] = (acc[:, :Q_WIDTH] * Q_SCALE).astype(BF16)
    _store_k_v(acc[:, Q_WIDTH:QK_WIDTH], acc[:, QK_WIDTH:], k_ref, vt_ref)


def _x_specs(x, tm):
    if not isinstance(x, tuple):
        return [pl.BlockSpec((tm, D_MODEL), lambda i: (i, 0))]
    split = x[0].shape[0] // tm
    return [pl.BlockSpec((tm, D_MODEL), lambda i: (jnp.minimum(i, split - 1), 0)),
            pl.BlockSpec((tm, D_MODEL), lambda i: (jnp.maximum(i - split, 0), 0))]


def _x_split(x, tm):
    return x[0].shape[0] // tm if isinstance(x, tuple) else None


def _take_x(refs, split):
    if split is None:
        return refs[0][...], refs[1:]
    return jnp.where(pl.program_id(0) < split, refs[0][...], refs[1][...]), refs[2:]


def _qkv_axial_kernel(*refs, split):
    x, (g_ref, w_ref, cq_ref, sq_ref, ck_ref, sk_ref, q_ref, k_ref, vt_ref) = _take_x(refs, split)
    h = _rms(x, g_ref[...]).astype(BF16)
    acc = jnp.dot(h, w_ref[...], preferred_element_type=F32)
    low = lax.broadcasted_iota(jnp.int32, (acc.shape[0], LANES), 1) < HEAD_DIM

    def norm_rope(col, c, s):
        seg = acc[:, col:col + LANES]
        partner = acc[:, QKV_WIDTH + col:QKV_WIDTH + col + LANES]
        sq = seg * seg
        s_all = jnp.sum(sq, axis=-1, keepdims=True)
        s_lo = jnp.sum(jnp.where(low, sq, 0.0), axis=-1, keepdims=True)
        ms = jnp.where(low, s_lo, s_all - s_lo) * (1.0 / HEAD_DIM)
        return (seg * c + partner * s) * lax.rsqrt(ms + EPS)

    cq, sq_, ck, sk = cq_ref[...], sq_ref[...], ck_ref[...], sk_ref[...]
    for j in range(Q_WIDTH // LANES):
        q_ref[:, j * LANES:(j + 1) * LANES] = norm_rope(j * LANES, cq, sq_).astype(BF16)
    k = jnp.concatenate([norm_rope(Q_WIDTH + j * LANES, ck, sk) for j in range(KV_WIDTH // LANES)], axis=1)
    _store_k_v(k, acc[:, QK_WIDTH:QKV_WIDTH], k_ref, vt_ref)


def _qkv_proj(x, g, w, tables=None):
    xs = x if isinstance(x, tuple) else (x,)
    n_tok = sum(a.shape[0] for a in xs)
    tm = ROW_TILE
    tiles_per_seq = SEQ // tm
    n_seq = n_tok // SEQ
    axial = tables is not None
    tables = tuple(tables) if axial else ()
    body = functools.partial(_qkv_axial_kernel, split=_x_split(x, tm)) if axial else _qkv_window_kernel
    return pl.pallas_call(
        body,
        grid=(n_tok // tm,),
        in_specs=_x_specs(x, tm) + [
            _resident((1, D_MODEL), lambda i: (0, 0)),
            _resident(w.shape, lambda i: (0, 0)),
        ] + [pl.BlockSpec((tm, LANES), lambda i: (i % tiles_per_seq, 0)) for _ in tables],
        out_specs=[
            pl.BlockSpec((tm, Q_WIDTH), lambda i: (i, 0)),
            pl.BlockSpec((tm, K_PAD_WIDTH), lambda i: (i, 0)),
            pl.BlockSpec((1, 1, KV_WIDTH, tm), lambda i: (i // tiles_per_seq, i % tiles_per_seq, 0, 0)),
        ],
        out_shape=[
            jax.ShapeDtypeStruct((n_tok, Q_WIDTH), BF16),
            jax.ShapeDtypeStruct((n_tok, K_PAD_WIDTH), BF16),
            jax.ShapeDtypeStruct((n_seq, tiles_per_seq, KV_WIDTH, tm), BF16),
        ],
        compiler_params=_params("parallel"),
        name="qkv_axial" if axial else "qkv_window",
    )(*xs, g, w, *tables)


def _q_transposed(q_ref):
    return q_ref[...].T


def _group_queries(qt, h):
    base = h * GROUP * HEAD_DIM
    return jnp.concatenate(
        [qt[base + g * HEAD_DIM:base + (g + 1) * HEAD_DIM, :] for g in range(GROUP)], axis=1)


def _v_with_ones(vt_h, ones):
    return jnp.concatenate([vt_h, ones], axis=0)


def _store_heads(o_ref, outs, tq):
    rows = [o[:, g * tq:(g + 1) * tq] for o in outs for g in range(GROUP)]
    o_ref[...] = jnp.concatenate(rows, axis=0).astype(o_ref.dtype).T


def _global_attn_kernel(q_ref, k_ref, vt_ref, o_ref, s_ref):
    tq = q_ref.shape[0]
    n_kv = vt_ref.shape[1] // GLOBAL_KV_ROW_TILES
    tk = vt_ref.shape[3] * GLOBAL_KV_ROW_TILES
    cols = GROUP * tq
    qt = _q_transposed(q_ref)
    queries = [_group_queries(qt, h) for h in range(N_KV_HEADS)]
    ones = jnp.ones((V_ONES, tk), BF16)

    def scores(h, j):
        start = pl.multiple_of(j * tk, tk)
        k_h = k_ref[pl.ds(start, tk), h * LANES:h * LANES + HEAD_DIM]
        return jnp.dot(k_h, queries[h], preferred_element_type=F32)

    def update(h, j, s, m, acc):
        m_new = jnp.maximum(m, jnp.max(s, axis=0, keepdims=True))
        p = jnp.exp2(s - m_new).astype(BF16)
        alpha = jnp.exp2(m - m_new)
        vt_h = jnp.concatenate([vt_ref[0, j * GLOBAL_KV_ROW_TILES + t, h * HEAD_DIM:(h + 1) * HEAD_DIM, :]
                                for t in range(GLOBAL_KV_ROW_TILES)], axis=1)
        v_h = _v_with_ones(vt_h, ones)
        return m_new, alpha * acc + jnp.dot(v_h, p, preferred_element_type=F32)

    ahead = s_ref.shape[0]
    for n in range(ahead):
        s_ref[n] = scores(n, 0)

    def body(j, carry):
        new = []
        in_flight = [s_ref[n] for n in range(ahead)]
        for h in range(N_KV_HEADS):
            h_next = h + ahead
            if h_next < N_KV_HEADS:
                in_flight.append(scores(h_next, j))
            else:
                in_flight.append(scores(h_next - N_KV_HEADS, jnp.minimum(j + 1, n_kv - 1)))
            new.append(update(h, j, in_flight.pop(0), *carry[h]))
        for n in range(ahead):
            s_ref[n] = in_flight[n]
        return tuple(new)

    init = tuple((jnp.full((1, cols), -jnp.inf, F32), jnp.zeros((HEAD_DIM + V_ONES, cols), F32))
                 for _ in range(N_KV_HEADS))
    final = lax.fori_loop(0, n_kv, body, init, unroll=2)
    outs = [acc[:HEAD_DIM] / acc[HEAD_DIM:HEAD_DIM + 1] for _, acc in final]
    _store_heads(o_ref, outs, tq)


def _global_attention(q, k, vt):
    n_tok = q.shape[0]
    n_seq = n_tok // SEQ
    tq = GLOBAL_Q_TILE
    nq = SEQ // tq
    return pl.pallas_call(
        _global_attn_kernel,
        grid=(n_seq, nq),
        in_specs=[
            pl.BlockSpec((tq, Q_WIDTH), lambda b, i: (b * nq + i, 0)),
            pl.BlockSpec((SEQ, K_PAD_WIDTH), lambda b, i: (b, 0)),
            pl.BlockSpec((1,) + vt.shape[1:], lambda b, i: (b, 0, 0, 0)),
        ],
        out_specs=pl.BlockSpec((tq, Q_WIDTH), lambda b, i: (b * nq + i, 0)),
        out_shape=jax.ShapeDtypeStruct((n_tok, Q_WIDTH), BF16),
        scratch_shapes=[pltpu.VMEM((SCORE_LOOKAHEAD, GLOBAL_KV_ROW_TILES * vt.shape[3], GROUP * tq), F32)],
        compiler_params=_params("parallel", "arbitrary"),
        name="global_attention",
    )(q, k, vt)


def _window_attn_kernel(sink_ref, q_ref, kl_ref, kc_ref, kr_ref, vtl_ref, vtc_ref, vtr_ref, bias_ref, o_ref):
    i = pl.program_id(1)
    qb = Q_BLOCK
    n_sub = q_ref.shape[0] // qb
    n_keys = 3 * qb
    k_all = jnp.concatenate([kl_ref[...], kc_ref[...], kr_ref[...]], axis=0)
    vt_all = jnp.concatenate([vtl_ref[0, 0], vtc_ref[0, 0], vtr_ref[0, 0]], axis=1)
    qt = _q_transposed(q_ref)
    ones = jnp.ones((V_ONES, n_keys), BF16)
    variants = [jnp.where(i == 0, 1, 0)] + [0] * (n_sub - 2) + [jnp.where(i == pl.num_programs(1) - 1, 2, 0)]
    units = [(u, h) for u in range(n_sub) for h in range(N_KV_HEADS)]

    def scores(u, h):
        keys = k_all[u * qb:u * qb + n_keys, h * LANES:h * LANES + HEAD_DIM]
        return jnp.dot(keys, _group_queries(qt[:, u * qb:(u + 1) * qb], h), preferred_element_type=F32)

    outs = []
    in_flight = [scores(*unit) for unit in units[:WINDOW_LOOKAHEAD]]
    for n, (u, h) in enumerate(units):
        if n + WINDOW_LOOKAHEAD < len(units):
            in_flight.append(scores(*units[n + WINDOW_LOOKAHEAD]))
        s = in_flight.pop(0) + bias_ref[variants[u], h]
        sink = jnp.concatenate(
            [jnp.full((1, qb), sink_ref[h * GROUP + g], F32) for g in range(GROUP)], axis=1)
        m = jnp.maximum(jnp.max(s, axis=0, keepdims=True), sink)
        p = jnp.exp2(s - m).astype(BF16)
        v_h = _v_with_ones(vt_all[h * HEAD_DIM:(h + 1) * HEAD_DIM, u * qb:u * qb + n_keys], ones)
        acc = jnp.dot(v_h, p, preferred_element_type=F32)
        outs.append(acc[:HEAD_DIM] / (acc[HEAD_DIM:HEAD_DIM + 1] + jnp.exp2(sink - m)))
    blocks = []
    for u in range(n_sub):
        heads = outs[u * N_KV_HEADS:(u + 1) * N_KV_HEADS]
        blocks.append(jnp.concatenate([o[:, g * qb:(g + 1) * qb] for o in heads for g in range(GROUP)], axis=0))
    o_ref[...] = jnp.concatenate(blocks, axis=1).astype(o_ref.dtype).T


def _window_attention(q, k, vt, sinks, bias):
    n_tok = q.shape[0]
    n_seq = n_tok // SEQ
    qb = Q_BLOCK
    tq = WINDOW_Q_TILE
    nq = SEQ // tq
    nb = SEQ // qb
    sub = tq // qb
    assert sub >= 2, "the first and last query block of a step take different bias variants"

    def edge_block(b, i, off):
        return jnp.clip(i * sub + off, 0, nb - 1)

    def k_edge(off):
        return pl.BlockSpec((qb, K_PAD_WIDTH), lambda b, i, s: (b * nb + edge_block(b, i, off), 0))

    def vt_edge(off):
        per = ROW_TILE // qb
        return pl.BlockSpec((1, 1, KV_WIDTH, qb),
                            lambda b, i, s: (b, edge_block(b, i, off) // per, 0, edge_block(b, i, off) % per))

    per_tile = ROW_TILE // tq
    grid_spec = pltpu.PrefetchScalarGridSpec(
        num_scalar_prefetch=1,
        grid=(n_seq, nq),
        in_specs=[
            pl.BlockSpec((tq, Q_WIDTH), lambda b, i, s: (b * nq + i, 0)),
            k_edge(-1),
            pl.BlockSpec((tq, K_PAD_WIDTH), lambda b, i, s: (b * nq + i, 0)),
            k_edge(sub),
            vt_edge(-1),
            pl.BlockSpec((1, 1, KV_WIDTH, tq), lambda b, i, s: (b, i // per_tile, 0, i % per_tile)),
            vt_edge(sub),
            _resident(bias.shape, lambda b, i, s: (0, 0, 0, 0)),
        ],
        out_specs=pl.BlockSpec((tq, Q_WIDTH), lambda b, i, s: (b * nq + i, 0)),
    )
    return pl.pallas_call(
        _window_attn_kernel,
        grid_spec=grid_spec,
        out_shape=jax.ShapeDtypeStruct((n_tok, Q_WIDTH), BF16),
        compiler_params=_params("parallel", "arbitrary"),
        name="window_attention",
    )(sinks, q, k, k, k, vt, vt, vt, bias)


def _attn_residual(o_ref, wo_ref, x):
    return x + jnp.dot(o_ref[...], wo_ref[...], preferred_element_type=F32)


def _attn_residual_specs(x, tm):
    return [
        pl.BlockSpec((tm, Q_WIDTH), lambda i: (i, 0)),
        _resident((Q_WIDTH, D_MODEL), lambda i: (0, 0)),
    ] + _x_specs(x, tm)


def _ff_chunks(d_ff, width):
    return [(c, min(c + width, d_ff)) for c in range(0, d_ff, width)]


def _dense_ffn_kernel(o_ref, wo_ref, *refs, split):
    x_in, (g_ref, w_in_ref, w_out_ref, y_ref) = _take_x(refs, split)
    x = _attn_residual(o_ref, wo_ref, x_in)
    h = _rms(x, g_ref[...]).astype(BF16)
    acc = x
    for c0, c1 in _ff_chunks(D_FF, DENSE_FF_CHUNK):
        gate = jnp.dot(h, w_in_ref[:, c0:c1], preferred_element_type=F32)
        up = jnp.dot(h, w_in_ref[:, D_FF + c0:D_FF + c1], preferred_element_type=F32)
        a = (gate * jax.nn.sigmoid(gate) * up).astype(BF16)
        acc = acc + jnp.dot(a, w_out_ref[c0:c1, :], preferred_element_type=F32)
    y_ref[...] = acc


def _dense_ffn(o, w_o, x, g, w_in, w_out):
    xs = x if isinstance(x, tuple) else (x,)
    n_tok = o.shape[0]
    tm = ROW_TILE
    return pl.pallas_call(
        functools.partial(_dense_ffn_kernel, split=_x_split(x, tm)),
        grid=(n_tok // tm,),
        in_specs=_attn_residual_specs(x, tm) + [
            _resident((1, D_MODEL), lambda i: (0, 0)),
            _resident((D_MODEL, 2 * D_FF), lambda i: (0, 0)),
            _resident((D_FF, D_MODEL), lambda i: (0, 0)),
        ],
        out_specs=pl.BlockSpec((tm, D_MODEL), lambda i: (i, 0)),
        out_shape=jax.ShapeDtypeStruct((n_tok, D_MODEL), F32),
        compiler_params=_params("parallel"),
        name="dense_ffn",
    )(o, w_o, *xs, g, w_in, w_out)


def _router_kernel(o_ref, wo_ref, x_ref, g_ref, wr_ref, earlier_ref, xn_ref, meta_ref, gate_ref, count_ref,
                   base_ref):
    @pl.when(pl.program_id(0) == 0)
    def _():
        base_ref[...] = jnp.zeros_like(base_ref)

    x = _attn_residual(o_ref, wo_ref, x_ref[...])
    xn_ref[...] = x
    hf = _rms(x, g_ref[...])
    h_hi = hf.astype(BF16)
    h_lo = (hf - h_hi.astype(F32)).astype(BF16)
    by_hi = jnp.dot(h_hi, wr_ref[...], preferred_element_type=F32)
    by_lo = jnp.dot(h_lo, wr_ref[:, :LANES], preferred_element_type=F32)
    logits = by_hi[:, :LANES] + (by_hi[:, LANES:] + by_lo)
    lane = lax.broadcasted_iota(jnp.int32, logits.shape, 1)
    logits = jnp.where(lane < N_EXPERTS, logits, -jnp.inf)
    m1 = jnp.max(logits, axis=-1, keepdims=True)
    i1 = jnp.min(jnp.where(logits == m1, lane, LANES), axis=-1, keepdims=True)
    rest = jnp.where(lane == i1, -jnp.inf, logits)
    m2 = jnp.max(rest, axis=-1, keepdims=True)
    i2 = jnp.min(jnp.where(rest == m2, lane, LANES), axis=-1, keepdims=True)
    e2 = jnp.exp(m2 - m1)
    g1 = 1.0 / (1.0 + e2)
    g2 = e2 / (1.0 + e2)

    chosen = jnp.where((lane == i1) | (lane == i2), 1.0, 0.0)
    before = jnp.dot(earlier_ref[...], chosen.astype(BF16), preferred_element_type=F32) + base_ref[...]
    r1 = jnp.sum(jnp.where(lane == i1, before, 0.0), axis=-1, keepdims=True)
    r2 = jnp.sum(jnp.where(lane == i2, before, 0.0), axis=-1, keepdims=True)
    meta = jnp.where(lane == 0, i1.astype(F32), jnp.where(lane == 1, i2.astype(F32),
                                                          jnp.where(lane == 2, r1, jnp.where(lane == 3, r2, 0.0))))
    meta_ref[...] = meta.T[:META_ROWS].astype(jnp.int32)
    gate_ref[...] = jnp.where(lane == 0, g1, jnp.where(lane == 1, g2, 0.0))
    base_ref[...] += jnp.sum(chosen, axis=0, keepdims=True)
    count_ref[...] = base_ref[...]


def _router(o, w_o, x, g, w_router):
    n_tok = x.shape[0]
    tm = ROW_TILE
    w_hi = w_router.astype(BF16)
    w_lo = (w_router - w_hi.astype(F32)).astype(BF16)
    pad = ((0, 0), (0, LANES - N_EXPERTS))
    wr = jnp.concatenate([jnp.pad(w_hi, pad), jnp.pad(w_lo, pad)], axis=1)
    earlier = jnp.tril(jnp.ones((tm, tm), BF16), -1)
    return pl.pallas_call(
        _router_kernel,
        grid=(n_tok // tm,),
        in_specs=_attn_residual_specs(x, tm) + [
            _resident((1, D_MODEL), lambda i: (0, 0)),
            _resident((D_MODEL, 2 * LANES), lambda i: (0, 0)),
            _resident((tm, tm), lambda i: (0, 0)),
        ],
        out_specs=[
            pl.BlockSpec((tm, D_MODEL), lambda i: (i, 0)),
            pl.BlockSpec((META_ROWS, tm), lambda i: (0, i)),
            pl.BlockSpec((tm, LANES), lambda i: (i, 0)),
            pl.BlockSpec((1, LANES), lambda i: (0, 0)),
        ],
        out_shape=[
            jax.ShapeDtypeStruct((n_tok, D_MODEL), F32),
            jax.ShapeDtypeStruct((META_ROWS, n_tok), jnp.int32),
            jax.ShapeDtypeStruct((n_tok, LANES), F32),
            jax.ShapeDtypeStruct((1, LANES), F32),
        ],
        scratch_shapes=[pltpu.VMEM((1, LANES), F32)],
        compiler_params=_params("arbitrary"),
        name="router",
    )(o, w_o, x, g, wr, earlier)


def _moe_plan(meta, counts):
    tm = MOE_ROW_TILE
    n_tok = meta.shape[1]
    n_tiles = (2 * n_tok) // tm + N_EXPERTS
    cnt = counts[0, :N_EXPERTS].astype(jnp.int32)
    tiles = (cnt + tm - 1) // tm
    tile_end = jnp.cumsum(tiles)
    offsets = (tile_end - tiles) * tm
    tile_ids = jnp.arange(n_tiles, dtype=jnp.int32)
    tile_expert = jnp.minimum(jnp.sum(tile_ids[:, None] >= tile_end[None, :], axis=1), N_EXPERTS - 1)
    used = tile_end[-1:].astype(jnp.int32)
    experts = jnp.arange(N_EXPERTS, dtype=jnp.int32)

    def slot(expert, rank):
        first = jnp.sum(jnp.where(expert[:, None] == experts[None, :], offsets[None, :], 0), axis=1)
        return first + rank

    pos1 = slot(meta[0], meta[2])
    pos2 = slot(meta[1], meta[3])
    pos = jnp.stack([pos1.reshape(-1, ROW_TILE), pos2.reshape(-1, ROW_TILE)], axis=1).reshape(-1)
    pad_start = (offsets + cnt).astype(jnp.int32)
    pad_count = (tiles * tm - cnt).astype(jnp.int32)
    return (pos.astype(jnp.int32), tile_expert.astype(jnp.int32), used, pad_start, pad_count, n_tiles * tm)


def _load_positions(pos_hbm, pos_smem, pos_sem, tile):
    n = pos_smem.shape[0]
    start = pl.multiple_of(tile * n, n)
    return pltpu.make_async_copy(pos_hbm.at[pl.ds(start, n)], pos_smem, pos_sem)


def _dispatch_kernel(pad_start_ref, pad_count_ref, used_ref, pos_hbm, x_ref, g_ref, xs_out, h_ref, zero_ref,
                     pos_smem, sems, pos_sem, pad_sem):
    i = pl.program_id(0)
    tm = h_ref.shape[1]
    buf = i % 2
    pos_copy = _load_positions(pos_hbm, pos_smem, pos_sem, i)
    pos_copy.start()
    h_ref[buf] = _rms(x_ref[...], g_ref[...])
    pos_copy.wait()

    @pl.when(i == 0)
    def _():
        zero_ref[...] = jnp.zeros_like(zero_ref)

        def zero_copy(slot):
            return pltpu.make_async_copy(zero_ref.at[pl.ds(0, 1)], xs_out.at[pl.ds(slot, 1)], pad_sem)

        for e in range(N_EXPERTS):
            def fill(r, carry):
                zero_copy(pad_start_ref[e] + r).start()
                return carry

            lax.fori_loop(0, pad_count_ref[e], fill, 0)

        tail_start = used_ref[0] * MOE_ROW_TILE
        n_tail = (xs_out.shape[0] - tail_start) // tm

        def tail_copy(t):
            start = pl.multiple_of(tail_start + t * tm, tm)
            return pltpu.make_async_copy(zero_ref, xs_out.at[pl.ds(start, tm)], pad_sem)

        def fill_tail(t, carry):
            tail_copy(t).start()
            return carry

        lax.fori_loop(0, n_tail, fill_tail, 0)
        for e in range(N_EXPERTS):
            def fill_done(r, carry):
                zero_copy(pad_start_ref[e] + r).wait()
                return carry

            lax.fori_loop(0, pad_count_ref[e], fill_done, 0)

        def tail_done(t, carry):
            tail_copy(t).wait()
            return carry

        lax.fori_loop(0, n_tail, tail_done, 0)

    def issue(r, carry):
        for k in range(2):
            pltpu.make_async_copy(h_ref.at[buf, pl.ds(r, 1)], xs_out.at[pl.ds(pos_smem[k * tm + r], 1)],
                                  sems.at[buf]).start()
        return carry

    lax.fori_loop(0, tm, issue, 0, unroll=8)

    def drain(b):
        for _ in range(2):
            pltpu.make_async_copy(h_ref.at[b], xs_out.at[pl.ds(0, tm)], sems.at[b]).wait()

    @pl.when(i > 0)
    def _():
        drain(1 - buf)

    @pl.when(i == pl.num_programs(0) - 1)
    def _():
        drain(buf)


def _dispatch(pad_start, pad_count, used, pos, x, g, n_slots):
    n_tok = x.shape[0]
    tm = ROW_TILE
    grid_spec = pltpu.PrefetchScalarGridSpec(
        num_scalar_prefetch=3,
        grid=(n_tok // tm,),
        in_specs=[
            pl.BlockSpec(memory_space=pl.ANY),
            pl.BlockSpec((tm, D_MODEL), lambda i, ps, pc, u: (i, 0)),
            _resident((1, D_MODEL), lambda i, ps, pc, u: (0, 0)),
        ],
        out_specs=pl.BlockSpec(memory_space=pl.ANY),
        scratch_shapes=[
            pltpu.VMEM((2, tm, D_MODEL), F32),
            pltpu.VMEM((tm, D_MODEL), F32),
            pltpu.SMEM((2 * tm,), jnp.int32),
            pltpu.SemaphoreType.DMA((2,)),
            pltpu.SemaphoreType.DMA(()),
            pltpu.SemaphoreType.DMA(()),
        ],
    )
    return pl.pallas_call(
        _dispatch_kernel,
        grid_spec=grid_spec,
        out_shape=jax.ShapeDtypeStruct((n_slots, D_MODEL), F32),
        compiler_params=_params("arbitrary"),
        name="moe_dispatch",
    )(pad_start, pad_count, used, pos, x, g)


def _expert_kernel(te_ref, used_ref, xs_ref, wg_ref, wu_ref, wo_ref, y_ref, hb_ref):
    del te_ref
    c = pl.program_id(1)
    in_use = pl.program_id(0) < used_ref[0]

    @pl.when(jnp.logical_not(in_use) & (c == 0))
    def _():
        y_ref[...] = jnp.zeros_like(y_ref)

    @pl.when(in_use)
    def _():
        @pl.when(c == 0)
        def _():
            hb_ref[...] = xs_ref[...].astype(BF16)
            y_ref[...] = jnp.zeros_like(y_ref)

        h = hb_ref[...]
        y = y_ref[...]
        for c0, c1 in _ff_chunks(wg_ref.shape[3], MOE_INNER_CHUNK):
            gate = jnp.dot(h, wg_ref[0, 0, :, c0:c1], preferred_element_type=F32)
            up = jnp.dot(h, wu_ref[0, 0, :, c0:c1], preferred_element_type=F32)
            a = (gate * jax.nn.sigmoid(gate) * up).astype(BF16)
            y = y + jnp.dot(a, wo_ref[0, 0, c0:c1, :], preferred_element_type=F32)
        y_ref[...] = y


def _experts(tile_expert, used, xs, w_in, w_out, layer):
    tm = MOE_ROW_TILE
    ck = MOE_FF_CHUNK
    n_ck = D_FF_EXPERT // ck
    n_slots = xs.shape[0]

    def chunk(i, c, te, used):
        return jnp.where(i < used[0], c, n_ck - 1)

    grid_spec = pltpu.PrefetchScalarGridSpec(
        num_scalar_prefetch=2,
        grid=(n_slots // tm, n_ck),
        in_specs=[
            pl.BlockSpec((tm, D_MODEL), lambda i, c, te, used: (jnp.minimum(i, used[0] - 1), 0)),
            pl.BlockSpec((1, 1, D_MODEL, ck), lambda i, c, te, used: (layer, te[i], 0, chunk(i, c, te, used))),
            pl.BlockSpec((1, 1, D_MODEL, ck),
                         lambda i, c, te, used: (layer, te[i], 0, n_ck + chunk(i, c, te, used))),
            pl.BlockSpec((1, 1, ck, D_MODEL), lambda i, c, te, used: (layer, te[i], chunk(i, c, te, used), 0)),
        ],
        out_specs=pl.BlockSpec((tm, D_MODEL), lambda i, c, te, used: (i, 0)),
        scratch_shapes=[pltpu.VMEM((tm, D_MODEL), BF16)],
    )
    return pl.pallas_call(
        _expert_kernel,
        grid_spec=grid_spec,
        out_shape=jax.ShapeDtypeStruct((n_slots, D_MODEL), F32),
        compiler_params=_params("arbitrary", "arbitrary"),
        name="moe_experts",
    )(tile_expert, used, xs, w_in, w_in, w_out)


def _combine_kernel(pos_hbm, y_hbm, x_ref, gate_ref, gf_ref, *rest, split):
    out_refs = rest[:1 if split is None else 2]
    rows_ref, pos_smem, sems, pos_sem = rest[len(out_refs):]
    i = pl.program_id(0)
    n_tiles = pl.num_programs(0)
    tm = x_ref.shape[0]

    def gather(tile, buf):
        pos_copy = _load_positions(pos_hbm, pos_smem, pos_sem, tile)
        pos_copy.start()
        pos_copy.wait()

        def issue(r, carry):
            for k in range(2):
                pltpu.make_async_copy(y_hbm.at[pl.ds(pos_smem[k * tm + r], 1)],
                                      rows_ref.at[buf, k, pl.ds(r, 1)], sems.at[buf]).start()
            return carry

        lax.fori_loop(0, tm, issue, 0, unroll=8)

    @pl.when(i == 0)
    def _():
        gather(0, 0)

    @pl.when(i + 1 < n_tiles)
    def _():
        gather(i + 1, (i + 1) % 2)

    buf = i % 2
    for k in range(2):
        pltpu.make_async_copy(y_hbm.at[pl.ds(0, tm)], rows_ref.at[buf, k], sems.at[buf]).wait()
    gates = gate_ref[...]
    out = x_ref[...] + (gates[:, 0:1] * rows_ref[buf, 0] + gates[:, 1:2] * rows_ref[buf, 1])
    if split is None:
        out_refs[0][...] = out
    else:
        out = _rms(out, gf_ref[...])

        @pl.when(i < split)
        def _():
            out_refs[0][...] = out

        @pl.when(i >= split)
        def _():
            out_refs[1][...] = out


def _combine(pos, y, x, gates, g_final, *, split):
    n_tok = x.shape[0]
    tm = ROW_TILE
    n_tiles = n_tok // tm
    if split is None:
        out_specs = [pl.BlockSpec((tm, D_MODEL), lambda i: (i, 0))]
        out_shape = [jax.ShapeDtypeStruct((n_tok, D_MODEL), F32)]
    else:
        out_specs = [pl.BlockSpec((tm, D_MODEL), lambda i: (jnp.minimum(i, split - 1), 0)),
                     pl.BlockSpec((tm, D_MODEL), lambda i: (jnp.maximum(i - split, 0), 0))]
        out_shape = [jax.ShapeDtypeStruct((split * tm, D_MODEL), F32),
                     jax.ShapeDtypeStruct(((n_tiles - split) * tm, D_MODEL), F32)]
    return pl.pallas_call(
        functools.partial(_combine_kernel, split=split),
        grid=(n_tiles,),
        in_specs=[
            pl.BlockSpec(memory_space=pl.ANY),
            pl.BlockSpec(memory_space=pl.ANY),
            pl.BlockSpec((tm, D_MODEL), lambda i: (i, 0)),
            pl.BlockSpec((tm, LANES), lambda i: (i, 0)),
            _resident((1, D_MODEL), lambda i: (0, 0)),
        ],
        out_specs=out_specs,
        out_shape=out_shape,
        scratch_shapes=[
            pltpu.VMEM((2, 2, tm, D_MODEL), F32),
            pltpu.SMEM((2 * tm,), jnp.int32),
            pltpu.SemaphoreType.DMA((2,)),
            pltpu.SemaphoreType.DMA(()),
        ],
        compiler_params=_params("arbitrary"),
        name="moe_combine",
    )(pos, y, x, gates, g_final)


def _moe(o, w_o, x, g, w_router, w_in, w_out, layer, g_final, *, split):
    x, meta, gates, counts = _router(o, w_o, x, g, w_router)
    pos, tile_expert, used, pad_start, pad_count, n_slots = _moe_plan(meta, counts)
    xs = _dispatch(pad_start, pad_count, used, pos, x, g, n_slots)
    y = _experts(tile_expert, used, xs, w_in, w_out, layer)
    return _combine(pos, y, x, gates, g_final, split=split)


def _rope_tables():
    t = jnp.arange(SEQ)
    row = (t // GRID_W).astype(F32)
    col = (t % GRID_W).astype(F32)
    half = HEAD_DIM // 2
    inv_freq = ROPE_THETA ** (-jnp.arange(0, half, 2, dtype=F32) / half)
    ang = jnp.concatenate([row[:, None] * inv_freq[None, :], col[:, None] * inv_freq[None, :]], axis=-1)
    cos = jnp.repeat(jnp.cos(ang), 2, axis=-1)
    sign = jnp.where(jnp.arange(HEAD_DIM) % 2 == 0, -1.0, 1.0).astype(F32)
    sin = jnp.repeat(jnp.sin(ang), 2, axis=-1) * sign
    reps = LANES // HEAD_DIM
    return jnp.tile(cos, (1, reps)), jnp.tile(sin, (1, reps))


def _t5_bucket(rel):
    half = N_BUCKETS // 2
    ret = jnp.where(rel > 0, half, 0)
    n = jnp.abs(rel)
    max_exact = half // 2
    nf = jnp.maximum(n, 1).astype(F32)
    large = max_exact + (jnp.log(nf / max_exact) / math.log(MAX_DISTANCE / max_exact)
                         * (half - max_exact)).astype(jnp.int32)
    large = jnp.minimum(large, half - 1)
    return ret + jnp.where(n < max_exact, n, large)


def _window_bias(rel_bias):
    r = jnp.arange(Q_BLOCK)[:, None]
    c = jnp.arange(Q_BLOCK + 2 * WINDOW)[None, :]
    rel = c - WINDOW - r
    bucket = _t5_bucket(rel)[:, :, None]
    bias = jnp.zeros(rel.shape + (N_Q_HEADS,), F32)
    for b in range(N_BUCKETS):
        bias = jnp.where(bucket == b, rel_bias[b].astype(F32)[None, None, :], bias)
    bias = bias * LOG2E
    bias = jnp.where((jnp.abs(rel) <= WINDOW)[:, :, None], bias, NEG_INF)
    bias = bias.reshape(Q_BLOCK, Q_BLOCK + 2 * WINDOW, N_KV_HEADS, GROUP).transpose(2, 1, 3, 0)
    bias = bias.reshape(N_KV_HEADS, Q_BLOCK + 2 * WINDOW, GROUP * Q_BLOCK)
    key = jnp.arange(Q_BLOCK + 2 * WINDOW)[None, :, None]
    no_left = jnp.where(key < WINDOW, NEG_INF, bias)
    no_right = jnp.where(key >= Q_BLOCK + WINDOW, NEG_INF, bias)
    return jnp.stack([bias, no_left, no_right])


def _swap_pairs(a):
    return a.reshape(a.shape[:-1] + (a.shape[-1] // 2, 2))[..., ::-1].reshape(a.shape)


def _axial_tables(cos, sin, q_gain, k_gain):
    def pair(gain, scale):
        g = jnp.tile(gain.astype(F32), LANES // HEAD_DIM)[None, :]
        return cos * (g * scale), sin * (_swap_pairs(g) * scale)

    return pair(q_gain, Q_SCALE) + pair(k_gain, 1.0)


def _axial_qkv_weights(w):
    return jnp.concatenate([w, _swap_pairs(w[:, :QK_WIDTH])], axis=1).astype(BF16)


def kernel(x_prompt, x_sample, norm_mix, norm_ffn, norm_final, w_qkv_a, q_gain_a, k_gain_a, w_o_a,
           w_qkv_b, sink_b, w_o_b, rel_bias, w_ff_in, w_ff_out, w_router, w_exp_in, w_exp_out):
    n_prompt = x_prompt.shape[0] * x_prompt.shape[1]
    x = (x_prompt.reshape(-1, D_MODEL), x_sample.reshape(-1, D_MODEL))
    cos, sin = _rope_tables()
    bias = _window_bias(rel_bias)
    w_exp_in_bf16 = w_exp_in.astype(BF16)
    w_exp_out_bf16 = w_exp_out.astype(BF16)
    for i in range(DEPTH):
        j = i // 2
        g_mix = norm_mix[i][None, :]
        g_ffn = norm_ffn[i][None, :]
        if i % 2 == 0:
            q, k, vt = _qkv_proj(x, g_mix, _axial_qkv_weights(w_qkv_a[j]),
                                 _axial_tables(cos, sin, q_gain_a[j], k_gain_a[j]))
            o = _global_attention(q, k, vt)
            x = _dense_ffn(o, w_o_a[j].astype(BF16), x, g_ffn, w_ff_in[j].astype(BF16), w_ff_out[j].astype(BF16))
        else:
            q, k, vt = _qkv_proj(x, g_mix, w_qkv_b[j].astype(BF16))
            o = _window_attention(q, k, vt, sink_b[j].astype(F32) * LOG2E, bias)
            last = i == DEPTH - 1
            x = _moe(o, w_o_b[j].astype(BF16), x, g_ffn, w_router[j], w_exp_in_bf16, w_exp_out_bf16, j,
                     norm_final[None, :],
                     split=n_prompt // ROW_TILE if last else None)
            if not last:
                x = x[0]
    y_prompt, y_sample = x
    return (y_prompt.reshape(x_prompt.shape), y_sample.reshape(x_sample.shape))
```

```python
import functools
import math

import jax
import jax.numpy as jnp
from jax import lax
from jax.experimental import pallas as pl
from jax.experimental.pallas import tpu as pltpu

D_MODEL = 1024
SEQ = 4096
DEPTH = 4
HEAD_DIM = 64
N_Q_HEADS = 16
N_KV_HEADS = 4
GROUP = N_Q_HEADS // N_KV_HEADS
Q_WIDTH = N_Q_HEADS * HEAD_DIM
KV_WIDTH = N_KV_HEADS * HEAD_DIM
QK_WIDTH = Q_WIDTH + KV_WIDTH
QKV_WIDTH = Q_WIDTH + 2 * KV_WIDTH
GRID_W = 64
ROPE_THETA = 10000.0
Q_BLOCK = 128
WINDOW = 128
N_BUCKETS = 32
MAX_DISTANCE = 128
D_FF = 2816
N_EXPERTS = 8
D_FF_EXPERT = 3584
EPS = 1e-6
NEG_INF = -1e30

LANES = 128
VMEM_LIMIT = 56 * 1024 * 1024

ROW_TILE = 512
WINDOW_Q_TILE = 512
WINDOW_LOOKAHEAD = 1
SAFE_SHIFT = 55.0
BOUND_SLACK = 1.0 + 2.0 ** -10
GLOBAL_KV_ROW_TILES = 1
SCORE_LOOKAHEAD = 1
GLOBAL_Q_TILE = 256
MOE_ROW_TILE = 1024
MOE_FF_CHUNK = 1792
MOE_INNER_CHUNK = 256
DENSE_FF_CHUNK = 1024
META_ROWS = 8

K_PAD_WIDTH = N_KV_HEADS * LANES
V_ONES = 16
LOG2E = math.log2(math.e)
Q_SCALE = HEAD_DIM ** -0.5 * LOG2E

F32 = jnp.float32
BF16 = jnp.bfloat16


def _params(*sem):
    return pltpu.CompilerParams(dimension_semantics=sem, vmem_limit_bytes=VMEM_LIMIT)


def _resident(shape, index_map):
    return pl.BlockSpec(shape, index_map, pipeline_mode=pl.Buffered(1))


def _rms(x, g):
    ms = jnp.mean(x * x, axis=-1, keepdims=True)
    return x * lax.rsqrt(ms + EPS) * g


def _store_k_v(k, v, k_ref, vt_ref):
    pad = jnp.zeros((k.shape[0], LANES - HEAD_DIM), BF16)
    for hd in range(N_KV_HEADS):
        k_ref[:, hd * LANES:hd * LANES + HEAD_DIM] = k[:, hd * HEAD_DIM:(hd + 1) * HEAD_DIM].astype(BF16)
        k_ref[:, hd * LANES + HEAD_DIM:(hd + 1) * LANES] = pad
    vt_ref[0, 0] = v.astype(BF16).T


def _qkv_window_kernel(x_ref, g_ref, w_ref, q_ref, k_ref, vt_ref):
    h = _rms(x_ref[...], g_ref[...]).astype(BF16)
    acc = jnp.dot(h, w_ref[...], preferred_element_type=F32)
    q_ref[...] = (acc[:, :Q_WIDTH] * Q_SCALE).astype(BF16)
    _store_k_v(acc[:, Q_WIDTH:QK_WIDTH], acc[:, QK_WIDTH:], k_ref, vt_ref)


def _x_specs(x, tm):
    if not isinstance(x, tuple):
        return [pl.BlockSpec((tm, D_MODEL), lambda i: (i, 0))]
    split = x[0].shape[0] // tm
    return [pl.BlockSpec((tm, D_MODEL), lambda i: (jnp.minimum(i, split - 1), 0)),
            pl.BlockSpec((tm, D_MODEL), lambda i: (jnp.maximum(i - split, 0), 0))]


def _x_split(x, tm):
    return x[0].shape[0] // tm if isinstance(x, tuple) else None


def _take_x(refs, split):
    if split is None:
        return refs[0][...], refs[1:]
    return jnp.where(pl.program_id(0) < split, refs[0][...], refs[1][...]), refs[2:]


def _qkv_axial_kernel(*refs, split):
    x, (g_ref, w_ref, cq_ref, sq_ref, ck_ref, sk_ref, q_ref, k_ref, vt_ref) = _take_x(refs, split)
    h = _rms(x, g_ref[...]).astype(BF16)
    acc = jnp.dot(h, w_ref[...], preferred_element_type=F32)
    low = lax.broadcasted_iota(jnp.int32, (acc.shape[0], LANES), 1) < HEAD_DIM

    def norm_rope(col, c, s):
        seg = acc[:, col:col + LANES]
        partner = acc[:, QKV_WIDTH + col:QKV_WIDTH + col + LANES]
        sq = seg * seg
        s_all = jnp.sum(sq, axis=-1, keepdims=True)
        s_lo = jnp.sum(jnp.where(low, sq, 0.0), axis=-1, keepdims=True)
        ms = jnp.where(low, s_lo, s_all - s_lo) * (1.0 / HEAD_DIM)
        return (seg * c + partner * s) * lax.rsqrt(ms + EPS)

    cq, sq_, ck, sk = cq_ref[...], sq_ref[...], ck_ref[...], sk_ref[...]
    for j in range(Q_WIDTH // LANES):
        q_ref[:, j * LANES:(j + 1) * LANES] = norm_rope(j * LANES, cq, sq_).astype(BF16)
    k = jnp.concatenate([norm_rope(Q_WIDTH + j * LANES, ck, sk) for j in range(KV_WIDTH // LANES)], axis=1)
    _store_k_v(k, acc[:, QK_WIDTH:QKV_WIDTH], k_ref, vt_ref)


def _qkv_proj(x, g, w, tables=None):
    xs = x if isinstance(x, tuple) else (x,)
    n_tok = sum(a.shape[0] for a in xs)
    tm = ROW_TILE
    tiles_per_seq = SEQ // tm
    n_seq = n_tok // SEQ
    axial = tables is not None
    tables = tuple(tables) if axial else ()
    body = functools.partial(_qkv_axial_kernel, split=_x_split(x, tm)) if axial else _qkv_window_kernel
    return pl.pallas_call(
        body,
        grid=(n_tok // tm,),
        in_specs=_x_specs(x, tm) + [
            _resident((1, D_MODEL), lambda i: (0, 0)),
            _resident(w.shape, lambda i: (0, 0)),
        ] + [pl.BlockSpec((tm, LANES), lambda i: (i % tiles_per_seq, 0)) for _ in tables],
        out_specs=[
            pl.BlockSpec((tm, Q_WIDTH), lambda i: (i, 0)),
            pl.BlockSpec((tm, K_PAD_WIDTH), lambda i: (i, 0)),
            pl.BlockSpec((1, 1, KV_WIDTH, tm), lambda i: (i // tiles_per_seq, i % tiles_per_seq, 0, 0)),
        ],
        out_shape=[
            jax.ShapeDtypeStruct((n_tok, Q_WIDTH), BF16),
            jax.ShapeDtypeStruct((n_tok, K_PAD_WIDTH), BF16),
            jax.ShapeDtypeStruct((n_seq, tiles_per_seq, KV_WIDTH, tm), BF16),
        ],
        compiler_params=_params("parallel"),
        name="qkv_axial" if axial else "qkv_window",
    )(*xs, g, w, *tables)


def _q_transposed(q_ref):
    return q_ref[...].T


def _group_queries(qt, h):
    base = h * GROUP * HEAD_DIM
    return jnp.concatenate(
        [qt[base + g * HEAD_DIM:base + (g + 1) * HEAD_DIM, :] for g in range(GROUP)], axis=1)


def _v_with_ones(vt_h, ones):
    return jnp.concatenate([vt_h, ones], axis=0)


def _store_heads(o_ref, outs, tq):
    rows = [o[:, g * tq:(g + 1) * tq] for o in outs for g in range(GROUP)]
    o_ref[...] = jnp.concatenate(rows, axis=0).astype(o_ref.dtype).T


def _global_attn_kernel(q_ref, k_ref, vt_ref, o_ref, s_ref, kmax_ref):
    tq = q_ref.shape[0]
    n_kv = vt_ref.shape[1] // GLOBAL_KV_ROW_TILES
    tk = vt_ref.shape[3] * GLOBAL_KV_ROW_TILES
    cols = GROUP * tq
    qt = _q_transposed(q_ref)
    queries = [_group_queries(qt, h) for h in range(N_KV_HEADS)]
    ones = jnp.ones((V_ONES, tk), BF16)

    @pl.when(pl.program_id(1) == 0)
    def _():
        def block_max(c, carry):
            rows = k_ref[pl.ds(pl.multiple_of(c * tk, tk), tk), :].astype(F32)
            sq = rows * rows
            return tuple(
                jnp.maximum(carry[h], jnp.max(jnp.sum(sq[:, h * LANES:(h + 1) * LANES], axis=1, keepdims=True),
                                              axis=0, keepdims=True))
                for h in range(N_KV_HEADS))

        k2max = lax.fori_loop(0, k_ref.shape[0] // tk, block_max,
                              tuple(jnp.zeros((1, 1), F32) for _ in range(N_KV_HEADS)))
        for h in range(N_KV_HEADS):
            kmax_ref[h] = jnp.broadcast_to(jnp.sqrt(k2max[h]), kmax_ref.shape[1:])

    qf = qt.astype(F32)

    def logit_bound(h):
        norms = []
        for g in range(GROUP):
            r0 = (h * GROUP + g) * HEAD_DIM
            blk = qf[r0:r0 + HEAD_DIM, :]
            norms.append(jnp.sqrt(jnp.sum(blk * blk, axis=0, keepdims=True)))
        return jnp.concatenate(norms, axis=1) * (kmax_ref[h][0:1, 0:1] * BOUND_SLACK)

    bounds = [logit_bound(h) for h in range(N_KV_HEADS)]
    bound_is_safe = jnp.max(jnp.concatenate(bounds, axis=0)) <= SAFE_SHIFT

    def scores(h, j):
        start = pl.multiple_of(j * tk, tk)
        k_h = k_ref[pl.ds(start, tk), h * LANES:h * LANES + HEAD_DIM]
        return jnp.dot(k_h, queries[h], preferred_element_type=F32)

    def values(h, j):
        vt_h = jnp.concatenate([vt_ref[0, j * GLOBAL_KV_ROW_TILES + t, h * HEAD_DIM:(h + 1) * HEAD_DIM, :]
                                for t in range(GLOBAL_KV_ROW_TILES)], axis=1)
        return _v_with_ones(vt_h, ones)

    def update_fixed(h, j, s, m, acc):
        p = jnp.exp2(s - m).astype(BF16)
        return m, acc + jnp.dot(values(h, j), p, preferred_element_type=F32)

    def update_online(h, j, s, m, acc):
        m_new = jnp.maximum(m, jnp.max(s, axis=0, keepdims=True))
        p = jnp.exp2(s - m_new).astype(BF16)
        alpha = jnp.exp2(m - m_new)
        return m_new, alpha * acc + jnp.dot(values(h, j), p, preferred_element_type=F32)

    ahead = s_ref.shape[0]
    for n in range(ahead):
        s_ref[n] = scores(n, 0)

    def attend(update, shifts):
        def body(j, carry):
            new = []
            in_flight = [s_ref[n] for n in range(ahead)]
            for h in range(N_KV_HEADS):
                h_next = h + ahead
                if h_next < N_KV_HEADS:
                    in_flight.append(scores(h_next, j))
                else:
                    in_flight.append(scores(h_next - N_KV_HEADS, jnp.minimum(j + 1, n_kv - 1)))
                new.append(update(h, j, in_flight.pop(0), *carry[h]))
            for n in range(ahead):
                s_ref[n] = in_flight[n]
            return tuple(new)

        init = tuple((shifts[h], jnp.zeros((HEAD_DIM + V_ONES, cols), F32)) for h in range(N_KV_HEADS))
        final = lax.fori_loop(0, n_kv, body, init, unroll=2)
        outs = [acc[:HEAD_DIM] / acc[HEAD_DIM:HEAD_DIM + 1] for _, acc in final]
        _store_heads(o_ref, outs, tq)

    @pl.when(bound_is_safe)
    def _():
        attend(update_fixed, bounds)

    @pl.when(jnp.logical_not(bound_is_safe))
    def _():
        attend(update_online, [jnp.full((1, cols), -jnp.inf, F32)] * N_KV_HEADS)


def _global_attention(q, k, vt):
    n_tok = q.shape[0]
    n_seq = n_tok // SEQ
    tq = GLOBAL_Q_TILE
    nq = SEQ // tq
    return pl.pallas_call(
        _global_attn_kernel,
        grid=(n_seq, nq),
        in_specs=[
            pl.BlockSpec((tq, Q_WIDTH), lambda b, i: (b * nq + i, 0)),
            pl.BlockSpec((SEQ, K_PAD_WIDTH), lambda b, i: (b, 0)),
            pl.BlockSpec((1,) + vt.shape[1:], lambda b, i: (b, 0, 0, 0)),
        ],
        out_specs=pl.BlockSpec((tq, Q_WIDTH), lambda b, i: (b * nq + i, 0)),
        out_shape=jax.ShapeDtypeStruct((n_tok, Q_WIDTH), BF16),
        scratch_shapes=[pltpu.VMEM((SCORE_LOOKAHEAD, GLOBAL_KV_ROW_TILES * vt.shape[3], GROUP * tq), F32),
                        pltpu.VMEM((N_KV_HEADS, 8, LANES), F32)],
        compiler_params=_params("parallel", "arbitrary"),
        name="global_attention",
    )(q, k, vt)


def _window_attn_kernel(sink_ref, q_ref, kl_ref, kc_ref, kr_ref, vtl_ref, vtc_ref, vtr_ref, bias_ref, o_ref):
    i = pl.program_id(1)
    qb = Q_BLOCK
    n_sub = q_ref.shape[0] // qb
    n_keys = 3 * qb
    k_all = jnp.concatenate([kl_ref[...], kc_ref[...], kr_ref[...]], axis=0)
    vt_all = jnp.concatenate([vtl_ref[0, 0], vtc_ref[0, 0], vtr_ref[0, 0]], axis=1)
    qt = _q_transposed(q_ref)
    ones = jnp.ones((V_ONES, n_keys), BF16)
    variants = [jnp.where(i == 0, 1, 0)] + [0] * (n_sub - 2) + [jnp.where(i == pl.num_programs(1) - 1, 2, 0)]
    units = [(u, h) for u in range(n_sub) for h in range(N_KV_HEADS)]

    def scores(u, h):
        keys = k_all[u * qb:u * qb + n_keys, h * LANES:h * LANES + HEAD_DIM]
        return jnp.dot(keys, _group_queries(qt[:, u * qb:(u + 1) * qb], h), preferred_element_type=F32)

    outs = []
    in_flight = [scores(*unit) for unit in units[:WINDOW_LOOKAHEAD]]
    for n, (u, h) in enumerate(units):
        if n + WINDOW_LOOKAHEAD < len(units):
            in_flight.append(scores(*units[n + WINDOW_LOOKAHEAD]))
        s = in_flight.pop(0) + bias_ref[variants[u], h]
        sink = jnp.concatenate(
            [jnp.full((1, qb), sink_ref[h * GROUP + g], F32) for g in range(GROUP)], axis=1)
        m = jnp.maximum(jnp.max(s, axis=0, keepdims=True), sink)
        p = jnp.exp2(s - m).astype(BF16)
        v_h = _v_with_ones(vt_all[h * HEAD_DIM:(h + 1) * HEAD_DIM, u * qb:u * qb + n_keys], ones)
        acc = jnp.dot(v_h, p, preferred_element_type=F32)
        outs.append(acc[:HEAD_DIM] / (acc[HEAD_DIM:HEAD_DIM + 1] + jnp.exp2(sink - m)))
    blocks = []
    for u in range(n_sub):
        heads = outs[u * N_KV_HEADS:(u + 1) * N_KV_HEADS]
        blocks.append(jnp.concatenate([o[:, g * qb:(g + 1) * qb] for o in heads for g in range(GROUP)], axis=0))
    o_ref[...] = jnp.concatenate(blocks, axis=1).astype(o_ref.dtype).T


def _window_attention(q, k, vt, sinks, bias):
    n_tok = q.shape[0]
    n_seq = n_tok // SEQ
    qb = Q_BLOCK
    tq = WINDOW_Q_TILE
    nq = SEQ // tq
    nb = SEQ // qb
    sub = tq // qb
    assert sub >= 2, "the first and last query block of a step take different bias variants"

    def edge_block(b, i, off):
        return jnp.clip(i * sub + off, 0, nb - 1)

    def k_edge(off):
        return pl.BlockSpec((qb, K_PAD_WIDTH), lambda b, i, s: (b * nb + edge_block(b, i, off), 0))

    def vt_edge(off):
        per = ROW_TILE // qb
        return pl.BlockSpec((1, 1, KV_WIDTH, qb),
                            lambda b, i, s: (b, edge_block(b, i, off) // per, 0, edge_block(b, i, off) % per))

    per_tile = ROW_TILE // tq
    grid_spec = pltpu.PrefetchScalarGridSpec(
        num_scalar_prefetch=1,
        grid=(n_seq, nq),
        in_specs=[
            pl.BlockSpec((tq, Q_WIDTH), lambda b, i, s: (b * nq + i, 0)),
            k_edge(-1),
            pl.BlockSpec((tq, K_PAD_WIDTH), lambda b, i, s: (b * nq + i, 0)),
            k_edge(sub),
            vt_edge(-1),
            pl.BlockSpec((1, 1, KV_WIDTH, tq), lambda b, i, s: (b, i // per_tile, 0, i % per_tile)),
            vt_edge(sub),
            _resident(bias.shape, lambda b, i, s: (0, 0, 0, 0)),
        ],
        out_specs=pl.BlockSpec((tq, Q_WIDTH), lambda b, i, s: (b * nq + i, 0)),
    )
    return pl.pallas_call(
        _window_attn_kernel,
        grid_spec=grid_spec,
        out_shape=jax.ShapeDtypeStruct((n_tok, Q_WIDTH), BF16),
        compiler_params=_params("parallel", "arbitrary"),
        name="window_attention",
    )(sinks, q, k, k, k, vt, vt, vt, bias)


def _attn_residual(o_ref, wo_ref, x):
    return x + jnp.dot(o_ref[...], wo_ref[...], preferred_element_type=F32)


def _attn_residual_specs(x, tm):
    return [
        pl.BlockSpec((tm, Q_WIDTH), lambda i: (i, 0)),
        _resident((Q_WIDTH, D_MODEL), lambda i: (0, 0)),
    ] + _x_specs(x, tm)


def _ff_chunks(d_ff, width):
    return [(c, min(c + width, d_ff)) for c in range(0, d_ff, width)]


def _dense_ffn_kernel(o_ref, wo_ref, *refs, split):
    x_in, (g_ref, w_in_ref, w_out_ref, y_ref) = _take_x(refs, split)
    x = _attn_residual(o_ref, wo_ref, x_in)
    h = _rms(x, g_ref[...]).astype(BF16)
    acc = x
    for c0, c1 in _ff_chunks(D_FF, DENSE_FF_CHUNK):
        gate = jnp.dot(h, w_in_ref[:, c0:c1], preferred_element_type=F32)
        up = jnp.dot(h, w_in_ref[:, D_FF + c0:D_FF + c1], preferred_element_type=F32)
        a = (gate * jax.nn.sigmoid(gate) * up).astype(BF16)
        acc = acc + jnp.dot(a, w_out_ref[c0:c1, :], preferred_element_type=F32)
    y_ref[...] = acc


def _dense_ffn(o, w_o, x, g, w_in, w_out):
    xs = x if isinstance(x, tuple) else (x,)
    n_tok = o.shape[0]
    tm = ROW_TILE
    return pl.pallas_call(
        functools.partial(_dense_ffn_kernel, split=_x_split(x, tm)),
        grid=(n_tok // tm,),
        in_specs=_attn_residual_specs(x, tm) + [
            _resident((1, D_MODEL), lambda i: (0, 0)),
            _resident((D_MODEL, 2 * D_FF), lambda i: (0, 0)),
            _resident((D_FF, D_MODEL), lambda i: (0, 0)),
        ],
        out_specs=pl.BlockSpec((tm, D_MODEL), lambda i: (i, 0)),
        out_shape=jax.ShapeDtypeStruct((n_tok, D_MODEL), F32),
        compiler_params=_params("parallel"),
        name="dense_ffn",
    )(o, w_o, *xs, g, w_in, w_out)


def _router_kernel(o_ref, wo_ref, x_ref, g_ref, wr_ref, earlier_ref, xn_ref, meta_ref, gate_ref, count_ref,
                   base_ref):
    @pl.when(pl.program_id(0) == 0)
    def _():
        base_ref[...] = jnp.zeros_like(base_ref)

    x = _attn_residual(o_ref, wo_ref, x_ref[...])
    xn_ref[...] = x
    hf = _rms(x, g_ref[...])
    h_hi = hf.astype(BF16)
    h_lo = (hf - h_hi.astype(F32)).astype(BF16)
    by_hi = jnp.dot(h_hi, wr_ref[...], preferred_element_type=F32)
    by_lo = jnp.dot(h_lo, wr_ref[:, :LANES], preferred_element_type=F32)
    logits = by_hi[:, :LANES] + (by_hi[:, LANES:] + by_lo)
    lane = lax.broadcasted_iota(jnp.int32, logits.shape, 1)
    logits = jnp.where(lane < N_EXPERTS, logits, -jnp.inf)
    m1 = jnp.max(logits, axis=-1, keepdims=True)
    i1 = jnp.min(jnp.where(logits == m1, lane, LANES), axis=-1, keepdims=True)
    rest = jnp.where(lane == i1, -jnp.inf, logits)
    m2 = jnp.max(rest, axis=-1, keepdims=True)
    i2 = jnp.min(jnp.where(rest == m2, lane, LANES), axis=-1, keepdims=True)
    e2 = jnp.exp(m2 - m1)
    g1 = 1.0 / (1.0 + e2)
    g2 = e2 / (1.0 + e2)

    chosen = jnp.where((lane == i1) | (lane == i2), 1.0, 0.0)
    before = jnp.dot(earlier_ref[...], chosen.astype(BF16), preferred_element_type=F32) + base_ref[...]
    r1 = jnp.sum(jnp.where(lane == i1, before, 0.0), axis=-1, keepdims=True)
    r2 = jnp.sum(jnp.where(lane == i2, before, 0.0), axis=-1, keepdims=True)
    meta = jnp.where(lane == 0, i1.astype(F32), jnp.where(lane == 1, i2.astype(F32),
                                                          jnp.where(lane == 2, r1, jnp.where(lane == 3, r2, 0.0))))
    meta_ref[...] = meta.T[:META_ROWS].astype(jnp.int32)
    gate_ref[...] = jnp.where(lane == 0, g1, jnp.where(lane == 1, g2, 0.0))
    base_ref[...] += jnp.sum(chosen, axis=0, keepdims=True)
    count_ref[...] = base_ref[...]


def _router(o, w_o, x, g, w_router):
    n_tok = x.shape[0]
    tm = ROW_TILE
    w_hi = w_router.astype(BF16)
    w_lo = (w_router - w_hi.astype(F32)).astype(BF16)
    pad = ((0, 0), (0, LANES - N_EXPERTS))
    wr = jnp.concatenate([jnp.pad(w_hi, pad), jnp.pad(w_lo, pad)], axis=1)
    earlier = jnp.tril(jnp.ones((tm, tm), BF16), -1)
    return pl.pallas_call(
        _router_kernel,
        grid=(n_tok // tm,),
        in_specs=_attn_residual_specs(x, tm) + [
            _resident((1, D_MODEL), lambda i: (0, 0)),
            _resident((D_MODEL, 2 * LANES), lambda i: (0, 0)),
            _resident((tm, tm), lambda i: (0, 0)),
        ],
        out_specs=[
            pl.BlockSpec((tm, D_MODEL), lambda i: (i, 0)),
            pl.BlockSpec((META_ROWS, tm), lambda i: (0, i)),
            pl.BlockSpec((tm, LANES), lambda i: (i, 0)),
            pl.BlockSpec((1, LANES), lambda i: (0, 0)),
        ],
        out_shape=[
            jax.ShapeDtypeStruct((n_tok, D_MODEL), F32),
            jax.ShapeDtypeStruct((META_ROWS, n_tok), jnp.int32),
            jax.ShapeDtypeStruct((n_tok, LANES), F32),
            jax.ShapeDtypeStruct((1, LANES), F32),
        ],
        scratch_shapes=[pltpu.VMEM((1, LANES), F32)],
        compiler_params=_params("arbitrary"),
        name="router",
    )(o, w_o, x, g, wr, earlier)


def _moe_plan(meta, counts):
    tm = MOE_ROW_TILE
    n_tok = meta.shape[1]
    n_tiles = (2 * n_tok) // tm + N_EXPERTS
    cnt = counts[0, :N_EXPERTS].astype(jnp.int32)
    tiles = (cnt + tm - 1) // tm
    tile_end = jnp.cumsum(tiles)
    offsets = (tile_end - tiles) * tm
    tile_ids = jnp.arange(n_tiles, dtype=jnp.int32)
    tile_expert = jnp.minimum(jnp.sum(tile_ids[:, None] >= tile_end[None, :], axis=1), N_EXPERTS - 1)
    used = tile_end[-1:].astype(jnp.int32)
    experts = jnp.arange(N_EXPERTS, dtype=jnp.int32)

    def slot(expert, rank):
        first = jnp.sum(jnp.where(expert[:, None] == experts[None, :], offsets[None, :], 0), axis=1)
        return first + rank

    pos1 = slot(meta[0], meta[2])
    pos2 = slot(meta[1], meta[3])
    pos = jnp.stack([pos1.reshape(-1, ROW_TILE), pos2.reshape(-1, ROW_TILE)], axis=1).reshape(-1)
    pad_start = (offsets + cnt).astype(jnp.int32)
    pad_count = (tiles * tm - cnt).astype(jnp.int32)
    return (pos.astype(jnp.int32), tile_expert.astype(jnp.int32), used, pad_start, pad_count, n_tiles * tm)


def _load_positions(pos_hbm, pos_smem, pos_sem, tile):
    n = pos_smem.shape[0]
    start = pl.multiple_of(tile * n, n)
    return pltpu.make_async_copy(pos_hbm.at[pl.ds(start, n)], pos_smem, pos_sem)


def _dispatch_kernel(pad_start_ref, pad_count_ref, used_ref, pos_hbm, x_ref, g_ref, xs_out, h_ref, zero_ref,
                     pos_smem, sems, pos_sem, pad_sem):
    i = pl.program_id(0)
    tm = h_ref.shape[1]
    buf = i % 2
    pos_copy = _load_positions(pos_hbm, pos_smem, pos_sem, i)
    pos_copy.start()
    h_ref[buf] = _rms(x_ref[...], g_ref[...])
    pos_copy.wait()

    @pl.when(i == 0)
    def _():
        zero_ref[...] = jnp.zeros_like(zero_ref)

        def zero_copy(slot):
            return pltpu.make_async_copy(zero_ref.at[pl.ds(0, 1)], xs_out.at[pl.ds(slot, 1)], pad_sem)

        for e in range(N_EXPERTS):
            def fill(r, carry):
                zero_copy(pad_start_ref[e] + r).start()
                return carry

            lax.fori_loop(0, pad_count_ref[e], fill, 0)

        tail_start = used_ref[0] * MOE_ROW_TILE
        n_tail = (xs_out.shape[0] - tail_start) // tm

        def tail_copy(t):
            start = pl.multiple_of(tail_start + t * tm, tm)
            return pltpu.make_async_copy(zero_ref, xs_out.at[pl.ds(start, tm)], pad_sem)

        def fill_tail(t, carry):
            tail_copy(t).start()
            return carry

        lax.fori_loop(0, n_tail, fill_tail, 0)
        for e in range(N_EXPERTS):
            def fill_done(r, carry):
                zero_copy(pad_start_ref[e] + r).wait()
                return carry

            lax.fori_loop(0, pad_count_ref[e], fill_done, 0)

        def tail_done(t, carry):
            tail_copy(t).wait()
            return carry

        lax.fori_loop(0, n_tail, tail_done, 0)

    def issue(r, carry):
        for k in range(2):
            pltpu.make_async_copy(h_ref.at[buf, pl.ds(r, 1)], xs_out.at[pl.ds(pos_smem[k * tm + r], 1)],
                                  sems.at[buf]).start()
        return carry

    lax.fori_loop(0, tm, issue, 0, unroll=8)

    def drain(b):
        for _ in range(2):
            pltpu.make_async_copy(h_ref.at[b], xs_out.at[pl.ds(0, tm)], sems.at[b]).wait()

    @pl.when(i > 0)
    def _():
        drain(1 - buf)

    @pl.when(i == pl.num_programs(0) - 1)
    def _():
        drain(buf)


def _dispatch(pad_start, pad_count, used, pos, x, g, n_slots):
    n_tok = x.shape[0]
    tm = ROW_TILE
    grid_spec = pltpu.PrefetchScalarGridSpec(
        num_scalar_prefetch=3,
        grid=(n_tok // tm,),
        in_specs=[
            pl.BlockSpec(memory_space=pl.ANY),
            pl.BlockSpec((tm, D_MODEL), lambda i, ps, pc, u: (i, 0)),
            _resident((1, D_MODEL), lambda i, ps, pc, u: (0, 0)),
        ],
        out_specs=pl.BlockSpec(memory_space=pl.ANY),
        scratch_shapes=[
            pltpu.VMEM((2, tm, D_MODEL), F32),
            pltpu.VMEM((tm, D_MODEL), F32),
            pltpu.SMEM((2 * tm,), jnp.int32),
            pltpu.SemaphoreType.DMA((2,)),
            pltpu.SemaphoreType.DMA(()),
            pltpu.SemaphoreType.DMA(()),
        ],
    )
    return pl.pallas_call(
        _dispatch_kernel,
        grid_spec=grid_spec,
        out_shape=jax.ShapeDtypeStruct((n_slots, D_MODEL), F32),
        compiler_params=_params("arbitrary"),
        name="moe_dispatch",
    )(pad_start, pad_count, used, pos, x, g)


def _expert_kernel(te_ref, used_ref, xs_ref, wg_ref, wu_ref, wo_ref, y_ref, hb_ref):
    del te_ref
    c = pl.program_id(1)
    in_use = pl.program_id(0) < used_ref[0]

    @pl.when(jnp.logical_not(in_use) & (c == 0))
    def _():
        y_ref[...] = jnp.zeros_like(y_ref)

    @pl.when(in_use)
    def _():
        @pl.when(c == 0)
        def _():
            hb_ref[...] = xs_ref[...].astype(BF16)
            y_ref[...] = jnp.zeros_like(y_ref)

        h = hb_ref[...]
        y = y_ref[...]
        for c0, c1 in _ff_chunks(wg_ref.shape[3], MOE_INNER_CHUNK):
            gate = jnp.dot(h, wg_ref[0, 0, :, c0:c1], preferred_element_type=F32)
            up = jnp.dot(h, wu_ref[0, 0, :, c0:c1], preferred_element_type=F32)
            a = (gate * jax.nn.sigmoid(gate) * up).astype(BF16)
            y = y + jnp.dot(a, wo_ref[0, 0, c0:c1, :], preferred_element_type=F32)
        y_ref[...] = y


def _experts(tile_expert, used, xs, w_in, w_out, layer):
    tm = MOE_ROW_TILE
    ck = MOE_FF_CHUNK
    n_ck = D_FF_EXPERT // ck
    n_slots = xs.shape[0]

    def chunk(i, c, te, used):
        return jnp.where(i < used[0], c, n_ck - 1)

    grid_spec = pltpu.PrefetchScalarGridSpec(
        num_scalar_prefetch=2,
        grid=(n_slots // tm, n_ck),
        in_specs=[
            pl.BlockSpec((tm, D_MODEL), lambda i, c, te, used: (jnp.minimum(i, used[0] - 1), 0)),
            pl.BlockSpec((1, 1, D_MODEL, ck), lambda i, c, te, used: (layer, te[i], 0, chunk(i, c, te, used))),
            pl.BlockSpec((1, 1, D_MODEL, ck),
                         lambda i, c, te, used: (layer, te[i], 0, n_ck + chunk(i, c, te, used))),
            pl.BlockSpec((1, 1, ck, D_MODEL), lambda i, c, te, used: (layer, te[i], chunk(i, c, te, used), 0)),
        ],
        out_specs=pl.BlockSpec((tm, D_MODEL), lambda i, c, te, used: (i, 0)),
        scratch_shapes=[pltpu.VMEM((tm, D_MODEL), BF16)],
    )
    return pl.pallas_call(
        _expert_kernel,
        grid_spec=grid_spec,
        out_shape=jax.ShapeDtypeStruct((n_slots, D_MODEL), F32),
        compiler_params=_params("arbitrary", "arbitrary"),
        name="moe_experts",
    )(tile_expert, used, xs, w_in, w_in, w_out)


def _combine_kernel(pos_hbm, y_hbm, x_ref, gate_ref, gf_ref, *rest, split):
    out_refs = rest[:1 if split is None else 2]
    rows_ref, pos_smem, sems, pos_sem = rest[len(out_refs):]
    i = pl.program_id(0)
    n_tiles = pl.num_programs(0)
    tm = x_ref.shape[0]

    def gather(tile, buf):
        pos_copy = _load_positions(pos_hbm, pos_smem, pos_sem, tile)
        pos_copy.start()
        pos_copy.wait()

        def issue(r, carry):
            for k in range(2):
                pltpu.make_async_copy(y_hbm.at[pl.ds(pos_smem[k * tm + r], 1)],
                                      rows_ref.at[buf, k, pl.ds(r, 1)], sems.at[buf]).start()
            return carry

        lax.fori_loop(0, tm, issue, 0, unroll=8)

    @pl.when(i == 0)
    def _():
        gather(0, 0)

    @pl.when(i + 1 < n_tiles)
    def _():
        gather(i + 1, (i + 1) % 2)

    buf = i % 2
    for k in range(2):
        pltpu.make_async_copy(y_hbm.at[pl.ds(0, tm)], rows_ref.at[buf, k], sems.at[buf]).wait()
    gates = gate_ref[...]
    out = x_ref[...] + (gates[:, 0:1] * rows_ref[buf, 0] + gates[:, 1:2] * rows_ref[buf, 1])
    if split is None:
        out_refs[0][...] = out
    else:
        out = _rms(out, gf_ref[...])

        @pl.when(i < split)
        def _():
            out_refs[0][...] = out

        @pl.when(i >= split)
        def _():
            out_refs[1][...] = out


def _combine(pos, y, x, gates, g_final, *, split):
    n_tok = x.shape[0]
    tm = ROW_TILE
    n_tiles = n_tok // tm
    if split is None:
        out_specs = [pl.BlockSpec((tm, D_MODEL), lambda i: (i, 0))]
        out_shape = [jax.ShapeDtypeStruct((n_tok, D_MODEL), F32)]
    else:
        out_specs = [pl.BlockSpec((tm, D_MODEL), lambda i: (jnp.minimum(i, split - 1), 0)),
                     pl.BlockSpec((tm, D_MODEL), lambda i: (jnp.maximum(i - split, 0), 0))]
        out_shape = [jax.ShapeDtypeStruct((split * tm, D_MODEL), F32),
                     jax.ShapeDtypeStruct(((n_tiles - split) * tm, D_MODEL), F32)]
    return pl.pallas_call(
        functools.partial(_combine_kernel, split=split),
        grid=(n_tiles,),
        in_specs=[
            pl.BlockSpec(memory_space=pl.ANY),
            pl.BlockSpec(memory_space=pl.ANY),
            pl.BlockSpec((tm, D_MODEL), lambda i: (i, 0)),
            pl.BlockSpec((tm, LANES), lambda i: (i, 0)),
            _resident((1, D_MODEL), lambda i: (0, 0)),
        ],
        out_specs=out_specs,
        out_shape=out_shape,
        scratch_shapes=[
            pltpu.VMEM((2, 2, tm, D_MODEL), F32),
            pltpu.SMEM((2 * tm,), jnp.int32),
            pltpu.SemaphoreType.DMA((2,)),
            pltpu.SemaphoreType.DMA(()),
        ],
        compiler_params=_params("arbitrary"),
        name="moe_combine",
    )(pos, y, x, gates, g_final)


def _moe(o, w_o, x, g, w_router, w_in, w_out, layer, g_final, *, split):
    x, meta, gates, counts = _router(o, w_o, x, g, w_router)
    pos, tile_expert, used, pad_start, pad_count, n_slots = _moe_plan(meta, counts)
    xs = _dispatch(pad_start, pad_count, used, pos, x, g, n_slots)
    y = _experts(tile_expert, used, xs, w_in, w_out, layer)
    return _combine(pos, y, x, gates, g_final, split=split)


def _rope_tables():
    t = jnp.arange(SEQ)
    row = (t // GRID_W).astype(F32)
    col = (t % GRID_W).astype(F32)
    half = HEAD_DIM // 2
    inv_freq = ROPE_THETA ** (-jnp.arange(0, half, 2, dtype=F32) / half)
    ang = jnp.concatenate([row[:, None] * inv_freq[None, :], col[:, None] * inv_freq[None, :]], axis=-1)
    cos = jnp.repeat(jnp.cos(ang), 2, axis=-1)
    sign = jnp.where(jnp.arange(HEAD_DIM) % 2 == 0, -1.0, 1.0).astype(F32)
    sin = jnp.repeat(jnp.sin(ang), 2, axis=-1) * sign
    reps = LANES // HEAD_DIM
    return jnp.tile(cos, (1, reps)), jnp.tile(sin, (1, reps))


def _t5_bucket(rel):
    half = N_BUCKETS // 2
    ret = jnp.where(rel > 0, half, 0)
    n = jnp.abs(rel)
    max_exact = half // 2
    nf = jnp.maximum(n, 1).astype(F32)
    large = max_exact + (jnp.log(nf / max_exact) / math.log(MAX_DISTANCE / max_exact)
                         * (half - max_exact)).astype(jnp.int32)
    large = jnp.minimum(large, half - 1)
    return ret + jnp.where(n < max_exact, n, large)


def _window_bias(rel_bias):
    r = jnp.arange(Q_BLOCK)[:, None]
    c = jnp.arange(Q_BLOCK + 2 * WINDOW)[None, :]
    rel = c - WINDOW - r
    bucket = _t5_bucket(rel)[:, :, None]
    bias = jnp.zeros(rel.shape + (N_Q_HEADS,), F32)
    for b in range(N_BUCKETS):
        bias = jnp.where(bucket == b, rel_bias[b].astype(F32)[None, None, :], bias)
    bias = bias * LOG2E
    bias = jnp.where((jnp.abs(rel) <= WINDOW)[:, :, None], bias, NEG_INF)
    bias = bias.reshape(Q_BLOCK, Q_BLOCK + 2 * WINDOW, N_KV_HEADS, GROUP).transpose(2, 1, 3, 0)
    bias = bias.reshape(N_KV_HEADS, Q_BLOCK + 2 * WINDOW, GROUP * Q_BLOCK)
    key = jnp.arange(Q_BLOCK + 2 * WINDOW)[None, :, None]
    no_left = jnp.where(key < WINDOW, NEG_INF, bias)
    no_right = jnp.where(key >= Q_BLOCK + WINDOW, NEG_INF, bias)
    return jnp.stack([bias, no_left, no_right])


def _swap_pairs(a):
    return a.reshape(a.shape[:-1] + (a.shape[-1] // 2, 2))[..., ::-1].reshape(a.shape)


def _axial_tables(cos, sin, q_gain, k_gain):
    def pair(gain, scale):
        g = jnp.tile(gain.astype(F32), LANES // HEAD_DIM)[None, :]
        return cos * (g * scale), sin * (_swap_pairs(g) * scale)

    return pair(q_gain, Q_SCALE) + pair(k_gain, 1.0)


def _axial_qkv_weights(w):
    return jnp.concatenate([w, _swap_pairs(w[:, :QK_WIDTH])], axis=1).astype(BF16)


def kernel(x_prompt, x_sample, norm_mix, norm_ffn, norm_final, w_qkv_a, q_gain_a, k_gain_a, w_o_a,
           w_qkv_b, sink_b, w_o_b, rel_bias, w_ff_in, w_ff_out, w_router, w_exp_in, w_exp_out):
    n_prompt = x_prompt.shape[0] * x_prompt.shape[1]
    x = (x_prompt.reshape(-1, D_MODEL), x_sample.reshape(-1, D_MODEL))
    cos, sin = _rope_tables()
    bias = _window_bias(rel_bias)
    w_exp_in_bf16 = w_exp_in.astype(BF16)
    w_exp_out_bf16 = w_exp_out.astype(BF16)
    for i in range(DEPTH):
        j = i // 2
        g_mix = norm_mix[i][None, :]
        g_ffn = norm_ffn[i][None, :]
        if i % 2 == 0:
            q, k, vt = _qkv_proj(x, g_mix, _axial_qkv_weights(w_qkv_a[j]),
                                 _axial_tables(cos, sin, q_gain_a[j], k_gain_a[j]))
            o = _global_attention(q, k, vt)
            x = _dense_ffn(o, w_o_a[j].astype(BF16), x, g_ffn, w_ff_in[j].astype(BF16), w_ff_out[j].astype(BF16))
        else:
            q, k, vt = _qkv_proj(x, g_mix, w_qkv_b[j].astype(BF16))
            o = _window_attention(q, k, vt, sink_b[j].astype(F32) * LOG2E, bias)
            last = i == DEPTH - 1
            x = _moe(o, w_o_b[j].astype(BF16), x, g_ffn, w_router[j], w_exp_in_bf16, w_exp_out_bf16, j,
                     norm_final[None, :],
                     split=n_prompt // ROW_TILE if last else None)
            if not last:
                x = x[0]
    y_prompt, y_sample = x
    return (y_prompt.reshape(x_prompt.shape), y_sample.reshape(x_sample.shape))
```

```python
import functools
import math

import jax
import jax.numpy as jnp
from jax import lax
from jax.experimental import pallas as pl
from jax.experimental.pallas import tpu as pltpu

D_MODEL = 1024
SEQ = 4096
DEPTH = 4
HEAD_DIM = 64
N_Q_HEADS = 16
N_KV_HEADS = 4
GROUP = N_Q_HEADS // N_KV_HEADS
Q_WIDTH = N_Q_HEADS * HEAD_DIM
KV_WIDTH = N_KV_HEADS * HEAD_DIM
QK_WIDTH = Q_WIDTH + KV_WIDTH
QKV_WIDTH = Q_WIDTH + 2 * KV_WIDTH
GRID_W = 64
ROPE_THETA = 10000.0
Q_BLOCK = 128
WINDOW = 128
N_BUCKETS = 32
MAX_DISTANCE = 128
D_FF = 2816
N_EXPERTS = 8
D_FF_EXPERT = 3584
EPS = 1e-6
NEG_INF = -1e30

LANES = 128
VMEM_LIMIT = 56 * 1024 * 1024

ROW_TILE = 512
WINDOW_Q_TILE = 512
WINDOW_LOOKAHEAD = 1
SAFE_SHIFT = 55.0
BOUND_SLACK = 1.0 + 2.0 ** -10
GLOBAL_KV_ROW_TILES = 1
SCORE_LOOKAHEAD = 1
GLOBAL_Q_TILE = 256
MOE_ROW_TILE = 1024
MOE_FF_CHUNK = 1792
MOE_INNER_CHUNK = 256
DENSE_FF_CHUNK = 1024
META_ROWS = 8

K_PAD_WIDTH = N_KV_HEADS * LANES
V_ONES = 16
LOG2E = math.log2(math.e)
Q_SCALE = HEAD_DIM ** -0.5 * LOG2E

F32 = jnp.float32
BF16 = jnp.bfloat16


def _params(*sem):
    return pltpu.CompilerParams(dimension_semantics=sem, vmem_limit_bytes=VMEM_LIMIT)


def _resident(shape, index_map):
    return pl.BlockSpec(shape, index_map, pipeline_mode=pl.Buffered(1))


def _rms(x, g):
    ms = jnp.mean(x * x, axis=-1, keepdims=True)
    return x * lax.rsqrt(ms + EPS) * g


def _store_k_v(k, v, k_ref, vt_ref):
    pad = jnp.zeros((k.shape[0], LANES - HEAD_DIM), BF16)
    for hd in range(N_KV_HEADS):
        k_ref[:, hd * LANES:hd * LANES + HEAD_DIM] = k[:, hd * HEAD_DIM:(hd + 1) * HEAD_DIM].astype(BF16)
        k_ref[:, hd * LANES + HEAD_DIM:(hd + 1) * LANES] = pad
    vt_ref[0, 0] = v.astype(BF16).T


def _qkv_window_kernel(x_ref, g_ref, w_ref, q_ref, k_ref, vt_ref):
    h = _rms(x_ref[...], g_ref[...]).astype(BF16)
    acc = jnp.dot(h, w_ref[...], preferred_element_type=F32)
    q_ref[...] = (acc[:, :Q_WIDTH] * Q_SCALE).astype(BF16)
    _store_k_v(acc[:, Q_WIDTH:QK_WIDTH], acc[:, QK_WIDTH:], k_ref, vt_ref)


def _x_specs(x, tm):
    if not isinstance(x, tuple):
        return [pl.BlockSpec((tm, D_MODEL), lambda i: (i, 0))]
    split = x[0].shape[0] // tm
    return [pl.BlockSpec((tm, D_MODEL), lambda i: (jnp.minimum(i, split - 1), 0)),
            pl.BlockSpec((tm, D_MODEL), lambda i: (jnp.maximum(i - split, 0), 0))]


def _x_split(x, tm):
    return x[0].shape[0] // tm if isinstance(x, tuple) else None


def _take_x(refs, split):
    if split is None:
        return refs[0][...], refs[1:]
    return jnp.where(pl.program_id(0) < split, refs[0][...], refs[1][...]), refs[2:]


def _qkv_axial_kernel(*refs, split):
    x, (g_ref, w_ref, cq_ref, sq_ref, ck_ref, sk_ref, q_ref, k_ref, vt_ref) = _take_x(refs, split)
    h = _rms(x, g_ref[...]).astype(BF16)
    acc = jnp.dot(h, w_ref[...], preferred_element_type=F32)
    low = lax.broadcasted_iota(jnp.int32, (acc.shape[0], LANES), 1) < HEAD_DIM

    def norm_rope(col, c, s):
        seg = acc[:, col:col + LANES]
        partner = acc[:, QKV_WIDTH + col:QKV_WIDTH + col + LANES]
        sq = seg * seg
        s_all = jnp.sum(sq, axis=-1, keepdims=True)
        s_lo = jnp.sum(jnp.where(low, sq, 0.0), axis=-1, keepdims=True)
        ms = jnp.where(low, s_lo, s_all - s_lo) * (1.0 / HEAD_DIM)
        return (seg * c + partner * s) * lax.rsqrt(ms + EPS)

    cq, sq_, ck, sk = cq_ref[...], sq_ref[...], ck_ref[...], sk_ref[...]
    for j in range(Q_WIDTH // LANES):
        q_ref[:, j * LANES:(j + 1) * LANES] = norm_rope(j * LANES, cq, sq_).astype(BF16)
    k = jnp.concatenate([norm_rope(Q_WIDTH + j * LANES, ck, sk) for j in range(KV_WIDTH // LANES)], axis=1)
    _store_k_v(k, acc[:, QK_WIDTH:QKV_WIDTH], k_ref, vt_ref)


def _qkv_proj(x, g, w, tables=None):
    xs = x if isinstance(x, tuple) else (x,)
    n_tok = sum(a.shape[0] for a in xs)
    tm = ROW_TILE
    tiles_per_seq = SEQ // tm
    n_seq = n_tok // SEQ
    axial = tables is not None
    tables = tuple(tables) if axial else ()
    body = functools.partial(_qkv_axial_kernel, split=_x_split(x, tm)) if axial else _qkv_window_kernel
    return pl.pallas_call(
        body,
        grid=(n_tok // tm,),
        in_specs=_x_specs(x, tm) + [
            _resident((1, D_MODEL), lambda i: (0, 0)),
            _resident(w.shape, lambda i: (0, 0)),
        ] + [pl.BlockSpec((tm, LANES), lambda i: (i % tiles_per_seq, 0)) for _ in tables],
        out_specs=[
            pl.BlockSpec((tm, Q_WIDTH), lambda i: (i, 0)),
            pl.BlockSpec((tm, K_PAD_WIDTH), lambda i: (i, 0)),
            pl.BlockSpec((1, 1, KV_WIDTH, tm), lambda i: (i // tiles_per_seq, i % tiles_per_seq, 0, 0)),
        ],
        out_shape=[
            jax.ShapeDtypeStruct((n_tok, Q_WIDTH), BF16),
            jax.ShapeDtypeStruct((n_tok, K_PAD_WIDTH), BF16),
            jax.ShapeDtypeStruct((n_seq, tiles_per_seq, KV_WIDTH, tm), BF16),
        ],
        compiler_params=_params("parallel"),
        name="qkv_axial" if axial else "qkv_window",
    )(*xs, g, w, *tables)


def _q_transposed(q_ref):
    return q_ref[...].T


def _group_queries(qt, h):
    base = h * GROUP * HEAD_DIM
    return jnp.concatenate(
        [qt[base + g * HEAD_DIM:base + (g + 1) * HEAD_DIM, :] for g in range(GROUP)], axis=1)


def _v_with_ones(vt_h, ones):
    return jnp.concatenate([vt_h, ones], axis=0)


def _store_heads(o_ref, outs, tq):
    rows = [o[:, g * tq:(g + 1) * tq] for o in outs for g in range(GROUP)]
    o_ref[...] = jnp.concatenate(rows, axis=0).astype(o_ref.dtype).T


def _global_attn_kernel(q_ref, k_ref, vt_ref, o_ref, s_ref, kmax_ref):
    tq = q_ref.shape[0]
    n_kv = vt_ref.shape[1] // GLOBAL_KV_ROW_TILES
    tk = vt_ref.shape[3] * GLOBAL_KV_ROW_TILES
    cols = GROUP * tq
    qt = _q_transposed(q_ref)
    queries = [_group_queries(qt, h) for h in range(N_KV_HEADS)]
    ones = jnp.ones((V_ONES, tk), BF16)

    @pl.when(pl.program_id(1) == 0)
    def _():
        def block_max(c, carry):
            rows = k_ref[pl.ds(pl.multiple_of(c * tk, tk), tk), :].astype(F32)
            sq = rows * rows
            return tuple(
                jnp.maximum(carry[h], jnp.max(jnp.sum(sq[:, h * LANES:(h + 1) * LANES], axis=1, keepdims=True),
                                              axis=0, keepdims=True))
                for h in range(N_KV_HEADS))

        k2max = lax.fori_loop(0, k_ref.shape[0] // tk, block_max,
                              tuple(jnp.zeros((1, 1), F32) for _ in range(N_KV_HEADS)))
        for h in range(N_KV_HEADS):
            kmax_ref[h] = jnp.broadcast_to(jnp.sqrt(k2max[h]), kmax_ref.shape[1:])

    qf = qt.astype(F32)

    def logit_bound(h):
        norms = []
        for g in range(GROUP):
            r0 = (h * GROUP + g) * HEAD_DIM
            blk = qf[r0:r0 + HEAD_DIM, :]
            norms.append(jnp.sqrt(jnp.sum(blk * blk, axis=0, keepdims=True)))
        return jnp.concatenate(norms, axis=1) * (kmax_ref[h][0:1, 0:1] * BOUND_SLACK)

    bounds = [logit_bound(h) for h in range(N_KV_HEADS)]
    bound_is_safe = jnp.max(jnp.concatenate(bounds, axis=0)) <= SAFE_SHIFT

    def scores(h, j):
        start = pl.multiple_of(j * tk, tk)
        k_h = k_ref[pl.ds(start, tk), h * LANES:h * LANES + HEAD_DIM]
        return jnp.dot(k_h, queries[h], preferred_element_type=F32)

    def values(h, j):
        vt_h = jnp.concatenate([vt_ref[0, j * GLOBAL_KV_ROW_TILES + t, h * HEAD_DIM:(h + 1) * HEAD_DIM, :]
                                for t in range(GLOBAL_KV_ROW_TILES)], axis=1)
        return _v_with_ones(vt_h, ones)

    def update_fixed(h, j, s, m, acc):
        p = jnp.exp2(s - m).astype(BF16)
        return m, acc + jnp.dot(values(h, j), p, preferred_element_type=F32)

    def update_online(h, j, s, m, acc):
        m_new = jnp.maximum(m, jnp.max(s, axis=0, keepdims=True))
        p = jnp.exp2(s - m_new).astype(BF16)
        alpha = jnp.exp2(m - m_new)
        return m_new, alpha * acc + jnp.dot(values(h, j), p, preferred_element_type=F32)

    ahead = s_ref.shape[0]
    for n in range(ahead):
        s_ref[n] = scores(n, 0)

    def attend(update, shifts):
        def body(j, carry):
            new = []
            in_flight = [s_ref[n] for n in range(ahead)]
            for h in range(N_KV_HEADS):
                h_next = h + ahead
                if h_next < N_KV_HEADS:
                    in_flight.append(scores(h_next, j))
                else:
                    in_flight.append(scores(h_next - N_KV_HEADS, jnp.minimum(j + 1, n_kv - 1)))
                new.append(update(h, j, in_flight.pop(0), *carry[h]))
            for n in range(ahead):
                s_ref[n] = in_flight[n]
            return tuple(new)

        init = tuple((shifts[h], jnp.zeros((HEAD_DIM + V_ONES, cols), F32)) for h in range(N_KV_HEADS))
        final = lax.fori_loop(0, n_kv, body, init, unroll=4)
        outs = [acc[:HEAD_DIM] / acc[HEAD_DIM:HEAD_DIM + 1] for _, acc in final]
        _store_heads(o_ref, outs, tq)

    @pl.when(bound_is_safe)
    def _():
        attend(update_fixed, bounds)

    @pl.when(jnp.logical_not(bound_is_safe))
    def _():
        attend(update_online, [jnp.full((1, cols), -jnp.inf, F32)] * N_KV_HEADS)


def _global_attention(q, k, vt):
    n_tok = q.shape[0]
    n_seq = n_tok // SEQ
    tq = GLOBAL_Q_TILE
    nq = SEQ // tq
    return pl.pallas_call(
        _global_attn_kernel,
        grid=(n_seq, nq),
        in_specs=[
            pl.BlockSpec((tq, Q_WIDTH), lambda b, i: (b * nq + i, 0)),
            pl.BlockSpec((SEQ, K_PAD_WIDTH), lambda b, i: (b, 0)),
            pl.BlockSpec((1,) + vt.shape[1:], lambda b, i: (b, 0, 0, 0)),
        ],
        out_specs=pl.BlockSpec((tq, Q_WIDTH), lambda b, i: (b * nq + i, 0)),
        out_shape=jax.ShapeDtypeStruct((n_tok, Q_WIDTH), BF16),
        scratch_shapes=[pltpu.VMEM((SCORE_LOOKAHEAD, GLOBAL_KV_ROW_TILES * vt.shape[3], GROUP * tq), F32),
                        pltpu.VMEM((N_KV_HEADS, 8, LANES), F32)],
        compiler_params=_params("parallel", "arbitrary"),
        name="global_attention",
    )(q, k, vt)


def _window_attn_kernel(sink_ref, q_ref, kl_ref, kc_ref, kr_ref, vtl_ref, vtc_ref, vtr_ref, bias_ref, o_ref):
    i = pl.program_id(1)
    qb = Q_BLOCK
    n_sub = q_ref.shape[0] // qb
    n_keys = 3 * qb
    k_all = jnp.concatenate([kl_ref[...], kc_ref[...], kr_ref[...]], axis=0)
    vt_all = jnp.concatenate([vtl_ref[0, 0], vtc_ref[0, 0], vtr_ref[0, 0]], axis=1)
    qt = _q_transposed(q_ref)
    ones = jnp.ones((V_ONES, n_keys), BF16)
    variants = [jnp.where(i == 0, 1, 0)] + [0] * (n_sub - 2) + [jnp.where(i == pl.num_programs(1) - 1, 2, 0)]
    units = [(u, h) for u in range(n_sub) for h in range(N_KV_HEADS)]

    def scores(u, h):
        keys = k_all[u * qb:u * qb + n_keys, h * LANES:h * LANES + HEAD_DIM]
        return jnp.dot(keys, _group_queries(qt[:, u * qb:(u + 1) * qb], h), preferred_element_type=F32)

    outs = []
    in_flight = [scores(*unit) for unit in units[:WINDOW_LOOKAHEAD]]
    for n, (u, h) in enumerate(units):
        if n + WINDOW_LOOKAHEAD < len(units):
            in_flight.append(scores(*units[n + WINDOW_LOOKAHEAD]))
        s = in_flight.pop(0) + bias_ref[variants[u], h]
        sink = jnp.concatenate(
            [jnp.full((1, qb), sink_ref[h * GROUP + g], F32) for g in range(GROUP)], axis=1)
        m = jnp.maximum(jnp.max(s, axis=0, keepdims=True), sink)
        p = jnp.exp2(s - m).astype(BF16)
        v_h = _v_with_ones(vt_all[h * HEAD_DIM:(h + 1) * HEAD_DIM, u * qb:u * qb + n_keys], ones)
        acc = jnp.dot(v_h, p, preferred_element_type=F32)
        outs.append(acc[:HEAD_DIM] / (acc[HEAD_DIM:HEAD_DIM + 1] + jnp.exp2(sink - m)))
    blocks = []
    for u in range(n_sub):
        heads = outs[u * N_KV_HEADS:(u + 1) * N_KV_HEADS]
        blocks.append(jnp.concatenate([o[:, g * qb:(g + 1) * qb] for o in heads for g in range(GROUP)], axis=0))
    o_ref[...] = jnp.concatenate(blocks, axis=1).astype(o_ref.dtype).T


def _window_attention(q, k, vt, sinks, bias):
    n_tok = q.shape[0]
    n_seq = n_tok // SEQ
    qb = Q_BLOCK
    tq = WINDOW_Q_TILE
    nq = SEQ // tq
    nb = SEQ // qb
    sub = tq // qb
    assert sub >= 2, "the first and last query block of a step take different bias variants"

    def edge_block(b, i, off):
        return jnp.clip(i * sub + off, 0, nb - 1)

    def k_edge(off):
        return pl.BlockSpec((qb, K_PAD_WIDTH), lambda b, i, s: (b * nb + edge_block(b, i, off), 0))

    def vt_edge(off):
        per = ROW_TILE // qb
        return pl.BlockSpec((1, 1, KV_WIDTH, qb),
                            lambda b, i, s: (b, edge_block(b, i, off) // per, 0, edge_block(b, i, off) % per))

    per_tile = ROW_TILE // tq
    grid_spec = pltpu.PrefetchScalarGridSpec(
        num_scalar_prefetch=1,
        grid=(n_seq, nq),
        in_specs=[
            pl.BlockSpec((tq, Q_WIDTH), lambda b, i, s: (b * nq + i, 0)),
            k_edge(-1),
            pl.BlockSpec((tq, K_PAD_WIDTH), lambda b, i, s: (b * nq + i, 0)),
            k_edge(sub),
            vt_edge(-1),
            pl.BlockSpec((1, 1, KV_WIDTH, tq), lambda b, i, s: (b, i // per_tile, 0, i % per_tile)),
            vt_edge(sub),
            _resident(bias.shape, lambda b, i, s: (0, 0, 0, 0)),
        ],
        out_specs=pl.BlockSpec((tq, Q_WIDTH), lambda b, i, s: (b * nq + i, 0)),
    )
    return pl.pallas_call(
        _window_attn_kernel,
        grid_spec=grid_spec,
        out_shape=jax.ShapeDtypeStruct((n_tok, Q_WIDTH), BF16),
        compiler_params=_params("parallel", "arbitrary"),
        name="window_attention",
    )(sinks, q, k, k, k, vt, vt, vt, bias)


def _attn_residual(o_ref, wo_ref, x):
    return x + jnp.dot(o_ref[...], wo_ref[...], preferred_element_type=F32)


def _attn_residual_specs(x, tm):
    return [
        pl.BlockSpec((tm, Q_WIDTH), lambda i: (i, 0)),
        _resident((Q_WIDTH, D_MODEL), lambda i: (0, 0)),
    ] + _x_specs(x, tm)


def _ff_chunks(d_ff, width):
    return [(c, min(c + width, d_ff)) for c in range(0, d_ff, width)]


def _dense_ffn_kernel(o_ref, wo_ref, *refs, split):
    x_in, (g_ref, w_in_ref, w_out_ref, y_ref) = _take_x(refs, split)
    x = _attn_residual(o_ref, wo_ref, x_in)
    h = _rms(x, g_ref[...]).astype(BF16)
    acc = x
    for c0, c1 in _ff_chunks(D_FF, DENSE_FF_CHUNK):
        gate = jnp.dot(h, w_in_ref[:, c0:c1], preferred_element_type=F32)
        up = jnp.dot(h, w_in_ref[:, D_FF + c0:D_FF + c1], preferred_element_type=F32)
        a = (gate * jax.nn.sigmoid(gate) * up).astype(BF16)
        acc = acc + jnp.dot(a, w_out_ref[c0:c1, :], preferred_element_type=F32)
    y_ref[...] = acc


def _dense_ffn(o, w_o, x, g, w_in, w_out):
    xs = x if isinstance(x, tuple) else (x,)
    n_tok = o.shape[0]
    tm = ROW_TILE
    return pl.pallas_call(
        functools.partial(_dense_ffn_kernel, split=_x_split(x, tm)),
        grid=(n_tok // tm,),
        in_specs=_attn_residual_specs(x, tm) + [
            _resident((1, D_MODEL), lambda i: (0, 0)),
            _resident((D_MODEL, 2 * D_FF), lambda i: (0, 0)),
            _resident((D_FF, D_MODEL), lambda i: (0, 0)),
        ],
        out_specs=pl.BlockSpec((tm, D_MODEL), lambda i: (i, 0)),
        out_shape=jax.ShapeDtypeStruct((n_tok, D_MODEL), F32),
        compiler_params=_params("parallel"),
        name="dense_ffn",
    )(o, w_o, *xs, g, w_in, w_out)


def _router_kernel(o_ref, wo_ref, x_ref, g_ref, wr_ref, earlier_ref, xn_ref, meta_ref, gate_ref, count_ref,
                   base_ref):
    @pl.when(pl.program_id(0) == 0)
    def _():
        base_ref[...] = jnp.zeros_like(base_ref)

    x = _attn_residual(o_ref, wo_ref, x_ref[...])
    xn_ref[...] = x
    hf = _rms(x, g_ref[...])
    h_hi = hf.astype(BF16)
    h_lo = (hf - h_hi.astype(F32)).astype(BF16)
    by_hi = jnp.dot(h_hi, wr_ref[...], preferred_element_type=F32)
    by_lo = jnp.dot(h_lo, wr_ref[:, :LANES], preferred_element_type=F32)
    logits = by_hi[:, :LANES] + (by_hi[:, LANES:] + by_lo)
    lane = lax.broadcasted_iota(jnp.int32, logits.shape, 1)
    logits = jnp.where(lane < N_EXPERTS, logits, -jnp.inf)
    m1 = jnp.max(logits, axis=-1, keepdims=True)
    i1 = jnp.min(jnp.where(logits == m1, lane, LANES), axis=-1, keepdims=True)
    rest = jnp.where(lane == i1, -jnp.inf, logits)
    m2 = jnp.max(rest, axis=-1, keepdims=True)
    i2 = jnp.min(jnp.where(rest == m2, lane, LANES), axis=-1, keepdims=True)
    e2 = jnp.exp(m2 - m1)
    g1 = 1.0 / (1.0 + e2)
    g2 = e2 / (1.0 + e2)

    chosen = jnp.where((lane == i1) | (lane == i2), 1.0, 0.0)
    before = jnp.dot(earlier_ref[...], chosen.astype(BF16), preferred_element_type=F32) + base_ref[...]
    r1 = jnp.sum(jnp.where(lane == i1, before, 0.0), axis=-1, keepdims=True)
    r2 = jnp.sum(jnp.where(lane == i2, before, 0.0), axis=-1, keepdims=True)
    meta = jnp.where(lane == 0, i1.astype(F32), jnp.where(lane == 1, i2.astype(F32),
                                                          jnp.where(lane == 2, r1, jnp.where(lane == 3, r2, 0.0))))
    meta_ref[...] = meta.T[:META_ROWS].astype(jnp.int32)
    gate_ref[...] = jnp.where(lane == 0, g1, jnp.where(lane == 1, g2, 0.0))
    base_ref[...] += jnp.sum(chosen, axis=0, keepdims=True)
    count_ref[...] = base_ref[...]


def _router(o, w_o, x, g, w_router):
    n_tok = x.shape[0]
    tm = ROW_TILE
    w_hi = w_router.astype(BF16)
    w_lo = (w_router - w_hi.astype(F32)).astype(BF16)
    pad = ((0, 0), (0, LANES - N_EXPERTS))
    wr = jnp.concatenate([jnp.pad(w_hi, pad), jnp.pad(w_lo, pad)], axis=1)
    earlier = jnp.tril(jnp.ones((tm, tm), BF16), -1)
    return pl.pallas_call(
        _router_kernel,
        grid=(n_tok // tm,),
        in_specs=_attn_residual_specs(x, tm) + [
            _resident((1, D_MODEL), lambda i: (0, 0)),
            _resident((D_MODEL, 2 * LANES), lambda i: (0, 0)),
            _resident((tm, tm), lambda i: (0, 0)),
        ],
        out_specs=[
            pl.BlockSpec((tm, D_MODEL), lambda i: (i, 0)),
            pl.BlockSpec((META_ROWS, tm), lambda i: (0, i)),
            pl.BlockSpec((tm, LANES), lambda i: (i, 0)),
            pl.BlockSpec((1, LANES), lambda i: (0, 0)),
        ],
        out_shape=[
            jax.ShapeDtypeStruct((n_tok, D_MODEL), F32),
            jax.ShapeDtypeStruct((META_ROWS, n_tok), jnp.int32),
            jax.ShapeDtypeStruct((n_tok, LANES), F32),
            jax.ShapeDtypeStruct((1, LANES), F32),
        ],
        scratch_shapes=[pltpu.VMEM((1, LANES), F32)],
        compiler_params=_params("arbitrary"),
        name="router",
    )(o, w_o, x, g, wr, earlier)


def _moe_plan(meta, counts):
    tm = MOE_ROW_TILE
    n_tok = meta.shape[1]
    n_tiles = (2 * n_tok) // tm + N_EXPERTS
    cnt = counts[0, :N_EXPERTS].astype(jnp.int32)
    tiles = (cnt + tm - 1) // tm
    tile_end = jnp.cumsum(tiles)
    offsets = (tile_end - tiles) * tm
    tile_ids = jnp.arange(n_tiles, dtype=jnp.int32)
    tile_expert = jnp.minimum(jnp.sum(tile_ids[:, None] >= tile_end[None, :], axis=1), N_EXPERTS - 1)
    used = tile_end[-1:].astype(jnp.int32)
    experts = jnp.arange(N_EXPERTS, dtype=jnp.int32)

    def slot(expert, rank):
        first = jnp.sum(jnp.where(expert[:, None] == experts[None, :], offsets[None, :], 0), axis=1)
        return first + rank

    pos1 = slot(meta[0], meta[2])
    pos2 = slot(meta[1], meta[3])
    pos = jnp.stack([pos1.reshape(-1, ROW_TILE), pos2.reshape(-1, ROW_TILE)], axis=1).reshape(-1)
    pad_start = (offsets + cnt).astype(jnp.int32)
    pad_count = (tiles * tm - cnt).astype(jnp.int32)
    return (pos.astype(jnp.int32), tile_expert.astype(jnp.int32), used, pad_start, pad_count, n_tiles * tm)


def _load_positions(pos_hbm, pos_smem, pos_sem, tile):
    n = pos_smem.shape[0]
    start = pl.multiple_of(tile * n, n)
    return pltpu.make_async_copy(pos_hbm.at[pl.ds(start, n)], pos_smem, pos_sem)


def _dispatch_kernel(pad_start_ref, pad_count_ref, used_ref, pos_hbm, x_ref, g_ref, xs_out, h_ref, zero_ref,
                     pos_smem, sems, pos_sem, pad_sem):
    i = pl.program_id(0)
    tm = h_ref.shape[1]
    buf = i % 2
    pos_copy = _load_positions(pos_hbm, pos_smem, pos_sem, i)
    pos_copy.start()
    h_ref[buf] = _rms(x_ref[...], g_ref[...])
    pos_copy.wait()

    @pl.when(i == 0)
    def _():
        zero_ref[...] = jnp.zeros_like(zero_ref)

        def zero_copy(slot):
            return pltpu.make_async_copy(zero_ref.at[pl.ds(0, 1)], xs_out.at[pl.ds(slot, 1)], pad_sem)

        for e in range(N_EXPERTS):
            def fill(r, carry):
                zero_copy(pad_start_ref[e] + r).start()
                return carry

            lax.fori_loop(0, pad_count_ref[e], fill, 0)

        tail_start = used_ref[0] * MOE_ROW_TILE
        n_tail = (xs_out.shape[0] - tail_start) // tm

        def tail_copy(t):
            start = pl.multiple_of(tail_start + t * tm, tm)
            return pltpu.make_async_copy(zero_ref, xs_out.at[pl.ds(start, tm)], pad_sem)

        def fill_tail(t, carry):
            tail_copy(t).start()
            return carry

        lax.fori_loop(0, n_tail, fill_tail, 0)
        for e in range(N_EXPERTS):
            def fill_done(r, carry):
                zero_copy(pad_start_ref[e] + r).wait()
                return carry

            lax.fori_loop(0, pad_count_ref[e], fill_done, 0)

        def tail_done(t, carry):
            tail_copy(t).wait()
            return carry

        lax.fori_loop(0, n_tail, tail_done, 0)

    def issue(r, carry):
        for k in range(2):
            pltpu.make_async_copy(h_ref.at[buf, pl.ds(r, 1)], xs_out.at[pl.ds(pos_smem[k * tm + r], 1)],
                                  sems.at[buf]).start()
        return carry

    lax.fori_loop(0, tm, issue, 0, unroll=8)

    def drain(b):
        for _ in range(2):
            pltpu.make_async_copy(h_ref.at[b], xs_out.at[pl.ds(0, tm)], sems.at[b]).wait()

    @pl.when(i > 0)
    def _():
        drain(1 - buf)

    @pl.when(i == pl.num_programs(0) - 1)
    def _():
        drain(buf)


def _dispatch(pad_start, pad_count, used, pos, x, g, n_slots):
    n_tok = x.shape[0]
    tm = ROW_TILE
    grid_spec = pltpu.PrefetchScalarGridSpec(
        num_scalar_prefetch=3,
        grid=(n_tok // tm,),
        in_specs=[
            pl.BlockSpec(memory_space=pl.ANY),
            pl.BlockSpec((tm, D_MODEL), lambda i, ps, pc, u: (i, 0)),
            _resident((1, D_MODEL), lambda i, ps, pc, u: (0, 0)),
        ],
        out_specs=pl.BlockSpec(memory_space=pl.ANY),
        scratch_shapes=[
            pltpu.VMEM((2, tm, D_MODEL), F32),
            pltpu.VMEM((tm, D_MODEL), F32),
            pltpu.SMEM((2 * tm,), jnp.int32),
            pltpu.SemaphoreType.DMA((2,)),
            pltpu.SemaphoreType.DMA(()),
            pltpu.SemaphoreType.DMA(()),
        ],
    )
    return pl.pallas_call(
        _dispatch_kernel,
        grid_spec=grid_spec,
        out_shape=jax.ShapeDtypeStruct((n_slots, D_MODEL), F32),
        compiler_params=_params("arbitrary"),
        name="moe_dispatch",
    )(pad_start, pad_count, used, pos, x, g)


def _expert_kernel(te_ref, used_ref, xs_ref, wg_ref, wu_ref, wo_ref, y_ref, hb_ref):
    del te_ref
    c = pl.program_id(1)
    in_use = pl.program_id(0) < used_ref[0]

    @pl.when(jnp.logical_not(in_use) & (c == 0))
    def _():
        y_ref[...] = jnp.zeros_like(y_ref)

    @pl.when(in_use)
    def _():
        @pl.when(c == 0)
        def _():
            hb_ref[...] = xs_ref[...].astype(BF16)
            y_ref[...] = jnp.zeros_like(y_ref)

        h = hb_ref[...]
        y = y_ref[...]
        for c0, c1 in _ff_chunks(wg_ref.shape[3], MOE_INNER_CHUNK):
            gate = jnp.dot(h, wg_ref[0, 0, :, c0:c1], preferred_element_type=F32)
            up = jnp.dot(h, wu_ref[0, 0, :, c0:c1], preferred_element_type=F32)
            a = (gate * jax.nn.sigmoid(gate) * up).astype(BF16)
            y = y + jnp.dot(a, wo_ref[0, 0, c0:c1, :], preferred_element_type=F32)
        y_ref[...] = y


def _experts(tile_expert, used, xs, w_in, w_out, layer):
    tm = MOE_ROW_TILE
    ck = MOE_FF_CHUNK
    n_ck = D_FF_EXPERT // ck
    n_slots = xs.shape[0]

    def chunk(i, c, te, used):
        return jnp.where(i < used[0], c, n_ck - 1)

    grid_spec = pltpu.PrefetchScalarGridSpec(
        num_scalar_prefetch=2,
        grid=(n_slots // tm, n_ck),
        in_specs=[
            pl.BlockSpec((tm, D_MODEL), lambda i, c, te, used: (jnp.minimum(i, used[0] - 1), 0)),
            pl.BlockSpec((1, 1, D_MODEL, ck), lambda i, c, te, used: (layer, te[i], 0, chunk(i, c, te, used))),
            pl.BlockSpec((1, 1, D_MODEL, ck),
                         lambda i, c, te, used: (layer, te[i], 0, n_ck + chunk(i, c, te, used))),
            pl.BlockSpec((1, 1, ck, D_MODEL), lambda i, c, te, used: (layer, te[i], chunk(i, c, te, used), 0)),
        ],
        out_specs=pl.BlockSpec((tm, D_MODEL), lambda i, c, te, used: (i, 0)),
        scratch_shapes=[pltpu.VMEM((tm, D_MODEL), BF16)],
    )
    return pl.pallas_call(
        _expert_kernel,
        grid_spec=grid_spec,
        out_shape=jax.ShapeDtypeStruct((n_slots, D_MODEL), F32),
        compiler_params=_params("arbitrary", "arbitrary"),
        name="moe_experts",
    )(tile_expert, used, xs, w_in, w_in, w_out)


def _combine_kernel(pos_hbm, y_hbm, x_ref, gate_ref, gf_ref, *rest, split):
    out_refs = rest[:1 if split is None else 2]
    rows_ref, pos_smem, sems, pos_sem = rest[len(out_refs):]
    i = pl.program_id(0)
    n_tiles = pl.num_programs(0)
    tm = x_ref.shape[0]

    def gather(tile, buf):
        pos_copy = _load_positions(pos_hbm, pos_smem, pos_sem, tile)
        pos_copy.start()
        pos_copy.wait()

        def issue(r, carry):
            for k in range(2):
                pltpu.make_async_copy(y_hbm.at[pl.ds(pos_smem[k * tm + r], 1)],
                                      rows_ref.at[buf, k, pl.ds(r, 1)], sems.at[buf]).start()
            return carry

        lax.fori_loop(0, tm, issue, 0, unroll=8)

    @pl.when(i == 0)
    def _():
        gather(0, 0)

    @pl.when(i + 1 < n_tiles)
    def _():
        gather(i + 1, (i + 1) % 2)

    buf = i % 2
    for k in range(2):
        pltpu.make_async_copy(y_hbm.at[pl.ds(0, tm)], rows_ref.at[buf, k], sems.at[buf]).wait()
    gates = gate_ref[...]
    out = x_ref[...] + (gates[:, 0:1] * rows_ref[buf, 0] + gates[:, 1:2] * rows_ref[buf, 1])
    if split is None:
        out_refs[0][...] = out
    else:
        out = _rms(out, gf_ref[...])

        @pl.when(i < split)
        def _():
            out_refs[0][...] = out

        @pl.when(i >= split)
        def _():
            out_refs[1][...] = out


def _combine(pos, y, x, gates, g_final, *, split):
    n_tok = x.shape[0]
    tm = ROW_TILE
    n_tiles = n_tok // tm
    if split is None:
        out_specs = [pl.BlockSpec((tm, D_MODEL), lambda i: (i, 0))]
        out_shape = [jax.ShapeDtypeStruct((n_tok, D_MODEL), F32)]
    else:
        out_specs = [pl.BlockSpec((tm, D_MODEL), lambda i: (jnp.minimum(i, split - 1), 0)),
                     pl.BlockSpec((tm, D_MODEL), lambda i: (jnp.maximum(i - split, 0), 0))]
        out_shape = [jax.ShapeDtypeStruct((split * tm, D_MODEL), F32),
                     jax.ShapeDtypeStruct(((n_tiles - split) * tm, D_MODEL), F32)]
    return pl.pallas_call(
        functools.partial(_combine_kernel, split=split),
        grid=(n_tiles,),
        in_specs=[
            pl.BlockSpec(memory_space=pl.ANY),
            pl.BlockSpec(memory_space=pl.ANY),
            pl.BlockSpec((tm, D_MODEL), lambda i: (i, 0)),
            pl.BlockSpec((tm, LANES), lambda i: (i, 0)),
            _resident((1, D_MODEL), lambda i: (0, 0)),
        ],
        out_specs=out_specs,
        out_shape=out_shape,
        scratch_shapes=[
            pltpu.VMEM((2, 2, tm, D_MODEL), F32),
            pltpu.SMEM((2 * tm,), jnp.int32),
            pltpu.SemaphoreType.DMA((2,)),
            pltpu.SemaphoreType.DMA(()),
        ],
        compiler_params=_params("arbitrary"),
        name="moe_combine",
    )(pos, y, x, gates, g_final)


def _moe(o, w_o, x, g, w_router, w_in, w_out, layer, g_final, *, split):
    x, meta, gates, counts = _router(o, w_o, x, g, w_router)
    pos, tile_expert, used, pad_start, pad_count, n_slots = _moe_plan(meta, counts)
    xs = _dispatch(pad_start, pad_count, used, pos, x, g, n_slots)
    y = _experts(tile_expert, used, xs, w_in, w_out, layer)
    return _combine(pos, y, x, gates, g_final, split=split)


def _rope_tables():
    t = jnp.arange(SEQ)
    row = (t // GRID_W).astype(F32)
    col = (t % GRID_W).astype(F32)
    half = HEAD_DIM // 2
    inv_freq = ROPE_THETA ** (-jnp.arange(0, half, 2, dtype=F32) / half)
    ang = jnp.concatenate([row[:, None] * inv_freq[None, :], col[:, None] * inv_freq[None, :]], axis=-1)
    cos = jnp.repeat(jnp.cos(ang), 2, axis=-1)
    sign = jnp.where(jnp.arange(HEAD_DIM) % 2 == 0, -1.0, 1.0).astype(F32)
    sin = jnp.repeat(jnp.sin(ang), 2, axis=-1) * sign
    reps = LANES // HEAD_DIM
    return jnp.tile(cos, (1, reps)), jnp.tile(sin, (1, reps))


def _t5_bucket(rel):
    half = N_BUCKETS // 2
    ret = jnp.where(rel > 0, half, 0)
    n = jnp.abs(rel)
    max_exact = half // 2
    nf = jnp.maximum(n, 1).astype(F32)
    large = max_exact + (jnp.log(nf / max_exact) / math.log(MAX_DISTANCE / max_exact)
                         * (half - max_exact)).astype(jnp.int32)
    large = jnp.minimum(large, half - 1)
    return ret + jnp.where(n < max_exact, n, large)


def _window_bias(rel_bias):
    r = jnp.arange(Q_BLOCK)[:, None]
    c = jnp.arange(Q_BLOCK + 2 * WINDOW)[None, :]
    rel = c - WINDOW - r
    bucket = _t5_bucket(rel)[:, :, None]
    bias = jnp.zeros(rel.shape + (N_Q_HEADS,), F32)
    for b in range(N_BUCKETS):
        bias = jnp.where(bucket == b, rel_bias[b].astype(F32)[None, None, :], bias)
    bias = bias * LOG2E
    bias = jnp.where((jnp.abs(rel) <= WINDOW)[:, :, None], bias, NEG_INF)
    bias = bias.reshape(Q_BLOCK, Q_BLOCK + 2 * WINDOW, N_KV_HEADS, GROUP).transpose(2, 1, 3, 0)
    bias = bias.reshape(N_KV_HEADS, Q_BLOCK + 2 * WINDOW, GROUP * Q_BLOCK)
    key = jnp.arange(Q_BLOCK + 2 * WINDOW)[None, :, None]
    no_left = jnp.where(key < WINDOW, NEG_INF, bias)
    no_right = jnp.where(key >= Q_BLOCK + WINDOW, NEG_INF, bias)
    return jnp.stack([bias, no_left, no_right])


def _swap_pairs(a):
    return a.reshape(a.shape[:-1] + (a.shape[-1] // 2, 2))[..., ::-1].reshape(a.shape)


def _axial_tables(cos, sin, q_gain, k_gain):
    def pair(gain, scale):
        g = jnp.tile(gain.astype(F32), LANES // HEAD_DIM)[None, :]
        return cos * (g * scale), sin * (_swap_pairs(g) * scale)

    return pair(q_gain, Q_SCALE) + pair(k_gain, 1.0)


def _axial_qkv_weights(w):
    return jnp.concatenate([w, _swap_pairs(w[:, :QK_WIDTH])], axis=1).astype(BF16)


def kernel(x_prompt, x_sample, norm_mix, norm_ffn, norm_final, w_qkv_a, q_gain_a, k_gain_a, w_o_a,
           w_qkv_b, sink_b, w_o_b, rel_bias, w_ff_in, w_ff_out, w_router, w_exp_in, w_exp_out):
    n_prompt = x_prompt.shape[0] * x_prompt.shape[1]
    x = (x_prompt.reshape(-1, D_MODEL), x_sample.reshape(-1, D_MODEL))
    cos, sin = _rope_tables()
    bias = _window_bias(rel_bias)
    w_exp_in_bf16 = w_exp_in.astype(BF16)
    w_exp_out_bf16 = w_exp_out.astype(BF16)
    for i in range(DEPTH):
        j = i // 2
        g_mix = norm_mix[i][None, :]
        g_ffn = norm_ffn[i][None, :]
        if i % 2 == 0:
            q, k, vt = _qkv_proj(x, g_mix, _axial_qkv_weights(w_qkv_a[j]),
                                 _axial_tables(cos, sin, q_gain_a[j], k_gain_a[j]))
            o = _global_attention(q, k, vt)
            x = _dense_ffn(o, w_o_a[j].astype(BF16), x, g_ffn, w_ff_in[j].astype(BF16), w_ff_out[j].astype(BF16))
        else:
            q, k, vt = _qkv_proj(x, g_mix, w_qkv_b[j].astype(BF16))
            o = _window_attention(q, k, vt, sink_b[j].astype(F32) * LOG2E, bias)
            last = i == DEPTH - 1
            x = _moe(o, w_o_b[j].astype(BF16), x, g_ffn, w_router[j], w_exp_in_bf16, w_exp_out_bf16, j,
                     norm_final[None, :],
                     split=n_prompt // ROW_TILE if last else None)
            if not last:
                x = x[0]
    y_prompt, y_sample = x
    return (y_prompt.reshape(x_prompt.shape), y_sample.reshape(x_sample.shape))
```

```python
import functools
import math

import jax
import jax.numpy as jnp
from jax import lax
from jax.experimental import pallas as pl
from jax.experimental.pallas import tpu as pltpu

D_MODEL = 1024
SEQ = 4096
DEPTH = 4
HEAD_DIM = 64
N_Q_HEADS = 16
N_KV_HEADS = 4
GROUP = N_Q_HEADS // N_KV_HEADS
Q_WIDTH = N_Q_HEADS * HEAD_DIM
KV_WIDTH = N_KV_HEADS * HEAD_DIM
QK_WIDTH = Q_WIDTH + KV_WIDTH
QKV_WIDTH = Q_WIDTH + 2 * KV_WIDTH
GRID_W = 64
ROPE_THETA = 10000.0
Q_BLOCK = 128
WINDOW = 128
N_BUCKETS = 32
MAX_DISTANCE = 128
D_FF = 2816
N_EXPERTS = 8
D_FF_EXPERT = 3584
EPS = 1e-6
NEG_INF = -1e30

LANES = 128
VMEM_LIMIT = 56 * 1024 * 1024

ROW_TILE = 512
WINDOW_Q_TILE = 512
WINDOW_LOOKAHEAD = 1
SAFE_SHIFT = 55.0
BOUND_SLACK = 1.0 + 2.0 ** -10
GLOBAL_KV_ROW_TILES = 1
SCORE_LOOKAHEAD = 1
GLOBAL_Q_TILE = 256
MOE_ROW_TILE = 1024
MOE_FF_CHUNK = 1792
MOE_INNER_CHUNK = 256
DENSE_FF_CHUNK = 1024
META_ROWS = 8

K_PAD_WIDTH = N_KV_HEADS * LANES
V_ONES = 16
LOG2E = math.log2(math.e)
Q_SCALE = HEAD_DIM ** -0.5 * LOG2E

F32 = jnp.float32
BF16 = jnp.bfloat16


def _params(*sem):
    return pltpu.CompilerParams(dimension_semantics=sem, vmem_limit_bytes=VMEM_LIMIT)


def _resident(shape, index_map):
    return pl.BlockSpec(shape, index_map, pipeline_mode=pl.Buffered(1))


def _rms(x, g):
    ms = jnp.mean(x * x, axis=-1, keepdims=True)
    return x * lax.rsqrt(ms + EPS) * g


def _store_k_v(k, v, k_ref, vt_ref):
    pad = jnp.zeros((k.shape[0], LANES - HEAD_DIM), BF16)
    for hd in range(N_KV_HEADS):
        k_ref[:, hd * LANES:hd * LANES + HEAD_DIM] = k[:, hd * HEAD_DIM:(hd + 1) * HEAD_DIM].astype(BF16)
        k_ref[:, hd * LANES + HEAD_DIM:(hd + 1) * LANES] = pad
    vt_ref[0, 0] = v.astype(BF16).T


def _qkv_window_kernel(x_ref, g_ref, w_ref, q_ref, k_ref, vt_ref):
    h = _rms(x_ref[...], g_ref[...]).astype(BF16)
    acc = jnp.dot(h, w_ref[...], preferred_element_type=F32)
    q_ref[...] = (acc[:, :Q_WIDTH] * Q_SCALE).astype(BF16)
    _store_k_v(acc[:, Q_WIDTH:QK_WIDTH], acc[:, QK_WIDTH:], k_ref, vt_ref)


def _x_specs(x, tm):
    if not isinstance(x, tuple):
        return [pl.BlockSpec((tm, D_MODEL), lambda i: (i, 0))]
    split = x[0].shape[0] // tm
    return [pl.BlockSpec((tm, D_MODEL), lambda i: (jnp.minimum(i, split - 1), 0)),
            pl.BlockSpec((tm, D_MODEL), lambda i: (jnp.maximum(i - split, 0), 0))]


def _x_split(x, tm):
    return x[0].shape[0] // tm if isinstance(x, tuple) else None


def _take_x(refs, split):
    if split is None:
        return refs[0][...], refs[1:]
    return jnp.where(pl.program_id(0) < split, refs[0][...], refs[1][...]), refs[2:]


def _qkv_axial_kernel(*refs, split):
    x, (g_ref, w_ref, cq_ref, sq_ref, ck_ref, sk_ref, q_ref, k_ref, vt_ref) = _take_x(refs, split)
    h = _rms(x, g_ref[...]).astype(BF16)
    acc = jnp.dot(h, w_ref[...], preferred_element_type=F32)
    low = lax.broadcasted_iota(jnp.int32, (acc.shape[0], LANES), 1) < HEAD_DIM

    def norm_rope(col, c, s):
        seg = acc[:, col:col + LANES]
        partner = acc[:, QKV_WIDTH + col:QKV_WIDTH + col + LANES]
        sq = seg * seg
        s_all = jnp.sum(sq, axis=-1, keepdims=True)
        s_lo = jnp.sum(jnp.where(low, sq, 0.0), axis=-1, keepdims=True)
        ms = jnp.where(low, s_lo, s_all - s_lo) * (1.0 / HEAD_DIM)
        return (seg * c + partner * s) * lax.rsqrt(ms + EPS)

    cq, sq_, ck, sk = cq_ref[...], sq_ref[...], ck_ref[...], sk_ref[...]
    for j in range(Q_WIDTH // LANES):
        q_ref[:, j * LANES:(j + 1) * LANES] = norm_rope(j * LANES, cq, sq_).astype(BF16)
    k = jnp.concatenate([norm_rope(Q_WIDTH + j * LANES, ck, sk) for j in range(KV_WIDTH // LANES)], axis=1)
    _store_k_v(k, acc[:, QK_WIDTH:QKV_WIDTH], k_ref, vt_ref)


def _qkv_proj(x, g, w, tables=None):
    xs = x if isinstance(x, tuple) else (x,)
    n_tok = sum(a.shape[0] for a in xs)
    tm = ROW_TILE
    tiles_per_seq = SEQ // tm
    n_seq = n_tok // SEQ
    axial = tables is not None
    tables = tuple(tables) if axial else ()
    body = functools.partial(_qkv_axial_kernel, split=_x_split(x, tm)) if axial else _qkv_window_kernel
    return pl.pallas_call(
        body,
        grid=(n_tok // tm,),
        in_specs=_x_specs(x, tm) + [
            _resident((1, D_MODEL), lambda i: (0, 0)),
            _resident(w.shape, lambda i: (0, 0)),
        ] + [pl.BlockSpec((tm, LANES), lambda i: (i % tiles_per_seq, 0)) for _ in tables],
        out_specs=[
            pl.BlockSpec((tm, Q_WIDTH), lambda i: (i, 0)),
            pl.BlockSpec((tm, K_PAD_WIDTH), lambda i: (i, 0)),
            pl.BlockSpec((1, 1, KV_WIDTH, tm), lambda i: (i // tiles_per_seq, i % tiles_per_seq, 0, 0)),
        ],
        out_shape=[
            jax.ShapeDtypeStruct((n_tok, Q_WIDTH), BF16),
            jax.ShapeDtypeStruct((n_tok, K_PAD_WIDTH), BF16),
            jax.ShapeDtypeStruct((n_seq, tiles_per_seq, KV_WIDTH, tm), BF16),
        ],
        compiler_params=_params("parallel"),
        name="qkv_axial" if axial else "qkv_window",
    )(*xs, g, w, *tables)


def _q_transposed(q_ref):
    return q_ref[...].T


def _group_queries(qt, h):
    base = h * GROUP * HEAD_DIM
    return jnp.concatenate(
        [qt[base + g * HEAD_DIM:base + (g + 1) * HEAD_DIM, :] for g in range(GROUP)], axis=1)


def _v_with_ones(vt_h, ones):
    return jnp.concatenate([vt_h, ones], axis=0)


def _store_heads(o_ref, outs, tq):
    rows = [o[:, g * tq:(g + 1) * tq] for o in outs for g in range(GROUP)]
    o_ref[...] = jnp.concatenate(rows, axis=0).astype(o_ref.dtype).T


def _global_attn_kernel(q_ref, k_ref, vt_ref, o_ref, s_ref, kmax_ref):
    tq = q_ref.shape[0]
    n_kv = vt_ref.shape[1] // GLOBAL_KV_ROW_TILES
    tk = vt_ref.shape[3] * GLOBAL_KV_ROW_TILES
    cols = GROUP * tq
    qt = _q_transposed(q_ref)
    queries = [_group_queries(qt, h) for h in range(N_KV_HEADS)]
    ones = jnp.ones((V_ONES, tk), BF16)

    @pl.when(pl.program_id(1) == 0)
    def _():
        def block_max(c, carry):
            rows = k_ref[pl.ds(pl.multiple_of(c * tk, tk), tk), :].astype(F32)
            sq = rows * rows
            return tuple(
                jnp.maximum(carry[h], jnp.max(jnp.sum(sq[:, h * LANES:(h + 1) * LANES], axis=1, keepdims=True),
                                              axis=0, keepdims=True))
                for h in range(N_KV_HEADS))

        k2max = lax.fori_loop(0, k_ref.shape[0] // tk, block_max,
                              tuple(jnp.zeros((1, 1), F32) for _ in range(N_KV_HEADS)))
        for h in range(N_KV_HEADS):
            kmax_ref[h] = jnp.broadcast_to(jnp.sqrt(k2max[h]), kmax_ref.shape[1:])

    qf = qt.astype(F32)

    def logit_bound(h):
        norms = []
        for g in range(GROUP):
            r0 = (h * GROUP + g) * HEAD_DIM
            blk = qf[r0:r0 + HEAD_DIM, :]
            norms.append(jnp.sqrt(jnp.sum(blk * blk, axis=0, keepdims=True)))
        return jnp.concatenate(norms, axis=1) * (kmax_ref[h][0:1, 0:1] * BOUND_SLACK)

    bounds = [logit_bound(h) for h in range(N_KV_HEADS)]
    bound_is_safe = jnp.max(jnp.concatenate(bounds, axis=0)) <= SAFE_SHIFT

    def scores(h, j):
        start = pl.multiple_of(j * tk, tk)
        k_h = k_ref[pl.ds(start, tk), h * LANES:h * LANES + HEAD_DIM]
        return jnp.dot(k_h, queries[h], preferred_element_type=F32)

    def values(h, j):
        vt_h = jnp.concatenate([vt_ref[0, j * GLOBAL_KV_ROW_TILES + t, h * HEAD_DIM:(h + 1) * HEAD_DIM, :]
                                for t in range(GLOBAL_KV_ROW_TILES)], axis=1)
        return _v_with_ones(vt_h, ones)

    def update_fixed(h, j, s, m, acc):
        p = jnp.exp2(s - m).astype(BF16)
        return m, acc + jnp.dot(values(h, j), p, preferred_element_type=F32)

    def update_online(h, j, s, m, acc):
        m_new = jnp.maximum(m, jnp.max(s, axis=0, keepdims=True))
        p = jnp.exp2(s - m_new).astype(BF16)
        alpha = jnp.exp2(m - m_new)
        return m_new, alpha * acc + jnp.dot(values(h, j), p, preferred_element_type=F32)

    ahead = s_ref.shape[0]
    for n in range(ahead):
        s_ref[n] = scores(n, 0)

    def attend(update, shifts):
        def body(j, carry):
            new = []
            in_flight = [s_ref[n] for n in range(ahead)]
            for h in range(N_KV_HEADS):
                h_next = h + ahead
                if h_next < N_KV_HEADS:
                    in_flight.append(scores(h_next, j))
                else:
                    in_flight.append(scores(h_next - N_KV_HEADS, jnp.minimum(j + 1, n_kv - 1)))
                new.append(update(h, j, in_flight.pop(0), *carry[h]))
            for n in range(ahead):
                s_ref[n] = in_flight[n]
            return tuple(new)

        init = tuple((shifts[h], jnp.zeros((HEAD_DIM + V_ONES, cols), F32)) for h in range(N_KV_HEADS))
        final = lax.fori_loop(0, n_kv, body, init, unroll=4)
        outs = [acc[:HEAD_DIM] / acc[HEAD_DIM:HEAD_DIM + 1] for _, acc in final]
        _store_heads(o_ref, outs, tq)

    @pl.when(bound_is_safe)
    def _():
        attend(update_fixed, bounds)

    @pl.when(jnp.logical_not(bound_is_safe))
    def _():
        attend(update_online, [jnp.full((1, cols), -jnp.inf, F32)] * N_KV_HEADS)


def _global_attention(q, k, vt):
    n_tok = q.shape[0]
    n_seq = n_tok // SEQ
    tq = GLOBAL_Q_TILE
    nq = SEQ // tq
    assert 1 <= SCORE_LOOKAHEAD < N_KV_HEADS and vt.shape[1] % GLOBAL_KV_ROW_TILES == 0
    return pl.pallas_call(
        _global_attn_kernel,
        grid=(n_seq, nq),
        in_specs=[
            pl.BlockSpec((tq, Q_WIDTH), lambda b, i: (b * nq + i, 0)),
            pl.BlockSpec((SEQ, K_PAD_WIDTH), lambda b, i: (b, 0)),
            pl.BlockSpec((1,) + vt.shape[1:], lambda b, i: (b, 0, 0, 0)),
        ],
        out_specs=pl.BlockSpec((tq, Q_WIDTH), lambda b, i: (b * nq + i, 0)),
        out_shape=jax.ShapeDtypeStruct((n_tok, Q_WIDTH), BF16),
        scratch_shapes=[pltpu.VMEM((SCORE_LOOKAHEAD, GLOBAL_KV_ROW_TILES * vt.shape[3], GROUP * tq), F32),
                        pltpu.VMEM((N_KV_HEADS, 8, LANES), F32)],
        compiler_params=_params("parallel", "arbitrary"),
        name="global_attention",
    )(q, k, vt)


def _window_attn_kernel(sink_ref, q_ref, kl_ref, kc_ref, kr_ref, vtl_ref, vtc_ref, vtr_ref, bias_ref, o_ref):
    i = pl.program_id(1)
    qb = Q_BLOCK
    n_sub = q_ref.shape[0] // qb
    n_keys = 3 * qb
    k_all = jnp.concatenate([kl_ref[...], kc_ref[...], kr_ref[...]], axis=0)
    vt_all = jnp.concatenate([vtl_ref[0, 0], vtc_ref[0, 0], vtr_ref[0, 0]], axis=1)
    qt = _q_transposed(q_ref)
    ones = jnp.ones((V_ONES, n_keys), BF16)
    variants = [jnp.where(i == 0, 1, 0)] + [0] * (n_sub - 2) + [jnp.where(i == pl.num_programs(1) - 1, 2, 0)]
    units = [(u, h) for u in range(n_sub) for h in range(N_KV_HEADS)]

    def scores(u, h):
        keys = k_all[u * qb:u * qb + n_keys, h * LANES:h * LANES + HEAD_DIM]
        return jnp.dot(keys, _group_queries(qt[:, u * qb:(u + 1) * qb], h), preferred_element_type=F32)

    outs = []
    in_flight = [scores(*unit) for unit in units[:WINDOW_LOOKAHEAD]]
    for n, (u, h) in enumerate(units):
        if n + WINDOW_LOOKAHEAD < len(units):
            in_flight.append(scores(*units[n + WINDOW_LOOKAHEAD]))
        s = in_flight.pop(0) + bias_ref[variants[u], h]
        sink = jnp.concatenate(
            [jnp.full((1, qb), sink_ref[h * GROUP + g], F32) for g in range(GROUP)], axis=1)
        m = jnp.maximum(jnp.max(s, axis=0, keepdims=True), sink)
        p = jnp.exp2(s - m).astype(BF16)
        v_h = _v_with_ones(vt_all[h * HEAD_DIM:(h + 1) * HEAD_DIM, u * qb:u * qb + n_keys], ones)
        acc = jnp.dot(v_h, p, preferred_element_type=F32)
        outs.append(acc[:HEAD_DIM] / (acc[HEAD_DIM:HEAD_DIM + 1] + jnp.exp2(sink - m)))
    blocks = []
    for u in range(n_sub):
        heads = outs[u * N_KV_HEADS:(u + 1) * N_KV_HEADS]
        blocks.append(jnp.concatenate([o[:, g * qb:(g + 1) * qb] for o in heads for g in range(GROUP)], axis=0))
    o_ref[...] = jnp.concatenate(blocks, axis=1).astype(o_ref.dtype).T


def _window_attention(q, k, vt, sinks, bias):
    n_tok = q.shape[0]
    n_seq = n_tok // SEQ
    qb = Q_BLOCK
    tq = WINDOW_Q_TILE
    nq = SEQ // tq
    nb = SEQ // qb
    sub = tq // qb
    assert sub >= 2, "the first and last query block of a step take different bias variants"
    assert tq % qb == 0 and ROW_TILE % tq == 0, "a step's centre keys must lie inside one stored V^T row tile"

    def edge_block(b, i, off):
        return jnp.clip(i * sub + off, 0, nb - 1)

    def k_edge(off):
        return pl.BlockSpec((qb, K_PAD_WIDTH), lambda b, i, s: (b * nb + edge_block(b, i, off), 0))

    def vt_edge(off):
        per = ROW_TILE // qb
        return pl.BlockSpec((1, 1, KV_WIDTH, qb),
                            lambda b, i, s: (b, edge_block(b, i, off) // per, 0, edge_block(b, i, off) % per))

    per_tile = ROW_TILE // tq
    grid_spec = pltpu.PrefetchScalarGridSpec(
        num_scalar_prefetch=1,
        grid=(n_seq, nq),
        in_specs=[
            pl.BlockSpec((tq, Q_WIDTH), lambda b, i, s: (b * nq + i, 0)),
            k_edge(-1),
            pl.BlockSpec((tq, K_PAD_WIDTH), lambda b, i, s: (b * nq + i, 0)),
            k_edge(sub),
            vt_edge(-1),
            pl.BlockSpec((1, 1, KV_WIDTH, tq), lambda b, i, s: (b, i // per_tile, 0, i % per_tile)),
            vt_edge(sub),
            _resident(bias.shape, lambda b, i, s: (0, 0, 0, 0)),
        ],
        out_specs=pl.BlockSpec((tq, Q_WIDTH), lambda b, i, s: (b * nq + i, 0)),
    )
    return pl.pallas_call(
        _window_attn_kernel,
        grid_spec=grid_spec,
        out_shape=jax.ShapeDtypeStruct((n_tok, Q_WIDTH), BF16),
        compiler_params=_params("parallel", "arbitrary"),
        name="window_attention",
    )(sinks, q, k, k, k, vt, vt, vt, bias)


def _attn_residual(o_ref, wo_ref, x):
    return x + jnp.dot(o_ref[...], wo_ref[...], preferred_element_type=F32)


def _attn_residual_specs(x, tm):
    return [
        pl.BlockSpec((tm, Q_WIDTH), lambda i: (i, 0)),
        _resident((Q_WIDTH, D_MODEL), lambda i: (0, 0)),
    ] + _x_specs(x, tm)


def _ff_chunks(d_ff, width):
    return [(c, min(c + width, d_ff)) for c in range(0, d_ff, width)]


def _dense_ffn_kernel(o_ref, wo_ref, *refs, split):
    x_in, (g_ref, w_in_ref, w_out_ref, y_ref) = _take_x(refs, split)
    x = _attn_residual(o_ref, wo_ref, x_in)
    h = _rms(x, g_ref[...]).astype(BF16)
    acc = x
    for c0, c1 in _ff_chunks(D_FF, DENSE_FF_CHUNK):
        gate = jnp.dot(h, w_in_ref[:, c0:c1], preferred_element_type=F32)
        up = jnp.dot(h, w_in_ref[:, D_FF + c0:D_FF + c1], preferred_element_type=F32)
        a = (gate * jax.nn.sigmoid(gate) * up).astype(BF16)
        acc = acc + jnp.dot(a, w_out_ref[c0:c1, :], preferred_element_type=F32)
    y_ref[...] = acc


def _dense_ffn(o, w_o, x, g, w_in, w_out):
    xs = x if isinstance(x, tuple) else (x,)
    n_tok = o.shape[0]
    tm = ROW_TILE
    return pl.pallas_call(
        functools.partial(_dense_ffn_kernel, split=_x_split(x, tm)),
        grid=(n_tok // tm,),
        in_specs=_attn_residual_specs(x, tm) + [
            _resident((1, D_MODEL), lambda i: (0, 0)),
            _resident((D_MODEL, 2 * D_FF), lambda i: (0, 0)),
            _resident((D_FF, D_MODEL), lambda i: (0, 0)),
        ],
        out_specs=pl.BlockSpec((tm, D_MODEL), lambda i: (i, 0)),
        out_shape=jax.ShapeDtypeStruct((n_tok, D_MODEL), F32),
        compiler_params=_params("parallel"),
        name="dense_ffn",
    )(o, w_o, *xs, g, w_in, w_out)


def _router_kernel(o_ref, wo_ref, x_ref, g_ref, wr_ref, earlier_ref, xn_ref, meta_ref, gate_ref, count_ref,
                   base_ref):
    @pl.when(pl.program_id(0) == 0)
    def _():
        base_ref[...] = jnp.zeros_like(base_ref)

    x = _attn_residual(o_ref, wo_ref, x_ref[...])
    xn_ref[...] = x
    hf = _rms(x, g_ref[...])
    h_hi = hf.astype(BF16)
    h_lo = (hf - h_hi.astype(F32)).astype(BF16)
    by_hi = jnp.dot(h_hi, wr_ref[...], preferred_element_type=F32)
    by_lo = jnp.dot(h_lo, wr_ref[:, :LANES], preferred_element_type=F32)
    logits = by_hi[:, :LANES] + (by_hi[:, LANES:] + by_lo)
    lane = lax.broadcasted_iota(jnp.int32, logits.shape, 1)
    logits = jnp.where(lane < N_EXPERTS, logits, -jnp.inf)
    m1 = jnp.max(logits, axis=-1, keepdims=True)
    i1 = jnp.min(jnp.where(logits == m1, lane, LANES), axis=-1, keepdims=True)
    rest = jnp.where(lane == i1, -jnp.inf, logits)
    m2 = jnp.max(rest, axis=-1, keepdims=True)
    i2 = jnp.min(jnp.where(rest == m2, lane, LANES), axis=-1, keepdims=True)
    e2 = jnp.exp(m2 - m1)
    g1 = 1.0 / (1.0 + e2)
    g2 = e2 / (1.0 + e2)

    chosen = jnp.where((lane == i1) | (lane == i2), 1.0, 0.0)
    before = jnp.dot(earlier_ref[...], chosen.astype(BF16), preferred_element_type=F32) + base_ref[...]
    r1 = jnp.sum(jnp.where(lane == i1, before, 0.0), axis=-1, keepdims=True)
    r2 = jnp.sum(jnp.where(lane == i2, before, 0.0), axis=-1, keepdims=True)
    meta = jnp.where(lane == 0, i1.astype(F32), jnp.where(lane == 1, i2.astype(F32),
                                                          jnp.where(lane == 2, r1, jnp.where(lane == 3, r2, 0.0))))
    meta_ref[...] = meta.T[:META_ROWS].astype(jnp.int32)
    gate_ref[...] = jnp.where(lane == 0, g1, jnp.where(lane == 1, g2, 0.0))
    base_ref[...] += jnp.sum(chosen, axis=0, keepdims=True)
    count_ref[...] = base_ref[...]


def _router(o, w_o, x, g, w_router):
    n_tok = x.shape[0]
    tm = ROW_TILE
    w_hi = w_router.astype(BF16)
    w_lo = (w_router - w_hi.astype(F32)).astype(BF16)
    pad = ((0, 0), (0, LANES - N_EXPERTS))
    wr = jnp.concatenate([jnp.pad(w_hi, pad), jnp.pad(w_lo, pad)], axis=1)
    earlier = jnp.tril(jnp.ones((tm, tm), BF16), -1)
    return pl.pallas_call(
        _router_kernel,
        grid=(n_tok // tm,),
        in_specs=_attn_residual_specs(x, tm) + [
            _resident((1, D_MODEL), lambda i: (0, 0)),
            _resident((D_MODEL, 2 * LANES), lambda i: (0, 0)),
            _resident((tm, tm), lambda i: (0, 0)),
        ],
        out_specs=[
            pl.BlockSpec((tm, D_MODEL), lambda i: (i, 0)),
            pl.BlockSpec((META_ROWS, tm), lambda i: (0, i)),
            pl.BlockSpec((tm, LANES), lambda i: (i, 0)),
            pl.BlockSpec((1, LANES), lambda i: (0, 0)),
        ],
        out_shape=[
            jax.ShapeDtypeStruct((n_tok, D_MODEL), F32),
            jax.ShapeDtypeStruct((META_ROWS, n_tok), jnp.int32),
            jax.ShapeDtypeStruct((n_tok, LANES), F32),
            jax.ShapeDtypeStruct((1, LANES), F32),
        ],
        scratch_shapes=[pltpu.VMEM((1, LANES), F32)],
        compiler_params=_params("arbitrary"),
        name="router",
    )(o, w_o, x, g, wr, earlier)


def _moe_plan(meta, counts):
    tm = MOE_ROW_TILE
    n_tok = meta.shape[1]
    n_tiles = (2 * n_tok) // tm + N_EXPERTS
    cnt = counts[0, :N_EXPERTS].astype(jnp.int32)
    tiles = (cnt + tm - 1) // tm
    tile_end = jnp.cumsum(tiles)
    offsets = (tile_end - tiles) * tm
    tile_ids = jnp.arange(n_tiles, dtype=jnp.int32)
    tile_expert = jnp.minimum(jnp.sum(tile_ids[:, None] >= tile_end[None, :], axis=1), N_EXPERTS - 1)
    used = tile_end[-1:].astype(jnp.int32)
    experts = jnp.arange(N_EXPERTS, dtype=jnp.int32)

    def slot(expert, rank):
        first = jnp.sum(jnp.where(expert[:, None] == experts[None, :], offsets[None, :], 0), axis=1)
        return first + rank

    pos1 = slot(meta[0], meta[2])
    pos2 = slot(meta[1], meta[3])
    pos = jnp.stack([pos1.reshape(-1, ROW_TILE), pos2.reshape(-1, ROW_TILE)], axis=1).reshape(-1)
    pad_start = (offsets + cnt).astype(jnp.int32)
    pad_count = (tiles * tm - cnt).astype(jnp.int32)
    return (pos.astype(jnp.int32), tile_expert.astype(jnp.int32), used, pad_start, pad_count, n_tiles * tm)


def _load_positions(pos_hbm, pos_smem, pos_sem, tile):
    n = pos_smem.shape[0]
    start = pl.multiple_of(tile * n, n)
    return pltpu.make_async_copy(pos_hbm.at[pl.ds(start, n)], pos_smem, pos_sem)


def _dispatch_kernel(pad_start_ref, pad_count_ref, used_ref, pos_hbm, x_ref, g_ref, xs_out, h_ref, zero_ref,
                     pos_smem, sems, pos_sem, pad_sem):
    i = pl.program_id(0)
    tm = h_ref.shape[1]
    buf = i % 2
    pos_copy = _load_positions(pos_hbm, pos_smem, pos_sem, i)
    pos_copy.start()
    h_ref[buf] = _rms(x_ref[...], g_ref[...])
    pos_copy.wait()

    @pl.when(i == 0)
    def _():
        zero_ref[...] = jnp.zeros_like(zero_ref)

        def zero_copy(slot):
            return pltpu.make_async_copy(zero_ref.at[pl.ds(0, 1)], xs_out.at[pl.ds(slot, 1)], pad_sem)

        for e in range(N_EXPERTS):
            def fill(r, carry):
                zero_copy(pad_start_ref[e] + r).start()
                return carry

            lax.fori_loop(0, pad_count_ref[e], fill, 0)

        tail_start = used_ref[0] * MOE_ROW_TILE
        n_tail = (xs_out.shape[0] - tail_start) // tm

        def tail_copy(t):
            start = pl.multiple_of(tail_start + t * tm, tm)
            return pltpu.make_async_copy(zero_ref, xs_out.at[pl.ds(start, tm)], pad_sem)

        def fill_tail(t, carry):
            tail_copy(t).start()
            return carry

        lax.fori_loop(0, n_tail, fill_tail, 0)
        for e in range(N_EXPERTS):
            def fill_done(r, carry):
                zero_copy(pad_start_ref[e] + r).wait()
                return carry

            lax.fori_loop(0, pad_count_ref[e], fill_done, 0)

        def tail_done(t, carry):
            tail_copy(t).wait()
            return carry

        lax.fori_loop(0, n_tail, tail_done, 0)

    def issue(r, carry):
        for k in range(2):
            pltpu.make_async_copy(h_ref.at[buf, pl.ds(r, 1)], xs_out.at[pl.ds(pos_smem[k * tm + r], 1)],
                                  sems.at[buf]).start()
        return carry

    lax.fori_loop(0, tm, issue, 0, unroll=8)

    def drain(b):
        for _ in range(2):
            pltpu.make_async_copy(h_ref.at[b], xs_out.at[pl.ds(0, tm)], sems.at[b]).wait()

    @pl.when(i > 0)
    def _():
        drain(1 - buf)

    @pl.when(i == pl.num_programs(0) - 1)
    def _():
        drain(buf)


def _dispatch(pad_start, pad_count, used, pos, x, g, n_slots):
    n_tok = x.shape[0]
    tm = ROW_TILE
    grid_spec = pltpu.PrefetchScalarGridSpec(
        num_scalar_prefetch=3,
        grid=(n_tok // tm,),
        in_specs=[
            pl.BlockSpec(memory_space=pl.ANY),
            pl.BlockSpec((tm, D_MODEL), lambda i, ps, pc, u: (i, 0)),
            _resident((1, D_MODEL), lambda i, ps, pc, u: (0, 0)),
        ],
        out_specs=pl.BlockSpec(memory_space=pl.ANY),
        scratch_shapes=[
            pltpu.VMEM((2, tm, D_MODEL), F32),
            pltpu.VMEM((tm, D_MODEL), F32),
            pltpu.SMEM((2 * tm,), jnp.int32),
            pltpu.SemaphoreType.DMA((2,)),
            pltpu.SemaphoreType.DMA(()),
            pltpu.SemaphoreType.DMA(()),
        ],
    )
    return pl.pallas_call(
        _dispatch_kernel,
        grid_spec=grid_spec,
        out_shape=jax.ShapeDtypeStruct((n_slots, D_MODEL), F32),
        compiler_params=_params("arbitrary"),
        name="moe_dispatch",
    )(pad_start, pad_count, used, pos, x, g)


def _expert_kernel(te_ref, used_ref, xs_ref, wg_ref, wu_ref, wo_ref, y_ref, hb_ref):
    del te_ref
    c = pl.program_id(1)
    in_use = pl.program_id(0) < used_ref[0]

    @pl.when(jnp.logical_not(in_use) & (c == 0))
    def _():
        y_ref[...] = jnp.zeros_like(y_ref)

    @pl.when(in_use)
    def _():
        @pl.when(c == 0)
        def _():
            hb_ref[...] = xs_ref[...].astype(BF16)
            y_ref[...] = jnp.zeros_like(y_ref)

        h = hb_ref[...]
        y = y_ref[...]
        for c0, c1 in _ff_chunks(wg_ref.shape[3], MOE_INNER_CHUNK):
            gate = jnp.dot(h, wg_ref[0, 0, :, c0:c1], preferred_element_type=F32)
            up = jnp.dot(h, wu_ref[0, 0, :, c0:c1], preferred_element_type=F32)
            a = (gate * jax.nn.sigmoid(gate) * up).astype(BF16)
            y = y + jnp.dot(a, wo_ref[0, 0, c0:c1, :], preferred_element_type=F32)
        y_ref[...] = y


def _experts(tile_expert, used, xs, w_in, w_out, layer):
    tm = MOE_ROW_TILE
    ck = MOE_FF_CHUNK
    n_ck = D_FF_EXPERT // ck
    n_slots = xs.shape[0]

    def chunk(i, c, te, used):
        return jnp.where(i < used[0], c, n_ck - 1)

    grid_spec = pltpu.PrefetchScalarGridSpec(
        num_scalar_prefetch=2,
        grid=(n_slots // tm, n_ck),
        in_specs=[
            pl.BlockSpec((tm, D_MODEL), lambda i, c, te, used: (jnp.minimum(i, used[0] - 1), 0)),
            pl.BlockSpec((1, 1, D_MODEL, ck), lambda i, c, te, used: (layer, te[i], 0, chunk(i, c, te, used))),
            pl.BlockSpec((1, 1, D_MODEL, ck),
                         lambda i, c, te, used: (layer, te[i], 0, n_ck + chunk(i, c, te, used))),
            pl.BlockSpec((1, 1, ck, D_MODEL), lambda i, c, te, used: (layer, te[i], chunk(i, c, te, used), 0)),
        ],
        out_specs=pl.BlockSpec((tm, D_MODEL), lambda i, c, te, used: (i, 0)),
        scratch_shapes=[pltpu.VMEM((tm, D_MODEL), BF16)],
    )
    return pl.pallas_call(
        _expert_kernel,
        grid_spec=grid_spec,
        out_shape=jax.ShapeDtypeStruct((n_slots, D_MODEL), F32),
        compiler_params=_params("arbitrary", "arbitrary"),
        name="moe_experts",
    )(tile_expert, used, xs, w_in, w_in, w_out)


def _combine_kernel(pos_hbm, y_hbm, x_ref, gate_ref, gf_ref, *rest, split):
    out_refs = rest[:1 if split is None else 2]
    rows_ref, pos_smem, sems, pos_sem = rest[len(out_refs):]
    i = pl.program_id(0)
    n_tiles = pl.num_programs(0)
    tm = x_ref.shape[0]

    def gather(tile, buf):
        pos_copy = _load_positions(pos_hbm, pos_smem, pos_sem, tile)
        pos_copy.start()
        pos_copy.wait()

        def issue(r, carry):
            for k in range(2):
                pltpu.make_async_copy(y_hbm.at[pl.ds(pos_smem[k * tm + r], 1)],
                                      rows_ref.at[buf, k, pl.ds(r, 1)], sems.at[buf]).start()
            return carry

        lax.fori_loop(0, tm, issue, 0, unroll=8)

    @pl.when(i == 0)
    def _():
        gather(0, 0)

    @pl.when(i + 1 < n_tiles)
    def _():
        gather(i + 1, (i + 1) % 2)

    buf = i % 2
    for k in range(2):
        pltpu.make_async_copy(y_hbm.at[pl.ds(0, tm)], rows_ref.at[buf, k], sems.at[buf]).wait()
    gates = gate_ref[...]
    out = x_ref[...] + (gates[:, 0:1] * rows_ref[buf, 0] + gates[:, 1:2] * rows_ref[buf, 1])
    if split is None:
        out_refs[0][...] = out
    else:
        out = _rms(out, gf_ref[...])

        @pl.when(i < split)
        def _():
            out_refs[0][...] = out

        @pl.when(i >= split)
        def _():
            out_refs[1][...] = out


def _combine(pos, y, x, gates, g_final, *, split):
    n_tok = x.shape[0]
    tm = ROW_TILE
    n_tiles = n_tok // tm
    if split is None:
        out_specs = [pl.BlockSpec((tm, D_MODEL), lambda i: (i, 0))]
        out_shape = [jax.ShapeDtypeStruct((n_tok, D_MODEL), F32)]
    else:
        out_specs = [pl.BlockSpec((tm, D_MODEL), lambda i: (jnp.minimum(i, split - 1), 0)),
                     pl.BlockSpec((tm, D_MODEL), lambda i: (jnp.maximum(i - split, 0), 0))]
        out_shape = [jax.ShapeDtypeStruct((split * tm, D_MODEL), F32),
                     jax.ShapeDtypeStruct(((n_tiles - split) * tm, D_MODEL), F32)]
    return pl.pallas_call(
        functools.partial(_combine_kernel, split=split),
        grid=(n_tiles,),
        in_specs=[
            pl.BlockSpec(memory_space=pl.ANY),
            pl.BlockSpec(memory_space=pl.ANY),
            pl.BlockSpec((tm, D_MODEL), lambda i: (i, 0)),
            pl.BlockSpec((tm, LANES), lambda i: (i, 0)),
            _resident((1, D_MODEL), lambda i: (0, 0)),
        ],
        out_specs=out_specs,
        out_shape=out_shape,
        scratch_shapes=[
            pltpu.VMEM((2, 2, tm, D_MODEL), F32),
            pltpu.SMEM((2 * tm,), jnp.int32),
            pltpu.SemaphoreType.DMA((2,)),
            pltpu.SemaphoreType.DMA(()),
        ],
        compiler_params=_params("arbitrary"),
        name="moe_combine",
    )(pos, y, x, gates, g_final)


def _moe(o, w_o, x, g, w_router, w_in, w_out, layer, g_final, *, split):
    x, meta, gates, counts = _router(o, w_o, x, g, w_router)
    pos, tile_expert, used, pad_start, pad_count, n_slots = _moe_plan(meta, counts)
    xs = _dispatch(pad_start, pad_count, used, pos, x, g, n_slots)
    y = _experts(tile_expert, used, xs, w_in, w_out, layer)
    return _combine(pos, y, x, gates, g_final, split=split)


def _rope_tables():
    t = jnp.arange(SEQ)
    row = (t // GRID_W).astype(F32)
    col = (t % GRID_W).astype(F32)
    half = HEAD_DIM // 2
    inv_freq = ROPE_THETA ** (-jnp.arange(0, half, 2, dtype=F32) / half)
    ang = jnp.concatenate([row[:, None] * inv_freq[None, :], col[:, None] * inv_freq[None, :]], axis=-1)
    cos = jnp.repeat(jnp.cos(ang), 2, axis=-1)
    sign = jnp.where(jnp.arange(HEAD_DIM) % 2 == 0, -1.0, 1.0).astype(F32)
    sin = jnp.repeat(jnp.sin(ang), 2, axis=-1) * sign
    reps = LANES // HEAD_DIM
    return jnp.tile(cos, (1, reps)), jnp.tile(sin, (1, reps))


def _t5_bucket(rel):
    half = N_BUCKETS // 2
    ret = jnp.where(rel > 0, half, 0)
    n = jnp.abs(rel)
    max_exact = half // 2
    nf = jnp.maximum(n, 1).astype(F32)
    large = max_exact + (jnp.log(nf / max_exact) / math.log(MAX_DISTANCE / max_exact)
                         * (half - max_exact)).astype(jnp.int32)
    large = jnp.minimum(large, half - 1)
    return ret + jnp.where(n < max_exact, n, large)


def _window_bias(rel_bias):
    r = jnp.arange(Q_BLOCK)[:, None]
    c = jnp.arange(Q_BLOCK + 2 * WINDOW)[None, :]
    rel = c - WINDOW - r
    bucket = _t5_bucket(rel)[:, :, None]
    bias = jnp.zeros(rel.shape + (N_Q_HEADS,), F32)
    for b in range(N_BUCKETS):
        bias = jnp.where(bucket == b, rel_bias[b].astype(F32)[None, None, :], bias)
    bias = bias * LOG2E
    bias = jnp.where((jnp.abs(rel) <= WINDOW)[:, :, None], bias, NEG_INF)
    bias = bias.reshape(Q_BLOCK, Q_BLOCK + 2 * WINDOW, N_KV_HEADS, GROUP).transpose(2, 1, 3, 0)
    bias = bias.reshape(N_KV_HEADS, Q_BLOCK + 2 * WINDOW, GROUP * Q_BLOCK)
    key = jnp.arange(Q_BLOCK + 2 * WINDOW)[None, :, None]
    no_left = jnp.where(key < WINDOW, NEG_INF, bias)
    no_right = jnp.where(key >= Q_BLOCK + WINDOW, NEG_INF, bias)
    return jnp.stack([bias, no_left, no_right])


def _swap_pairs(a):
    return a.reshape(a.shape[:-1] + (a.shape[-1] // 2, 2))[..., ::-1].reshape(a.shape)


def _axial_tables(cos, sin, q_gain, k_gain):
    def pair(gain, scale):
        g = jnp.tile(gain.astype(F32), LANES // HEAD_DIM)[None, :]
        return cos * (g * scale), sin * (_swap_pairs(g) * scale)

    return pair(q_gain, Q_SCALE) + pair(k_gain, 1.0)


def _axial_qkv_weights(w):
    return jnp.concatenate([w, _swap_pairs(w[:, :QK_WIDTH])], axis=1).astype(BF16)


def kernel(x_prompt, x_sample, norm_mix, norm_ffn, norm_final, w_qkv_a, q_gain_a, k_gain_a, w_o_a,
           w_qkv_b, sink_b, w_o_b, rel_bias, w_ff_in, w_ff_out, w_router, w_exp_in, w_exp_out):
    n_prompt = x_prompt.shape[0] * x_prompt.shape[1]
    x = (x_prompt.reshape(-1, D_MODEL), x_sample.reshape(-1, D_MODEL))
    cos, sin = _rope_tables()
    bias = _window_bias(rel_bias)
    w_exp_in_bf16 = w_exp_in.astype(BF16)
    w_exp_out_bf16 = w_exp_out.astype(BF16)
    for i in range(DEPTH):
        j = i // 2
        g_mix = norm_mix[i][None, :]
        g_ffn = norm_ffn[i][None, :]
        if i % 2 == 0:
            q, k, vt = _qkv_proj(x, g_mix, _axial_qkv_weights(w_qkv_a[j]),
                                 _axial_tables(cos, sin, q_gain_a[j], k_gain_a[j]))
            o = _global_attention(q, k, vt)
            x = _dense_ffn(o, w_o_a[j].astype(BF16), x, g_ffn, w_ff_in[j].astype(BF16), w_ff_out[j].astype(BF16))
        else:
            q, k, vt = _qkv_proj(x, g_mix, w_qkv_b[j].astype(BF16))
            o = _window_attention(q, k, vt, sink_b[j].astype(F32) * LOG2E, bias)
            last = i == DEPTH - 1
            x = _moe(o, w_o_b[j].astype(BF16), x, g_ffn, w_router[j], w_exp_in_bf16, w_exp_out_bf16, j,
                     norm_final[None, :],
                     split=n_prompt // ROW_TILE if last else None)
            if not last:
                x = x[0]
    y_prompt, y_sample = x
    return (y_prompt.reshape(x_prompt.shape), y_sample.reshape(x_sample.shape))
```
